```python
import jax, jax.numpy as jnp
from jax import lax
import numpy as np

D_MODEL = 2048
BATCH = 2
SEQ = 4096
DEPTH = 2

CHUNK = 64
N_MIXERS = 2
N_POOL_LAYERS = (DEPTH + 1) // 2
N_RWKV_LAYERS = DEPTH // 2

POOL_WINDOWS = (2, 4, 8, 16)
N_POOL_GROUPS = len(POOL_WINDOWS)
POOL_GROUP_DIM = D_MODEL // N_POOL_GROUPS

HEAD_SIZE = 64
N_HEADS = D_MODEL // HEAD_SIZE
DECAY_LORA = max(32, int(round(1.8 * D_MODEL ** 0.5 / 32)) * 32)
AAA_LORA = max(32, int(round(1.8 * D_MODEL ** 0.5 / 32)) * 32)
GATE_LORA = max(32, int(round(0.6 * D_MODEL ** 0.8 / 32)) * 32)
N_SHIFT_MIX = 6
GN_EPS = 64e-5
L2_EPS = 1e-12

D_FF = -(-8 * D_MODEL // (3 * 256)) * 256
RMS_EPS = 1e-6

kernel_name = 'hybrid_pool_rwkv7_encoder'


def rms_norm(x, g):
    xf = x.astype(jnp.float32)
    y = xf * lax.rsqrt(jnp.mean(xf * xf, axis=-1, keepdims=True) + RMS_EPS)
    return (y * g.astype(jnp.float32)).astype(x.dtype)


def swiglu_ffn(h, w1, w3, w2):
    return (jax.nn.silu(h @ w1) * (h @ w3)) @ w2


def multiscale_pool_mixer(h, w_grp, b_grp, scale):
    B, S, D = h.shape
    hf = h.astype(jnp.float32)
    cs = jnp.cumsum(hf, axis=1)
    t = jnp.arange(S)
    groups = []
    for g, win in enumerate(POOL_WINDOWS):
        csg = cs[:, :, g * POOL_GROUP_DIM:(g + 1) * POOL_GROUP_DIM]
        lagged = jnp.pad(csg, ((0, 0), (win, 0), (0, 0)))[:, :S]
        count = jnp.minimum(t + 1, win).astype(jnp.float32)[None, :, None]
        groups.append((csg - lagged) / count - hf[:, :, g * POOL_GROUP_DIM:(g + 1) * POOL_GROUP_DIM])
    pooled = jnp.stack(groups, axis=2)
    mixed = jnp.einsum('bsgc,gcd->bsgd', pooled, w_grp.astype(jnp.float32)) + b_grp.astype(jnp.float32)
    return (mixed.reshape(B, S, D) * scale.astype(jnp.float32)).astype(h.dtype)


def rwkv7_time_mix(h, mu, w_r, w_k, w_v, w_o, w0, w_la, w_lb, a0, a_la, a_lb,
                   g_la, g_lb, k_k, k_a, r_k, lnx_w, lnx_b):
    B, S, D = h.shape
    f32 = jnp.float32
    xx = jnp.pad(h, ((0, 0), (1, 0), (0, 0)))[:, :S] - h
    xr, xw, xk, xv, xa, xg = (h + xx * mu[j] for j in range(N_SHIFT_MIX))
    r = xr @ w_r
    w_log = -jax.nn.softplus(-(w0 + jnp.tanh(xw @ w_la) @ w_lb)) - 0.5
    k = xk @ w_k
    v = xv @ w_v
    a = jax.nn.sigmoid(a0 + (xa @ a_la) @ a_lb)
    g = jax.nn.sigmoid(xg @ g_la) @ g_lb

    def heads(t):
        return t.astype(f32).reshape(B, S, N_HEADS, HEAD_SIZE)

    kk = heads(k * k_k)
    kk = kk / jnp.maximum(jnp.sqrt(jnp.sum(kk * kk, axis=-1, keepdims=True)), L2_EPS)
    k = k * (1 + (a - 1) * k_a)
    rh, kh, vh, ah = heads(r), heads(k), heads(v), heads(a)
    decay = jnp.exp(-jnp.exp(heads(w_log)))
    seq_in = tuple(jnp.moveaxis(t, 1, 0) for t in (rh, decay, kh, vh, -kk, kk * ah))

    def step(state, inp):
        r_t, w_t, k_t, v_t, a_t, b_t = inp
        sa = jnp.einsum('bhvk,bhk->bhv', state, a_t)
        state = (state * w_t[:, :, None, :] + sa[..., None] * b_t[:, :, None, :]
                 + v_t[..., None] * k_t[:, :, None, :])
        return state, jnp.einsum('bhvk,bhk->bhv', state, r_t)

    state0 = jnp.zeros((B, N_HEADS, HEAD_SIZE, HEAD_SIZE), f32)
    _, y = lax.scan(step, state0, seq_in)
    y = jnp.moveaxis(y, 0, 1)
    mean = jnp.mean(y, axis=-1, keepdims=True)
    var = jnp.mean(jnp.square(y - mean), axis=-1, keepdims=True)
    y = ((y - mean) * lax.rsqrt(var + GN_EPS)).reshape(B, S, D) * lnx_w.astype(f32) + lnx_b.astype(f32)
    bonus = jnp.sum(rh * kh * r_k.astype(f32), axis=-1, keepdims=True) * vh
    y = (y + bonus.reshape(B, S, D)).astype(h.dtype)
    return (y * g) @ w_o


def setup_inputs(seed: int = 0) -> dict:
    key = jax.random.key(seed)
    ks = jax.random.split(key, 32)
    f32 = jnp.float32
    D, C, G, H, N, F = D_MODEL, POOL_GROUP_DIM, N_POOL_GROUPS, N_HEADS, HEAD_SIZE, D_FF
    NP, NR, L = N_POOL_LAYERS, N_RWKV_LAYERS, DEPTH

    def nrm(k, shape, scale):
        return jax.random.normal(k, shape, f32) * scale

    def unif(k, shape, lo, hi):
        return jax.random.uniform(k, shape, f32, lo, hi)

    return {
        'x': nrm(ks[0], (BATCH, SEQ, D), 1.0),
        'norm1_g': 1.0 + nrm(ks[1], (L, D), 0.02),
        'norm2_g': 1.0 + nrm(ks[2], (L, D), 0.02),
        'final_g': 1.0 + nrm(ks[3], (D,), 0.02),
        'pool_w': nrm(ks[4], (NP, G, C, C), C ** -0.5),
        'pool_b': nrm(ks[5], (NP, G, C), 0.02),
        'pool_scale': unif(ks[6], (NP, D), 0.1, 0.5),
        'rw_mu': unif(ks[7], (NR, N_SHIFT_MIX, D), 0.0, 1.0),
        'rw_r': nrm(ks[8], (NR, D, D), D ** -0.5),
        'rw_k': nrm(ks[9], (NR, D, D), D ** -0.5),
        'rw_v': nrm(ks[10], (NR, D, D), D ** -0.5),
        'rw_o': nrm(ks[11], (NR, D, D), D ** -0.5 * 0.5),
        'rw_w0': unif(ks[12], (NR, D), -5.0, 1.0),
        'rw_w_la': nrm(ks[13], (NR, D, DECAY_LORA), D ** -0.5),
        'rw_w_lb': nrm(ks[14], (NR, DECAY_LORA, D), DECAY_LORA ** -0.5 * 0.5),
        'rw_a0': nrm(ks[15], (NR, D), 0.5),
        'rw_a_la': nrm(ks[16], (NR, D, AAA_LORA), D ** -0.5),
        'rw_a_lb': nrm(ks[17], (NR, AAA_LORA, D), AAA_LORA ** -0.5 * 0.5),
        'rw_g_la': nrm(ks[18], (NR, D, GATE_LORA), D ** -0.5),
        'rw_g_lb': nrm(ks[19], (NR, GATE_LORA, D), GATE_LORA ** -0.5),
        'rw_k_k': 0.85 + nrm(ks[20], (NR, D), 0.05),
        'rw_k_a': 1.0 + nrm(ks[21], (NR, D), 0.05),
        'rw_r_k': nrm(ks[22], (NR, H, N), 0.1),
        'rw_lnx_w': 1.0 + nrm(ks[23], (NR, D), 0.02),
        'rw_lnx_b': nrm(ks[24], (NR, D), 0.02),
        'ffn_w1': nrm(ks[25], (L, D, F), D ** -0.5),
        'ffn_w3': nrm(ks[26], (L, D, F), D ** -0.5),
        'ffn_w2': nrm(ks[27], (L, F, D), F ** -0.5 * 0.5),
    }


def reference(x, norm1_g, norm2_g, final_g, pool_w, pool_b, pool_scale,
              rw_mu, rw_r, rw_k, rw_v, rw_o, rw_w0, rw_w_la, rw_w_lb,
              rw_a0, rw_a_la, rw_a_lb, rw_g_la, rw_g_lb, rw_k_k, rw_k_a, rw_r_k,
              rw_lnx_w, rw_lnx_b, ffn_w1, ffn_w3, ffn_w2):
    h = x
    for i in range(DEPTH):
        hn = rms_norm(h, norm1_g[i])
        j = i // N_MIXERS
        if i % N_MIXERS == 0:
            h = h + multiscale_pool_mixer(hn, pool_w[j], pool_b[j], pool_scale[j])
        else:
            h = h + rwkv7_time_mix(hn, rw_mu[j], rw_r[j], rw_k[j], rw_v[j], rw_o[j],
                                   rw_w0[j], rw_w_la[j], rw_w_lb[j],
                                   rw_a0[j], rw_a_la[j], rw_a_lb[j],
                                   rw_g_la[j], rw_g_lb[j], rw_k_k[j], rw_k_a[j], rw_r_k[j],
                                   rw_lnx_w[j], rw_lnx_b[j])
        h = h + swiglu_ffn(rms_norm(h, norm2_g[i]), ffn_w1[i], ffn_w3[i], ffn_w2[i])
    return rms_norm(h, final_g)
```

```python
import functools

import jax
import jax.numpy as jnp
from jax import lax
from jax.experimental import pallas as pl
from jax.experimental.pallas import tpu as pltpu

F32 = jnp.float32
BF16 = jnp.bfloat16

RMS_EPS = 1e-6
GN_EPS = 64e-5
L2_EPS = 1e-12
POOL_WINDOWS = (2, 4, 8, 16)
POOL_HALO = 16
HEAD = 64
LANES = 128
CHUNK = 64
VMEM_LIMIT = 56 * 1024 * 1024


def _rms(x, g):
    return x * lax.rsqrt(jnp.mean(x * x, axis=-1, keepdims=True) + RMS_EPS) * g


def _dot(a, b):
    return jnp.dot(a, b, preferred_element_type=F32)


def _dot_nt(a, b):
    return lax.dot_general(a, b, (((1,), (1,)), ((), ())), preferred_element_type=F32)


def _split_dot(x, w, pieces=2):
    acc = None
    rem = x
    for _ in range(pieces):
        p = rem.astype(BF16)
        rem = rem - p.astype(F32)
        t = _dot(p, w)
        acc = t if acc is None else acc + t
    return acc


def _params(sem):
    return pltpu.CompilerParams(dimension_semantics=sem, vmem_limit_bytes=VMEM_LIMIT)


def _pool_kernel(x_ref, xprev_ref, g_ref, w_ref, b_ref, sc_ref, o_ref, *, ts, gdim):
    i = pl.program_id(1)
    g = g_ref[...]
    xc = x_ref[0]
    hn = _rms(xc, g)
    hp = _rms(xprev_ref[0], g)
    hp = jnp.where(i == 0, 0.0, hp)
    ext = jnp.concatenate([hp, hn], axis=0)
    rows = ts + POOL_HALO
    tau = lax.broadcasted_iota(jnp.int32, (ts, gdim), 0) + i * ts
    for gi, win in enumerate(POOL_WINDOWS):
        lo = gi * gdim
        e = ext[:, lo:lo + gdim]
        acc = e
        step = 1
        while step < win:
            acc = acc + pltpu.roll(acc, step, axis=0)
            step *= 2
        wsum = acc[POOL_HALO:rows]
        cnt = jnp.minimum(tau + 1, win).astype(F32)
        pooled = wsum / cnt - hn[:, lo:lo + gdim]
        mixed = _dot(pooled.astype(BF16), w_ref[gi]) + b_ref[:, lo:lo + gdim]
        o_ref[0, :, lo:lo + gdim] = xc[:, lo:lo + gdim] + mixed * sc_ref[:, lo:lo + gdim]


def _pool_layer(x, g, w_bf, b, scale, *, ts=512):
    B, S, D = x.shape
    G, C, _ = w_bf.shape
    ts = min(ts, S)
    halo_blocks = ts // POOL_HALO
    return pl.pallas_call(
        functools.partial(_pool_kernel, ts=ts, gdim=C),
        out_shape=jax.ShapeDtypeStruct((B, S, D), F32),
        grid=(B, S // ts),
        in_specs=[
            pl.BlockSpec((1, ts, D), lambda b, i: (b, i, 0)),
            pl.BlockSpec((1, POOL_HALO, D), lambda b, i: (b, jnp.maximum(i * halo_blocks - 1, 0), 0)),
            pl.BlockSpec((1, D), lambda b, i: (0, 0)),
            pl.BlockSpec((G, C, C), lambda b, i: (0, 0, 0)),
            pl.BlockSpec((1, D), lambda b, i: (0, 0)),
            pl.BlockSpec((1, D), lambda b, i: (0, 0)),
        ],
        out_specs=pl.BlockSpec((1, ts, D), lambda b, i: (b, i, 0)),
        compiler_params=_params(("parallel", "parallel")),
        name="pool_mixer",
    )(x, x, g.reshape(1, D), w_bf, b.reshape(1, D), scale.reshape(1, D))


def _ffn_kernel(h_ref, g_ref, w1_ref, w3_ref, w2_ref, fg_ref, o_ref, hn_ref, *, n_f, final_norm, tn):
    f = pl.program_id(1)

    @pl.when(f == 0)
    def _():
        hn_ref[...] = _rms(h_ref[...], g_ref[...]).astype(BF16)

    hn = hn_ref[...]
    a = _dot(hn, w1_ref[...])
    b = _dot(hn, w3_ref[...])
    act = (a * jax.nn.sigmoid(a) * b).astype(BF16)
    d = o_ref.shape[1]

    @pl.when(f == 0)
    def _():
        for n in range(0, d, tn):
            o_ref[:, n:n + tn] = h_ref[:, n:n + tn] + _dot(act, w2_ref[:, n:n + tn])

    @pl.when(f > 0)
    def _():
        for n in range(0, d, tn):
            o_ref[:, n:n + tn] += _dot(act, w2_ref[:, n:n + tn])

    if final_norm:
        @pl.when(f == n_f - 1)
        def _():
            o_ref[...] = _rms(o_ref[...], fg_ref[...])


def _ffn_layer(h, g, w1, w3, w2, final_g, *, final_norm, tm=1024, tf=256, tn=512):
    T, D = h.shape
    F = w1.shape[1]
    tm, tf, tn = min(tm, T), min(tf, F), min(tn, D)
    n_f = F // tf
    return pl.pallas_call(
        functools.partial(_ffn_kernel, n_f=n_f, final_norm=final_norm, tn=tn),
        out_shape=jax.ShapeDtypeStruct((T, D), F32),
        grid=(T // tm, n_f),
        in_specs=[
            pl.BlockSpec((tm, D), lambda i, f: (i, 0)),
            pl.BlockSpec((1, D), lambda i, f: (0, 0)),
            pl.BlockSpec((D, tf), lambda i, f: (0, f)),
            pl.BlockSpec((D, tf), lambda i, f: (0, f)),
            pl.BlockSpec((tf, D), lambda i, f: (f, 0)),
            pl.BlockSpec((1, D), lambda i, f: (0, 0)),
        ],
        out_specs=pl.BlockSpec((tm, D), lambda i, f: (i, 0)),
        scratch_shapes=[pltpu.VMEM((tm, D), BF16)],
        compiler_params=_params(("parallel", "arbitrary")),
        name="ffn_final" if final_norm else "ffn",
    )(h, g.reshape(1, D), w1, w3, w2, final_g.reshape(1, D))


def _proj_kernel(h_ref, hprev_ref, g_ref, mu_ref, wla_ref, ala_ref, gla_ref,
                 wr_ref, wk_ref, wv_ref, wlb_ref, alb_ref, glb_ref,
                 w0_ref, a0_ref, kk_ref, ka_ref, hsum_ref,
                 r_out, lw_out, k_out, v_out, an_out, bb_out, g_out,
                 xr_s, xk_s, xv_s, tw_s, ta_s, tg_s, *, tm, seq):
    i = pl.program_id(0)
    n = pl.program_id(1)

    @pl.when(n == 0)
    def _():
        g = g_ref[...]
        hn = _rms(h_ref[...], g)
        hp = _rms(hprev_ref[7:8, :], g)
        hp = jnp.where((i * tm) % seq == 0, 0.0, hp)
        row = lax.broadcasted_iota(jnp.int32, hn.shape, 0)
        shifted = jnp.where(row == 0, hp, pltpu.roll(hn, 1, axis=0))
        xx = shifted - hn
        xr_s[...] = (hn + xx * mu_ref[0:1, :]).astype(BF16)
        xk_s[...] = (hn + xx * mu_ref[2:3, :]).astype(BF16)
        xv_s[...] = (hn + xx * mu_ref[3:4, :]).astype(BF16)
        xw = (hn + xx * mu_ref[1:2, :]).astype(BF16)
        tw_s[...] = jnp.tanh(_dot(xw, wla_ref[...])).astype(BF16)
        xa = (hn + xx * mu_ref[4:5, :]).astype(BF16)
        ta_s[...] = _dot(xa, ala_ref[...]).astype(BF16)
        xg = (hn + xx * mu_ref[5:6, :]).astype(BF16)
        tg_s[...] = jax.nn.sigmoid(_dot(xg, gla_ref[...])).astype(BF16)

    r = _dot(xr_s[...], wr_ref[...])
    k = _dot(xk_s[...], wk_ref[...])
    v = _dot(xv_s[...], wv_ref[...])
    wl = _dot(tw_s[...], wlb_ref[...]) + w0_ref[...]
    al = _dot(ta_s[...], alb_ref[...]) + a0_ref[...]
    g_out[...] = _dot(tg_s[...], glb_ref[...])

    w_log = jnp.minimum(wl, 0.0) - jnp.log(1.0 + jnp.exp(-jnp.abs(wl))) - 0.5
    lw_out[...] = -jnp.exp(w_log)
    a_sig = jax.nn.sigmoid(al)
    kk = k * kk_ref[...]
    ss = _split_dot(kk * kk, hsum_ref[...], pieces=3)
    kk = kk / jnp.maximum(jnp.sqrt(ss), L2_EPS)
    r_out[...] = r
    k_out[...] = k * (1.0 + (a_sig - 1.0) * ka_ref[...])
    v_out[...] = v
    an_out[...] = -kk
    bb_out[...] = kk * a_sig


def _head_sum_matrix(n, value=1.0):
    hi = lax.broadcasted_iota(jnp.int32, (n, n), 0) // HEAD
    hj = lax.broadcasted_iota(jnp.int32, (n, n), 1) // HEAD
    return jnp.where(hi == hj, value, 0.0).astype(BF16)


def _proj_layer(h, seq, g, mu, w_r, w_k, w_v, w0, w_la, w_lb, a0, a_la, a_lb, g_la, g_lb, k_k, k_a,
                *, tm=256, tn=512):
    T, D = h.shape
    tm, tn = min(tm, T), min(tn, D)
    dl, al, gl = w_la.shape[1], a_la.shape[1], g_la.shape[1]
    row = lambda x: x.reshape(1, D)
    full = lambda shp: pl.BlockSpec(shp, lambda i, n: (0, 0))
    col = lambda rows: pl.BlockSpec((rows, tn), lambda i, n: (0, n))
    out_spec = pl.BlockSpec((tm, tn), lambda i, n: (i, n))
    prev_blocks = tm // 8
    outs = pl.pallas_call(
        functools.partial(_proj_kernel, tm=tm, seq=seq),
        out_shape=[jax.ShapeDtypeStruct((T, D), F32)] * 7,
        grid=(T // tm, D // tn),
        in_specs=[
            pl.BlockSpec((tm, D), lambda i, n: (i, 0)),
            pl.BlockSpec((8, D), lambda i, n: (jnp.maximum(i * prev_blocks - 1, 0), 0)),
            full((1, D)), full((8, D)), full((D, dl)), full((D, al)), full((D, gl)),
            col(D), col(D), col(D), col(dl), col(al), col(gl),
            col(1), col(1), col(1), col(1),
            full((tn, tn)),
        ],
        out_specs=[out_spec] * 7,
        scratch_shapes=[pltpu.VMEM((tm, D), BF16)] * 3
        + [pltpu.VMEM((tm, dl), BF16), pltpu.VMEM((tm, al), BF16), pltpu.VMEM((tm, gl), BF16)],
        compiler_params=_params(("parallel", "arbitrary")),
        name="rwkv_proj",
    )(h, h, row(g), jnp.pad(mu, ((0, 8 - mu.shape[0]), (0, 0))), w_la, a_la, g_la,
      w_r, w_k, w_v, w_lb, a_lb, g_lb, row(w0), row(a0), row(k_k), row(k_a),
      _head_sum_matrix(tn))
    return outs


def _scan_kernel(r_ref, lw_ref, k_ref, v_ref, an_ref, bb_ref, g_ref, rk_ref, lnw_ref, lnb_ref,
                 tri_ref, hmean_ref, hsum_ref, z_ref, s_ref, y_ref, *, n_chunks):
    cb = pl.program_id(2)

    @pl.when(cb == 0)
    def _():
        s_ref[...] = jnp.zeros_like(s_ref)

    two = 2 * CHUNK
    lane_head0 = lax.broadcasted_iota(jnp.int32, (CHUNK, LANES), 1) < HEAD
    ri = lax.broadcasted_iota(jnp.int32, (two, two), 0)
    ci = lax.broadcasted_iota(jnp.int32, (two, two), 1)
    same = (ri >= CHUNK) == (ci >= CHUNK)
    strict = same & (ci < ri)
    incl = same & (ci <= ri)
    eye = jnp.where(ri == ci, 1.0, 0.0).astype(F32)

    def stack(x):
        return jnp.concatenate([jnp.where(lane_head0, x, 0.0), jnp.where(lane_head0, 0.0, x)], axis=0)

    def unstack(x):
        return x[:CHUNK] + x[CHUNK:]

    state = s_ref[...]
    for c in range(n_chunks):
        sl = pl.ds(c * CHUNK, CHUNK)
        lw = lw_ref[0, sl, :]
        r = r_ref[0, sl, :]
        k = k_ref[0, sl, :]
        v = v_ref[0, sl, :]
        an = an_ref[0, sl, :]
        bb = bb_ref[0, sl, :]

        cum = _cumsum_rows(lw, tri_ref[...])
        tot = cum[CHUNK - 1:CHUNK, :]
        e_in = jnp.exp(cum)
        e_ex = jnp.exp(cum - lw)
        e_neg = jnp.exp(-cum)
        e_rem = jnp.exp(tot - cum)

        a_s = stack(an * e_ex)
        r_s = stack(r * e_in)
        v_s = stack(v)
        bh_s = stack(bb * e_rem)
        kh_s = stack(k * e_rem)
        b_t = (bb * e_neg).astype(BF16)
        k_t = (k * e_neg).astype(BF16)

        lhs = jnp.concatenate([a_s, r_s], axis=0).astype(BF16)
        rhs = jnp.concatenate([b_t, b_t, k_t, k_t], axis=0)
        gram = _dot_nt(lhs, rhs)
        l_m = jnp.where(strict, gram[:two, :two], 0.0)
        a_2 = jnp.where(strict, gram[:two, two:], 0.0)
        a_3 = jnp.where(incl, gram[two:, :two], 0.0)
        a_4 = jnp.where(incl, gram[two:, two:], 0.0)

        t_m = eye + l_m
        p_m = l_m
        span = 2
        while span < CHUNK:
            pb = p_m.astype(BF16)
            p_m = _dot(pb, pb)
            t_m = t_m + _dot(p_m.astype(BF16), t_m.astype(BF16))
            span *= 2

        v_sb = v_s.astype(BF16)
        rhs2 = jnp.concatenate([a_s, _dot(a_2.astype(BF16), v_sb)], axis=1)
        x_m = _dot(t_m.astype(BF16), rhs2.astype(BF16))
        x_b = x_m.astype(BF16)
        low = jnp.concatenate([jnp.zeros_like(v_sb), v_sb], axis=1)
        qy = _dot(jnp.concatenate([a_3, a_4], axis=1).astype(BF16),
                  jnp.concatenate([x_b, low], axis=0))
        q_m = unstack(r_s + qy[:, :LANES])
        y_loc = unstack(qy[:, LANES:])

        sb = state.astype(BF16)
        y_ref[sl, :] = _dot_nt(q_m.astype(BF16), sb) + y_loc

        mn = _dot(x_m.T.astype(BF16), bh_s.astype(BF16))
        n_2 = _dot(v_s.T.astype(BF16), kh_s.astype(BF16))
        state = (state * e_in[CHUNK - 1:CHUNK, :] + _dot(sb, mn[:LANES].astype(BF16))
                 + mn[LANES:] + n_2)

    s_ref[...] = state

    y = y_ref[...]
    mean = _split_dot(y, hmean_ref[...], pieces=3)
    d = y - mean
    var = _split_dot(d * d, hmean_ref[...], pieces=3)
    yn = d * lax.rsqrt(var + GN_EPS) * lnw_ref[...] + lnb_ref[...]
    rr = r_ref[0]
    bonus = _split_dot(rr * k_ref[0] * rk_ref[...], hsum_ref[...], pieces=3) * v_ref[0]
    z_ref[0] = ((yn + bonus) * g_ref[0]).astype(z_ref.dtype)


def _cumsum_rows(x, tri):
    acc = None
    rem = x
    for _ in range(3):
        p = rem.astype(BF16)
        rem = rem - p.astype(F32)
        t = _dot(tri, p)
        acc = t if acc is None else acc + t
    return acc


def _scan_layer(r, lw, k, v, an, bb, g, r_k, lnx_w, lnx_b, *, n_chunks=4):
    B, S, D = r.shape
    n_chunks = min(n_chunks, S // CHUNK)
    L = n_chunks * CHUNK
    blk = pl.BlockSpec((1, L, LANES), lambda b, j, c: (b, c, j))
    vec = pl.BlockSpec((1, LANES), lambda b, j, c: (0, j))
    const = lambda n: pl.BlockSpec((n, n), lambda b, j, c: (0, 0))
    ti = lax.broadcasted_iota(jnp.int32, (CHUNK, CHUNK), 0)
    tj = lax.broadcasted_iota(jnp.int32, (CHUNK, CHUNK), 1)
    tri = jnp.where(tj <= ti, 1.0, 0.0).astype(BF16)
    return pl.pallas_call(
        functools.partial(_scan_kernel, n_chunks=n_chunks),
        out_shape=jax.ShapeDtypeStruct((B, S, D), BF16),
        grid=(B, D // LANES, S // L),
        in_specs=[blk] * 7 + [vec] * 3 + [const(CHUNK), const(LANES), const(LANES)],
        out_specs=blk,
        scratch_shapes=[pltpu.VMEM((LANES, LANES), F32), pltpu.VMEM((L, LANES), F32)],
        compiler_params=_params(("parallel", "parallel", "arbitrary")),
        name="rwkv_scan",
    )(r, lw, k, v, an, bb, g, r_k.reshape(1, D), lnx_w.reshape(1, D), lnx_b.reshape(1, D),
      tri, _head_sum_matrix(LANES, 1.0 / HEAD), _head_sum_matrix(LANES))


def _oproj_kernel(z_ref, w_ref, h_ref, o_ref):
    o_ref[...] = h_ref[...] + _dot(z_ref[...], w_ref[...])


def _oproj_layer(z, w, h, *, tm=1024, tn=512):
    T, D = h.shape
    tm, tn = min(tm, T), min(tn, D)
    return pl.pallas_call(
        _oproj_kernel,
        out_shape=jax.ShapeDtypeStruct((T, D), F32),
        grid=(T // tm, D // tn),
        in_specs=[
            pl.BlockSpec((tm, D), lambda i, n: (i, 0)),
            pl.BlockSpec((D, tn), lambda i, n: (0, n)),
            pl.BlockSpec((tm, tn), lambda i, n: (i, n)),
        ],
        out_specs=pl.BlockSpec((tm, tn), lambda i, n: (i, n)),
        compiler_params=_params(("parallel", "parallel")),
        name="rwkv_oproj",
    )(z, w, h)


def kernel(x, norm1_g, norm2_g, final_g, pool_w, pool_b, pool_scale, rw_mu, rw_r, rw_k, rw_v, rw_o, rw_w0, rw_w_la, rw_w_lb, rw_a0, rw_a_la, rw_a_lb, rw_g_la, rw_g_lb, rw_k_k, rw_k_a, rw_r_k, rw_lnx_w, rw_lnx_b, ffn_w1, ffn_w3, ffn_w2):
    B, S, D = x.shape
    T = B * S
    bf = lambda w: w.astype(BF16)

    h = _pool_layer(x, norm1_g[0], bf(pool_w[0]), pool_b[0].reshape(-1), pool_scale[0])
    h = _ffn_layer(h.reshape(T, D), norm2_g[0], bf(ffn_w1[0]), bf(ffn_w3[0]), bf(ffn_w2[0]), final_g,
                   final_norm=False)

    r, lw, k, v, an, bb, g = _proj_layer(
        h, S, norm1_g[1], rw_mu[0], bf(rw_r[0]), bf(rw_k[0]), bf(rw_v[0]), rw_w0[0],
        bf(rw_w_la[0]), bf(rw_w_lb[0]), rw_a0[0], bf(rw_a_la[0]), bf(rw_a_lb[0]),
        bf(rw_g_la[0]), bf(rw_g_lb[0]), rw_k_k[0], rw_k_a[0])
    s3 = lambda t: t.reshape(B, S, D)
    z = _scan_layer(s3(r), s3(lw), s3(k), s3(v), s3(an), s3(bb), s3(g),
                    rw_r_k[0].reshape(-1), rw_lnx_w[0], rw_lnx_b[0])
    h = _oproj_layer(z.reshape(T, D), bf(rw_o[0]), h)
    h = _ffn_layer(h, norm2_g[1], bf(ffn_w1[1]), bf(ffn_w3[1]), bf(ffn_w2[1]), final_g,
                   final_norm=True)
    return h.reshape(B, S, D)
```

```python
import functools

import jax
import jax.numpy as jnp
from jax import lax
from jax.experimental import pallas as pl
from jax.experimental.pallas import tpu as pltpu

F32 = jnp.float32
BF16 = jnp.bfloat16

RMS_EPS = 1e-6
GN_EPS = 64e-5
L2_EPS = 1e-12
POOL_WINDOWS = (2, 4, 8, 16)
POOL_HALO = 16
HEAD = 64
LANES = 128
CHUNK = 64
VMEM_LIMIT = 56 * 1024 * 1024


def _rms(x, g):
    return x * lax.rsqrt(jnp.mean(x * x, axis=-1, keepdims=True) + RMS_EPS) * g


def _dot(a, b):
    return jnp.dot(a, b, preferred_element_type=F32)


def _dot_nt(a, b):
    return lax.dot_general(a, b, (((1,), (1,)), ((), ())), preferred_element_type=F32)


def _split_dot(x, w, pieces=2):
    acc = None
    rem = x
    for _ in range(pieces):
        p = rem.astype(BF16)
        rem = rem - p.astype(F32)
        t = _dot(p, w)
        acc = t if acc is None else acc + t
    return acc


def _params(sem):
    return pltpu.CompilerParams(dimension_semantics=sem, vmem_limit_bytes=VMEM_LIMIT)


def _pool_kernel(x_ref, xprev_ref, g_ref, w_ref, b_ref, sc_ref, o_ref, *, ts, gdim):
    i = pl.program_id(1)
    g = g_ref[...]
    xc = x_ref[0]
    hn = _rms(xc, g)
    hp = _rms(xprev_ref[0], g)
    hp = jnp.where(i == 0, 0.0, hp)
    ext = jnp.concatenate([hp, hn], axis=0)
    rows = ts + POOL_HALO
    tau = lax.broadcasted_iota(jnp.int32, (ts, gdim), 0) + i * ts
    for gi, win in enumerate(POOL_WINDOWS):
        lo = gi * gdim
        e = ext[:, lo:lo + gdim]
        acc = e
        step = 1
        while step < win:
            acc = acc + pltpu.roll(acc, step, axis=0)
            step *= 2
        wsum = acc[POOL_HALO:rows]
        cnt = jnp.minimum(tau + 1, win).astype(F32)
        pooled = wsum / cnt - hn[:, lo:lo + gdim]
        mixed = _dot(pooled.astype(BF16), w_ref[gi]) + b_ref[:, lo:lo + gdim]
        o_ref[0, :, lo:lo + gdim] = xc[:, lo:lo + gdim] + mixed * sc_ref[:, lo:lo + gdim]


def _pool_layer(x, g, w_bf, b, scale, *, ts=512):
    B, S, D = x.shape
    G, C, _ = w_bf.shape
    ts = min(ts, S)
    halo_blocks = ts // POOL_HALO
    return pl.pallas_call(
        functools.partial(_pool_kernel, ts=ts, gdim=C),
        out_shape=jax.ShapeDtypeStruct((B, S, D), F32),
        grid=(B, S // ts),
        in_specs=[
            pl.BlockSpec((1, ts, D), lambda b, i: (b, i, 0)),
            pl.BlockSpec((1, POOL_HALO, D), lambda b, i: (b, jnp.maximum(i * halo_blocks - 1, 0), 0)),
            pl.BlockSpec((1, D), lambda b, i: (0, 0)),
            pl.BlockSpec((G, C, C), lambda b, i: (0, 0, 0)),
            pl.BlockSpec((1, D), lambda b, i: (0, 0)),
            pl.BlockSpec((1, D), lambda b, i: (0, 0)),
        ],
        out_specs=pl.BlockSpec((1, ts, D), lambda b, i: (b, i, 0)),
        compiler_params=_params(("parallel", "parallel")),
        name="pool_mixer",
    )(x, x, g.reshape(1, D), w_bf, b.reshape(1, D), scale.reshape(1, D))


def _ffn_kernel(h_ref, g_ref, w1_ref, w3_ref, w2_ref, fg_ref, o_ref, hn_ref, *, n_f, final_norm, tn):
    f = pl.program_id(1)

    @pl.when(f == 0)
    def _():
        hn_ref[...] = _rms(h_ref[...], g_ref[...]).astype(BF16)

    hn = hn_ref[...]
    a = _dot(hn, w1_ref[...])
    b = _dot(hn, w3_ref[...])
    act = (a * jax.nn.sigmoid(a) * b).astype(BF16)
    d = o_ref.shape[1]

    @pl.when(f == 0)
    def _():
        for n in range(0, d, tn):
            o_ref[:, n:n + tn] = h_ref[:, n:n + tn] + _dot(act, w2_ref[:, n:n + tn])

    @pl.when(f > 0)
    def _():
        for n in range(0, d, tn):
            o_ref[:, n:n + tn] += _dot(act, w2_ref[:, n:n + tn])

    if final_norm:
        @pl.when(f == n_f - 1)
        def _():
            o_ref[...] = _rms(o_ref[...], fg_ref[...])


def _ffn_layer(h, g, w1, w3, w2, final_g, *, final_norm, tm=1024, tf=256, tn=512):
    T, D = h.shape
    F = w1.shape[1]
    tm, tf, tn = min(tm, T), min(tf, F), min(tn, D)
    n_f = F // tf
    return pl.pallas_call(
        functools.partial(_ffn_kernel, n_f=n_f, final_norm=final_norm, tn=tn),
        out_shape=jax.ShapeDtypeStruct((T, D), F32),
        grid=(T // tm, n_f),
        in_specs=[
            pl.BlockSpec((tm, D), lambda i, f: (i, 0)),
            pl.BlockSpec((1, D), lambda i, f: (0, 0)),
            pl.BlockSpec((D, tf), lambda i, f: (0, f)),
            pl.BlockSpec((D, tf), lambda i, f: (0, f)),
            pl.BlockSpec((tf, D), lambda i, f: (f, 0)),
            pl.BlockSpec((1, D), lambda i, f: (0, 0)),
        ],
        out_specs=pl.BlockSpec((tm, D), lambda i, f: (i, 0)),
        scratch_shapes=[pltpu.VMEM((tm, D), BF16)],
        compiler_params=_params(("parallel", "arbitrary")),
        name="ffn_final" if final_norm else "ffn",
    )(h, g.reshape(1, D), w1, w3, w2, final_g.reshape(1, D))


def _proj_kernel(h_ref, hprev_ref, g_ref, mu_ref, wla_ref, ala_ref, gla_ref,
                 wr_ref, wk_ref, wv_ref, wlb_ref, alb_ref, glb_ref,
                 w0_ref, a0_ref, kk_ref, ka_ref, hsum_ref,
                 r_out, lw_out, k_out, v_out, an_out, bb_out, g_out,
                 xr_s, xk_s, xv_s, tw_s, ta_s, tg_s, *, tm, seq):
    i = pl.program_id(0)
    n = pl.program_id(1)

    @pl.when(n == 0)
    def _():
        g = g_ref[...]
        hn = _rms(h_ref[...], g)
        hp = _rms(hprev_ref[7:8, :], g)
        hp = jnp.where((i * tm) % seq == 0, 0.0, hp)
        row = lax.broadcasted_iota(jnp.int32, hn.shape, 0)
        shifted = jnp.where(row == 0, hp, pltpu.roll(hn, 1, axis=0))
        xx = shifted - hn
        xr_s[...] = (hn + xx * mu_ref[0:1, :]).astype(BF16)
        xk_s[...] = (hn + xx * mu_ref[2:3, :]).astype(BF16)
        xv_s[...] = (hn + xx * mu_ref[3:4, :]).astype(BF16)
        xw = (hn + xx * mu_ref[1:2, :]).astype(BF16)
        tw_s[...] = jnp.tanh(_dot(xw, wla_ref[...])).astype(BF16)
        xa = (hn + xx * mu_ref[4:5, :]).astype(BF16)
        ta_s[...] = _dot(xa, ala_ref[...]).astype(BF16)
        xg = (hn + xx * mu_ref[5:6, :]).astype(BF16)
        tg_s[...] = jax.nn.sigmoid(_dot(xg, gla_ref[...])).astype(BF16)

    r = _dot(xr_s[...], wr_ref[...])
    k = _dot(xk_s[...], wk_ref[...])
    v = _dot(xv_s[...], wv_ref[...])
    wl = _dot(tw_s[...], wlb_ref[...]) + w0_ref[...]
    al = _dot(ta_s[...], alb_ref[...]) + a0_ref[...]
    g_out[...] = _dot(tg_s[...], glb_ref[...])

    w_log = jnp.minimum(wl, 0.0) - jnp.log(1.0 + jnp.exp(-jnp.abs(wl))) - 0.5
    lw_out[...] = -jnp.exp(w_log)
    a_sig = jax.nn.sigmoid(al)
    kk = k * kk_ref[...]
    ss = _split_dot(kk * kk, hsum_ref[...], pieces=3)
    kk = kk / jnp.maximum(jnp.sqrt(ss), L2_EPS)
    r_out[...] = r
    k_out[...] = k * (1.0 + (a_sig - 1.0) * ka_ref[...])
    v_out[...] = v
    an_out[...] = -kk
    bb_out[...] = kk * a_sig


def _head_sum_matrix(n, value=1.0):
    hi = lax.broadcasted_iota(jnp.int32, (n, n), 0) // HEAD
    hj = lax.broadcasted_iota(jnp.int32, (n, n), 1) // HEAD
    return jnp.where(hi == hj, value, 0.0).astype(BF16)


def _proj_layer(h, seq, g, mu, w_r, w_k, w_v, w0, w_la, w_lb, a0, a_la, a_lb, g_la, g_lb, k_k, k_a,
                *, tm=256, tn=512):
    T, D = h.shape
    tm, tn = min(tm, T), min(tn, D)
    dl, al, gl = w_la.shape[1], a_la.shape[1], g_la.shape[1]
    row = lambda x: x.reshape(1, D)
    full = lambda shp: pl.BlockSpec(shp, lambda i, n: (0, 0))
    col = lambda rows: pl.BlockSpec((rows, tn), lambda i, n: (0, n))
    out_spec = pl.BlockSpec((tm, tn), lambda i, n: (i, n))
    prev_blocks = tm // 8
    outs = pl.pallas_call(
        functools.partial(_proj_kernel, tm=tm, seq=seq),
        out_shape=[jax.ShapeDtypeStruct((T, D), F32)] * 7,
        grid=(T // tm, D // tn),
        in_specs=[
            pl.BlockSpec((tm, D), lambda i, n: (i, 0)),
            pl.BlockSpec((8, D), lambda i, n: (jnp.maximum(i * prev_blocks - 1, 0), 0)),
            full((1, D)), full((8, D)), full((D, dl)), full((D, al)), full((D, gl)),
            col(D), col(D), col(D), col(dl), col(al), col(gl),
            col(1), col(1), col(1), col(1),
            full((tn, tn)),
        ],
        out_specs=[out_spec] * 7,
        scratch_shapes=[pltpu.VMEM((tm, D), BF16)] * 3
        + [pltpu.VMEM((tm, dl), BF16), pltpu.VMEM((tm, al), BF16), pltpu.VMEM((tm, gl), BF16)],
        compiler_params=_params(("parallel", "arbitrary")),
        name="rwkv_proj",
    )(h, h, row(g), jnp.pad(mu, ((0, 8 - mu.shape[0]), (0, 0))), w_la, a_la, g_la,
      w_r, w_k, w_v, w_lb, a_lb, g_lb, row(w0), row(a0), row(k_k), row(k_a),
      _head_sum_matrix(tn))
    return outs


def _scan_kernel(r_ref, lw_ref, k_ref, v_ref, an_ref, bb_ref, g_ref, rk_ref, lnw_ref, lnb_ref,
                 tri_ref, hmean_ref, hsum_ref, z_ref, s_ref, y_ref, *, n_chunks):
    cb = pl.program_id(2)

    @pl.when(cb == 0)
    def _():
        s_ref[...] = jnp.zeros_like(s_ref)

    two = 2 * CHUNK
    lane_head0 = lax.broadcasted_iota(jnp.int32, (CHUNK, LANES), 1) < HEAD
    ri = lax.broadcasted_iota(jnp.int32, (two, two), 0)
    ci = lax.broadcasted_iota(jnp.int32, (two, two), 1)
    same = (ri >= CHUNK) == (ci >= CHUNK)
    strict = same & (ci < ri)
    incl = same & (ci <= ri)
    eye = jnp.where(ri == ci, 1.0, 0.0).astype(F32)

    def stack(x):
        return jnp.concatenate([jnp.where(lane_head0, x, 0.0), jnp.where(lane_head0, 0.0, x)], axis=0)

    def unstack(x):
        return x[:CHUNK] + x[CHUNK:]

    rng = range(n_chunks)
    sls = [pl.ds(c * CHUNK, CHUNK) for c in rng]
    cums = [_cumsum_rows(lw_ref[0, sl, :], tri_ref[...]) for sl in sls]
    a_s, r_s, v_s, v_sb, bh_b, kh_b, e_tot, grams = [], [], [], [], [], [], [], []
    for c in rng:
        sl = sls[c]
        lw = lw_ref[0, sl, :]
        cum = cums[c]
        e_in = jnp.exp(cum)
        e_neg = jnp.exp(-cum)
        e_rem = jnp.exp(cum[CHUNK - 1:CHUNK, :] - cum)
        k = k_ref[0, sl, :]
        bb = bb_ref[0, sl, :]
        a_s.append(stack(an_ref[0, sl, :] * jnp.exp(cum - lw)))
        r_s.append(stack(r_ref[0, sl, :] * e_in))
        v_s.append(stack(v_ref[0, sl, :]))
        v_sb.append(v_s[c].astype(BF16))
        bh_b.append(stack(bb * e_rem).astype(BF16))
        kh_b.append(stack(k * e_rem).astype(BF16))
        e_tot.append(e_in[CHUNK - 1:CHUNK, :])
        b_t = (bb * e_neg).astype(BF16)
        k_t = (k * e_neg).astype(BF16)
        lhs = jnp.concatenate([a_s[c], r_s[c]], axis=0).astype(BF16)
        rhs = jnp.concatenate([b_t, b_t, k_t, k_t], axis=0)
        grams.append(_dot_nt(lhs, rhs))
    p_m = [jnp.where(strict, gm[:two, :two], 0.0) for gm in grams]
    a_34 = [jnp.concatenate([jnp.where(incl, gm[two:, :two], 0.0),
                             jnp.where(incl, gm[two:, two:], 0.0)], axis=1).astype(BF16)
            for gm in grams]
    a2v = [_dot(jnp.where(strict, grams[c][:two, two:], 0.0).astype(BF16), v_sb[c]) for c in rng]

    t_m = [eye + p for p in p_m]
    span = 2
    while span < CHUNK:
        p_b = [p.astype(BF16) for p in p_m]
        p_m = [_dot(pb, pb) for pb in p_b]
        t_m = [t_m[c] + _dot(p_m[c].astype(BF16), t_m[c].astype(BF16)) for c in rng]
        span *= 2

    x_m = [_dot(t_m[c].astype(BF16), jnp.concatenate([a_s[c], a2v[c]], axis=1).astype(BF16))
           for c in rng]
    qy = [_dot(a_34[c], jnp.concatenate(
        [x_m[c].astype(BF16), jnp.concatenate([jnp.zeros_like(v_sb[c]), v_sb[c]], axis=1)], axis=0))
        for c in rng]
    mn = [_dot(x_m[c].T.astype(BF16), bh_b[c]) for c in rng]
    n_2 = [_dot(v_s[c].T.astype(BF16), kh_b[c]) for c in rng]

    state = s_ref[...]
    for c in rng:
        q_m = unstack(r_s[c] + qy[c][:, :LANES])
        sb = state.astype(BF16)
        y_ref[sls[c], :] = _dot_nt(q_m.astype(BF16), sb) + unstack(qy[c][:, LANES:])
        state = state * e_tot[c] + _dot(sb, mn[c][:LANES].astype(BF16)) + mn[c][LANES:] + n_2[c]

    s_ref[...] = state

    y = y_ref[...]
    mean = _split_dot(y, hmean_ref[...], pieces=3)
    d = y - mean
    var = _split_dot(d * d, hmean_ref[...], pieces=3)
    yn = d * lax.rsqrt(var + GN_EPS) * lnw_ref[...] + lnb_ref[...]
    rr = r_ref[0]
    bonus = _split_dot(rr * k_ref[0] * rk_ref[...], hsum_ref[...], pieces=3) * v_ref[0]
    z_ref[0] = ((yn + bonus) * g_ref[0]).astype(z_ref.dtype)


def _cumsum_rows(x, tri):
    acc = None
    rem = x
    for _ in range(3):
        p = rem.astype(BF16)
        rem = rem - p.astype(F32)
        t = _dot(tri, p)
        acc = t if acc is None else acc + t
    return acc


def _scan_layer(r, lw, k, v, an, bb, g, r_k, lnx_w, lnx_b, *, n_chunks=16):
    B, S, D = r.shape
    n_chunks = min(n_chunks, S // CHUNK)
    L = n_chunks * CHUNK
    blk = pl.BlockSpec((1, L, LANES), lambda b, j, c: (b, c, j))
    vec = pl.BlockSpec((1, LANES), lambda b, j, c: (0, j))
    const = lambda n: pl.BlockSpec((n, n), lambda b, j, c: (0, 0))
    ti = lax.broadcasted_iota(jnp.int32, (CHUNK, CHUNK), 0)
    tj = lax.broadcasted_iota(jnp.int32, (CHUNK, CHUNK), 1)
    tri = jnp.where(tj <= ti, 1.0, 0.0).astype(BF16)
    return pl.pallas_call(
        functools.partial(_scan_kernel, n_chunks=n_chunks),
        out_shape=jax.ShapeDtypeStruct((B, S, D), BF16),
        grid=(B, D // LANES, S // L),
        in_specs=[blk] * 7 + [vec] * 3 + [const(CHUNK), const(LANES), const(LANES)],
        out_specs=blk,
        scratch_shapes=[pltpu.VMEM((LANES, LANES), F32), pltpu.VMEM((L, LANES), F32)],
        compiler_params=_params(("parallel", "parallel", "arbitrary")),
        name="rwkv_scan",
    )(r, lw, k, v, an, bb, g, r_k.reshape(1, D), lnx_w.reshape(1, D), lnx_b.reshape(1, D),
      tri, _head_sum_matrix(LANES, 1.0 / HEAD), _head_sum_matrix(LANES))


def _oproj_kernel(z_ref, w_ref, h_ref, o_ref):
    o_ref[...] = h_ref[...] + _dot(z_ref[...], w_ref[...])


def _oproj_layer(z, w, h, *, tm=1024, tn=512):
    T, D = h.shape
    tm, tn = min(tm, T), min(tn, D)
    return pl.pallas_call(
        _oproj_kernel,
        out_shape=jax.ShapeDtypeStruct((T, D), F32),
        grid=(T // tm, D // tn),
        in_specs=[
            pl.BlockSpec((tm, D), lambda i, n: (i, 0)),
            pl.BlockSpec((D, tn), lambda i, n: (0, n)),
            pl.BlockSpec((tm, tn), lambda i, n: (i, n)),
        ],
        out_specs=pl.BlockSpec((tm, tn), lambda i, n: (i, n)),
        compiler_params=_params(("parallel", "parallel")),
        name="rwkv_oproj",
    )(z, w, h)


def kernel(x, norm1_g, norm2_g, final_g, pool_w, pool_b, pool_scale, rw_mu, rw_r, rw_k, rw_v, rw_o, rw_w0, rw_w_la, rw_w_lb, rw_a0, rw_a_la, rw_a_lb, rw_g_la, rw_g_lb, rw_k_k, rw_k_a, rw_r_k, rw_lnx_w, rw_lnx_b, ffn_w1, ffn_w3, ffn_w2):
    B, S, D = x.shape
    T = B * S
    bf = lambda w: w.astype(BF16)

    h = _pool_layer(x, norm1_g[0], bf(pool_w[0]), pool_b[0].reshape(-1), pool_scale[0])
    h = _ffn_layer(h.reshape(T, D), norm2_g[0], bf(ffn_w1[0]), bf(ffn_w3[0]), bf(ffn_w2[0]), final_g,
                   final_norm=False)

    r, lw, k, v, an, bb, g = _proj_layer(
        h, S, norm1_g[1], rw_mu[0], bf(rw_r[0]), bf(rw_k[0]), bf(rw_v[0]), rw_w0[0],
        bf(rw_w_la[0]), bf(rw_w_lb[0]), rw_a0[0], bf(rw_a_la[0]), bf(rw_a_lb[0]),
        bf(rw_g_la[0]), bf(rw_g_lb[0]), rw_k_k[0], rw_k_a[0])
    s3 = lambda t: t.reshape(B, S, D)
    z = _scan_layer(s3(r), s3(lw), s3(k), s3(v), s3(an), s3(bb), s3(g),
                    rw_r_k[0].reshape(-1), rw_lnx_w[0], rw_lnx_b[0])
    h = _oproj_layer(z.reshape(T, D), bf(rw_o[0]), h)
    h = _ffn_layer(h, norm2_g[1], bf(ffn_w1[1]), bf(ffn_w3[1]), bf(ffn_w2[1]), final_g,
                   final_norm=True)
    return h.reshape(B, S, D)
```

```python
import functools

import jax
import jax.numpy as jnp
from jax import lax
from jax.experimental import pallas as pl
from jax.experimental.pallas import tpu as pltpu

F32 = jnp.float32
BF16 = jnp.bfloat16

RMS_EPS = 1e-6
GN_EPS = 64e-5
L2_EPS = 1e-12
POOL_WINDOWS = (2, 4, 8, 16)
POOL_HALO = 16
HEAD = 64
LANES = 128
CHUNK = 64
VMEM_LIMIT = 56 * 1024 * 1024


def _rms(x, g):
    return x * lax.rsqrt(jnp.mean(x * x, axis=-1, keepdims=True) + RMS_EPS) * g


def _dot(a, b):
    return jnp.dot(a, b, preferred_element_type=F32)


def _dot_nt(a, b):
    return lax.dot_general(a, b, (((1,), (1,)), ((), ())), preferred_element_type=F32)


def _split_dot(x, w, pieces=2):
    acc = None
    rem = x
    for _ in range(pieces):
        p = rem.astype(BF16)
        rem = rem - p.astype(F32)
        t = _dot(p, w)
        acc = t if acc is None else acc + t
    return acc


def _params(sem):
    return pltpu.CompilerParams(dimension_semantics=sem, vmem_limit_bytes=VMEM_LIMIT)


def _pool_kernel(x_ref, xprev_ref, g_ref, w_ref, b_ref, sc_ref, o_ref, *, ts, gdim):
    i = pl.program_id(1)
    g = g_ref[...]
    xc = x_ref[0]
    hn = _rms(xc, g)
    hp = _rms(xprev_ref[0], g)
    hp = jnp.where(i == 0, 0.0, hp)
    ext = jnp.concatenate([hp, hn], axis=0)
    rows = ts + POOL_HALO
    tau = lax.broadcasted_iota(jnp.int32, (ts, gdim), 0) + i * ts
    for gi, win in enumerate(POOL_WINDOWS):
        lo = gi * gdim
        e = ext[:, lo:lo + gdim]
        acc = e
        step = 1
        while step < win:
            acc = acc + pltpu.roll(acc, step, axis=0)
            step *= 2
        wsum = acc[POOL_HALO:rows]
        cnt = jnp.minimum(tau + 1, win).astype(F32)
        pooled = wsum / cnt - hn[:, lo:lo + gdim]
        mixed = _dot(pooled.astype(BF16), w_ref[gi]) + b_ref[:, lo:lo + gdim]
        o_ref[0, :, lo:lo + gdim] = xc[:, lo:lo + gdim] + mixed * sc_ref[:, lo:lo + gdim]


def _pool_layer(x, g, w_bf, b, scale, *, ts=512):
    B, S, D = x.shape
    G, C, _ = w_bf.shape
    ts = min(ts, S)
    halo_blocks = ts // POOL_HALO
    return pl.pallas_call(
        functools.partial(_pool_kernel, ts=ts, gdim=C),
        out_shape=jax.ShapeDtypeStruct((B, S, D), F32),
        grid=(B, S // ts),
        in_specs=[
            pl.BlockSpec((1, ts, D), lambda b, i: (b, i, 0)),
            pl.BlockSpec((1, POOL_HALO, D), lambda b, i: (b, jnp.maximum(i * halo_blocks - 1, 0), 0)),
            pl.BlockSpec((1, D), lambda b, i: (0, 0)),
            pl.BlockSpec((G, C, C), lambda b, i: (0, 0, 0)),
            pl.BlockSpec((1, D), lambda b, i: (0, 0)),
            pl.BlockSpec((1, D), lambda b, i: (0, 0)),
        ],
        out_specs=pl.BlockSpec((1, ts, D), lambda b, i: (b, i, 0)),
        compiler_params=_params(("parallel", "parallel")),
        name="pool_mixer",
    )(x, x, g.reshape(1, D), w_bf, b.reshape(1, D), scale.reshape(1, D))


def _ffn_kernel(h_ref, g_ref, w1_ref, w3_ref, w2_ref, fg_ref, o_ref, hn_ref, *, n_f, final_norm, tn):
    f = pl.program_id(1)

    @pl.when(f == 0)
    def _():
        hn_ref[...] = _rms(h_ref[...], g_ref[...]).astype(BF16)

    hn = hn_ref[...]
    a = _dot(hn, w1_ref[...])
    b = _dot(hn, w3_ref[...])
    act = (a * jax.nn.sigmoid(a) * b).astype(BF16)
    d = o_ref.shape[1]

    @pl.when(f == 0)
    def _():
        for n in range(0, d, tn):
            o_ref[:, n:n + tn] = h_ref[:, n:n + tn] + _dot(act, w2_ref[:, n:n + tn])

    @pl.when(f > 0)
    def _():
        for n in range(0, d, tn):
            o_ref[:, n:n + tn] += _dot(act, w2_ref[:, n:n + tn])

    if final_norm:
        @pl.when(f == n_f - 1)
        def _():
            o_ref[...] = _rms(o_ref[...], fg_ref[...])


def _ffn_layer(h, g, w1, w3, w2, final_g, *, final_norm, tm=1024, tf=256, tn=512):
    T, D = h.shape
    F = w1.shape[1]
    tm, tf, tn = min(tm, T), min(tf, F), min(tn, D)
    n_f = F // tf
    return pl.pallas_call(
        functools.partial(_ffn_kernel, n_f=n_f, final_norm=final_norm, tn=tn),
        out_shape=jax.ShapeDtypeStruct((T, D), F32),
        grid=(T // tm, n_f),
        in_specs=[
            pl.BlockSpec((tm, D), lambda i, f: (i, 0)),
            pl.BlockSpec((1, D), lambda i, f: (0, 0)),
            pl.BlockSpec((D, tf), lambda i, f: (0, f)),
            pl.BlockSpec((D, tf), lambda i, f: (0, f)),
            pl.BlockSpec((tf, D), lambda i, f: (f, 0)),
            pl.BlockSpec((1, D), lambda i, f: (0, 0)),
        ],
        out_specs=pl.BlockSpec((tm, D), lambda i, f: (i, 0)),
        scratch_shapes=[pltpu.VMEM((tm, D), BF16)],
        compiler_params=_params(("parallel", "arbitrary")),
        name="ffn_final" if final_norm else "ffn",
    )(h, g.reshape(1, D), w1, w3, w2, final_g.reshape(1, D))


def _proj_kernel(h_ref, hprev_ref, g_ref, mu_ref, wla_ref, ala_ref, gla_ref,
                 wr_ref, wk_ref, wv_ref, wlb_ref, alb_ref, glb_ref,
                 w0_ref, a0_ref, kk_ref, ka_ref, hsum_ref,
                 r_out, lw_out, k_out, v_out, an_out, bb_out, g_out,
                 xr_s, xk_s, xv_s, tw_s, ta_s, tg_s, *, tm, seq):
    i = pl.program_id(0)
    n = pl.program_id(1)

    @pl.when(n == 0)
    def _():
        g = g_ref[...]
        hn = _rms(h_ref[...], g)
        hp = _rms(hprev_ref[7:8, :], g)
        hp = jnp.where((i * tm) % seq == 0, 0.0, hp)
        row = lax.broadcasted_iota(jnp.int32, hn.shape, 0)
        shifted = jnp.where(row == 0, hp, pltpu.roll(hn, 1, axis=0))
        xx = shifted - hn
        xr_s[...] = (hn + xx * mu_ref[0:1, :]).astype(BF16)
        xk_s[...] = (hn + xx * mu_ref[2:3, :]).astype(BF16)
        xv_s[...] = (hn + xx * mu_ref[3:4, :]).astype(BF16)
        xw = (hn + xx * mu_ref[1:2, :]).astype(BF16)
        tw_s[...] = jnp.tanh(_dot(xw, wla_ref[...])).astype(BF16)
        xa = (hn + xx * mu_ref[4:5, :]).astype(BF16)
        ta_s[...] = _dot(xa, ala_ref[...]).astype(BF16)
        xg = (hn + xx * mu_ref[5:6, :]).astype(BF16)
        tg_s[...] = jax.nn.sigmoid(_dot(xg, gla_ref[...])).astype(BF16)

    r = _dot(xr_s[...], wr_ref[...])
    k = _dot(xk_s[...], wk_ref[...])
    v = _dot(xv_s[...], wv_ref[...])
    wl = _dot(tw_s[...], wlb_ref[...]) + w0_ref[...]
    al = _dot(ta_s[...], alb_ref[...]) + a0_ref[...]
    g_out[...] = _dot(tg_s[...], glb_ref[...])

    w_log = jnp.minimum(wl, 0.0) - jnp.log(1.0 + jnp.exp(-jnp.abs(wl))) - 0.5
    lw_out[...] = -jnp.exp(w_log)
    a_sig = jax.nn.sigmoid(al)
    kk = k * kk_ref[...]
    ss = _split_dot(kk * kk, hsum_ref[...], pieces=3)
    kk = kk / jnp.maximum(jnp.sqrt(ss), L2_EPS)
    r_out[...] = r
    k_out[...] = k * (1.0 + (a_sig - 1.0) * ka_ref[...])
    v_out[...] = v
    an_out[...] = -kk
    bb_out[...] = kk * a_sig


def _head_sum_matrix(n, value=1.0):
    hi = lax.broadcasted_iota(jnp.int32, (n, n), 0) // HEAD
    hj = lax.broadcasted_iota(jnp.int32, (n, n), 1) // HEAD
    return jnp.where(hi == hj, value, 0.0).astype(BF16)


def _proj_layer(h, seq, g, mu, w_r, w_k, w_v, w0, w_la, w_lb, a0, a_la, a_lb, g_la, g_lb, k_k, k_a,
                *, tm=256, tn=512):
    T, D = h.shape
    tm, tn = min(tm, T), min(tn, D)
    dl, al, gl = w_la.shape[1], a_la.shape[1], g_la.shape[1]
    row = lambda x: x.reshape(1, D)
    full = lambda shp: pl.BlockSpec(shp, lambda i, n: (0, 0))
    col = lambda rows: pl.BlockSpec((rows, tn), lambda i, n: (0, n))
    out_spec = pl.BlockSpec((tm, tn), lambda i, n: (i, n))
    prev_blocks = tm // 8
    outs = pl.pallas_call(
        functools.partial(_proj_kernel, tm=tm, seq=seq),
        out_shape=[jax.ShapeDtypeStruct((T, D), F32)] * 7,
        grid=(T // tm, D // tn),
        in_specs=[
            pl.BlockSpec((tm, D), lambda i, n: (i, 0)),
            pl.BlockSpec((8, D), lambda i, n: (jnp.maximum(i * prev_blocks - 1, 0), 0)),
            full((1, D)), full((8, D)), full((D, dl)), full((D, al)), full((D, gl)),
            col(D), col(D), col(D), col(dl), col(al), col(gl),
            col(1), col(1), col(1), col(1),
            full((tn, tn)),
        ],
        out_specs=[out_spec] * 7,
        scratch_shapes=[pltpu.VMEM((tm, D), BF16)] * 3
        + [pltpu.VMEM((tm, dl), BF16), pltpu.VMEM((tm, al), BF16), pltpu.VMEM((tm, gl), BF16)],
        compiler_params=_params(("parallel", "arbitrary")),
        name="rwkv_proj",
    )(h, h, row(g), jnp.pad(mu, ((0, 8 - mu.shape[0]), (0, 0))), w_la, a_la, g_la,
      w_r, w_k, w_v, w_lb, a_lb, g_lb, row(w0), row(a0), row(k_k), row(k_a),
      _head_sum_matrix(tn))
    return outs


def _scan_kernel(r_ref, lw_ref, k_ref, v_ref, an_ref, bb_ref, g_ref, rk_ref, lnw_ref, lnb_ref,
                 hmean_ref, hsum_ref, z_ref,
                 s_ref, y_ref, q_s, yl_s, gm_s, nm_s, et_s, bg_s, *, n_chunks, blocks_per_seq):
    step = pl.program_id(0)

    @pl.when(step == 0)
    def _():
        for ref in (s_ref, q_s, yl_s, gm_s, nm_s, et_s, bg_s):
            ref[...] = jnp.zeros_like(ref)

    two = 2 * CHUNK
    lane_head0 = lax.broadcasted_iota(jnp.int32, (CHUNK, LANES), 1) < HEAD
    ri = lax.broadcasted_iota(jnp.int32, (two, two), 0)
    ci = lax.broadcasted_iota(jnp.int32, (two, two), 1)
    same = (ri >= CHUNK) == (ci >= CHUNK)
    strict = same & (ci < ri)
    incl = same & (ci <= ri)
    eye = jnp.where(ri == ci, 1.0, 0.0).astype(F32)

    def stack(x):
        return jnp.concatenate([jnp.where(lane_head0, x, 0.0), jnp.where(lane_head0, 0.0, x)], axis=0)

    def unstack(x):
        return x[:CHUNK] + x[CHUNK:]

    rng = range(n_chunks)
    sls = [pl.ds(c * CHUNK, CHUNK) for c in rng]

    prev_first = (step + blocks_per_seq - 1) % blocks_per_seq == 0
    carry = {"state": jnp.where(prev_first, 0.0, s_ref[...]), "next": 0}

    def tail_steps(count):
        for c in range(carry["next"], min(carry["next"] + count, n_chunks)):
            state = carry["state"]
            sb = state.astype(BF16)
            y_ref[sls[c], :] = _dot_nt(q_s[c], sb) + yl_s[c]
            carry["state"] = state * et_s[c, 0:1, :] + _dot_nt(sb, gm_s[c]) + nm_s[c]
            carry["next"] = c + 1

    per_stage = -(-n_chunks // 6)
    tail_steps(per_stage)

    lw_all = lw_ref[0]
    cum_all = _chunk_cumsum(lw_all)
    e_in_all = jnp.exp(cum_all)
    e_ex_all = jnp.exp(cum_all - lw_all)
    e_neg_all = jnp.exp(-cum_all)
    a_s, r_s, v_sb, bk_t, e_tot, grams = [], [], [], [], [], []
    for c in rng:
        sl = slice(c * CHUNK, (c + 1) * CHUNK)
        cum = cum_all[sl]
        e_rem = jnp.exp(cum[CHUNK - 1:CHUNK, :] - cum)
        k = k_ref[0, sls[c], :]
        bb = bb_ref[0, sls[c], :]
        a_s.append(stack(an_ref[0, sls[c], :] * e_ex_all[sl]))
        r_s.append(stack(r_ref[0, sls[c], :] * e_in_all[sl]))
        v_sb.append(stack(v_ref[0, sls[c], :]).astype(BF16))
        bk_t.append(jnp.concatenate([stack(bb * e_rem).T, stack(k * e_rem).T], axis=1).astype(BF16))
        e_tot.append(e_in_all[sl][CHUNK - 1:CHUNK, :])
        b_t = (bb * e_neg_all[sl]).astype(BF16)
        k_t = (k * e_neg_all[sl]).astype(BF16)
        lhs = jnp.concatenate([a_s[c], r_s[c]], axis=0).astype(BF16)
        rhs = jnp.concatenate([b_t, b_t, k_t, k_t], axis=0)
        grams.append(_dot_nt(lhs, rhs))
    tail_steps(per_stage)
    p_m = [jnp.where(strict, gm[:two, :two], 0.0) for gm in grams]
    a_34 = [jnp.concatenate([jnp.where(incl, gm[two:, :two], 0.0),
                             jnp.where(incl, gm[two:, two:], 0.0)], axis=1).astype(BF16)
            for gm in grams]
    a2v = [_dot(jnp.where(strict, grams[c][:two, two:], 0.0).astype(BF16), v_sb[c]) for c in rng]

    t_m = [eye + p for p in p_m]
    span = 2
    rounds = 0
    while span < CHUNK:
        tail_steps(per_stage if rounds < 3 else n_chunks)
        if rounds == 3:
            s_ref[...] = carry["state"]
            y = y_ref[...]
            d = y - _split_dot(y, hmean_ref[...])
        if rounds == 4:
            var = _split_dot(d * d, hmean_ref[...])
        p_b = [p.astype(BF16) for p in p_m]
        p_m = [_dot(pb, pb) for pb in p_b]
        t_m = [t_m[c] + _dot(p_m[c].astype(BF16), t_m[c].astype(BF16)) for c in rng]
        span *= 2
        rounds += 1

    x_b = [_dot(t_m[c].astype(BF16), jnp.concatenate([a_s[c], a2v[c]], axis=1).astype(BF16)).astype(BF16)
           for c in rng]
    yn = d * lax.rsqrt(var + GN_EPS) * lnw_ref[...] + lnb_ref[...]
    z_ref[0] = ((yn + bg_s[0]) * bg_s[1]).astype(z_ref.dtype)
    bg_s[0] = _split_dot(r_ref[0] * k_ref[0] * rk_ref[...], hsum_ref[...]) * v_ref[0]
    bg_s[1] = g_ref[0]

    for c in rng:
        rhs = jnp.concatenate([x_b[c], jnp.concatenate([jnp.zeros_like(v_sb[c]), v_sb[c]], axis=1)], axis=0)
        out = _dot(jnp.concatenate([a_34[c], bk_t[c]], axis=0), rhs)
        q_s[c] = unstack(r_s[c] + out[:two, :LANES]).astype(BF16)
        yl_s[c] = unstack(out[:two, LANES:])
        gm_s[c] = out[two:, :LANES].astype(BF16)
        nm_s[c] = out[two:, LANES:].T
        et_s[c] = jnp.broadcast_to(e_tot[c], (8, LANES))


def _chunk_cumsum(x):
    t = lax.broadcasted_iota(jnp.int32, x.shape, 0) % CHUNK
    sh = 1
    while sh < CHUNK:
        x = x + jnp.where(t >= sh, pltpu.roll(x, sh, axis=0), 0.0)
        sh *= 2
    return x


def _scan_layer(r, lw, k, v, an, bb, g, r_k, lnx_w, lnx_b, *, n_chunks=8):
    B, S, D = r.shape
    n_chunks = min(n_chunks, S // CHUNK)
    L = n_chunks * CHUNK
    nb, nj = S // L, D // LANES
    n_blocks = B * nj * nb

    def coords(f):
        return f // (nj * nb), f % nb, (f // nb) % nj

    cur = lambda s: coords(jnp.minimum(s, n_blocks - 1))
    prev = lambda s: coords(jnp.maximum(s - 1, 0))
    blk_in = pl.BlockSpec((1, L, LANES), cur)
    vec_in = pl.BlockSpec((1, LANES), lambda s: (0, cur(s)[2]))
    vec_out = pl.BlockSpec((1, LANES), lambda s: (0, prev(s)[2]))
    const = lambda n: pl.BlockSpec((n, n), lambda s: (0, 0))
    two = 2 * CHUNK
    return pl.pallas_call(
        functools.partial(_scan_kernel, n_chunks=n_chunks, blocks_per_seq=nb),
        out_shape=jax.ShapeDtypeStruct((B, S, D), BF16),
        grid=(n_blocks + 1,),
        in_specs=[blk_in] * 7 + [vec_in, vec_out, vec_out] + [const(LANES), const(LANES)],
        out_specs=pl.BlockSpec((1, L, LANES), prev),
        scratch_shapes=[
            pltpu.VMEM((LANES, LANES), F32),
            pltpu.VMEM((L, LANES), F32),
            pltpu.VMEM((n_chunks, CHUNK, LANES), BF16),
            pltpu.VMEM((n_chunks, CHUNK, LANES), F32),
            pltpu.VMEM((n_chunks, two, LANES), BF16),
            pltpu.VMEM((n_chunks, two, LANES), F32),
            pltpu.VMEM((n_chunks, 8, LANES), F32),
            pltpu.VMEM((2, L, LANES), F32),
        ],
        compiler_params=_params(("arbitrary",)),
        name="rwkv_scan",
    )(r, lw, k, v, an, bb, g, r_k.reshape(1, D), lnx_w.reshape(1, D), lnx_b.reshape(1, D),
      _head_sum_matrix(LANES, 1.0 / HEAD), _head_sum_matrix(LANES))


def _oproj_kernel(z_ref, w_ref, h_ref, o_ref):
    o_ref[...] = h_ref[...] + _dot(z_ref[...], w_ref[...])


def _oproj_layer(z, w, h, *, tm=1024, tn=512):
    T, D = h.shape
    tm, tn = min(tm, T), min(tn, D)
    return pl.pallas_call(
        _oproj_kernel,
        out_shape=jax.ShapeDtypeStruct((T, D), F32),
        grid=(T // tm, D // tn),
        in_specs=[
            pl.BlockSpec((tm, D), lambda i, n: (i, 0)),
            pl.BlockSpec((D, tn), lambda i, n: (0, n)),
            pl.BlockSpec((tm, tn), lambda i, n: (i, n)),
        ],
        out_specs=pl.BlockSpec((tm, tn), lambda i, n: (i, n)),
        compiler_params=_params(("parallel", "parallel")),
        name="rwkv_oproj",
    )(z, w, h)


def kernel(x, norm1_g, norm2_g, final_g, pool_w, pool_b, pool_scale, rw_mu, rw_r, rw_k, rw_v, rw_o, rw_w0, rw_w_la, rw_w_lb, rw_a0, rw_a_la, rw_a_lb, rw_g_la, rw_g_lb, rw_k_k, rw_k_a, rw_r_k, rw_lnx_w, rw_lnx_b, ffn_w1, ffn_w3, ffn_w2):
    B, S, D = x.shape
    T = B * S
    bf = lambda w: w.astype(BF16)

    h = _pool_layer(x, norm1_g[0], bf(pool_w[0]), pool_b[0].reshape(-1), pool_scale[0])
    h = _ffn_layer(h.reshape(T, D), norm2_g[0], bf(ffn_w1[0]), bf(ffn_w3[0]), bf(ffn_w2[0]), final_g,
                   final_norm=False)

    r, lw, k, v, an, bb, g = _proj_layer(
        h, S, norm1_g[1], rw_mu[0], bf(rw_r[0]), bf(rw_k[0]), bf(rw_v[0]), rw_w0[0],
        bf(rw_w_la[0]), bf(rw_w_lb[0]), rw_a0[0], bf(rw_a_la[0]), bf(rw_a_lb[0]),
        bf(rw_g_la[0]), bf(rw_g_lb[0]), rw_k_k[0], rw_k_a[0])
    s3 = lambda t: t.reshape(B, S, D)
    z = _scan_layer(s3(r), s3(lw), s3(k), s3(v), s3(an), s3(bb), s3(g),
                    rw_r_k[0].reshape(-1), rw_lnx_w[0], rw_lnx_b[0])
    h = _oproj_layer(z.reshape(T, D), bf(rw_o[0]), h)
    h = _ffn_layer(h, norm2_g[1], bf(ffn_w1[1]), bf(ffn_w3[1]), bf(ffn_w2[1]), final_g,
                   final_norm=True)
    return h.reshape(B, S, D)
```

```python
import functools

import jax
import jax.numpy as jnp
from jax import lax
from jax.experimental import pallas as pl
from jax.experimental.pallas import tpu as pltpu

F32 = jnp.float32
BF16 = jnp.bfloat16

RMS_EPS = 1e-6
GN_EPS = 64e-5
L2_EPS = 1e-12
POOL_WINDOWS = (2, 4, 8, 16)
POOL_HALO = 16
HEAD = 64
LANES = 128
MXU_WIDTH = 256
CHUNK = 64
VMEM_LIMIT = 56 * 1024 * 1024


def _rms(x, g):
    return x * lax.rsqrt(jnp.mean(x * x, axis=-1, keepdims=True) + RMS_EPS) * g


def _dot(a, b):
    return jnp.dot(a, b, preferred_element_type=F32)


def _dot_nt(a, b):
    return lax.dot_general(a, b, (((1,), (1,)), ((), ())), preferred_element_type=F32)


def _split_dot(x, w, pieces=2):
    acc = None
    rem = x
    for _ in range(pieces):
        p = rem.astype(BF16)
        rem = rem - p.astype(F32)
        t = _dot(p, w)
        acc = t if acc is None else acc + t
    return acc


def _params(sem):
    return pltpu.CompilerParams(dimension_semantics=sem, vmem_limit_bytes=VMEM_LIMIT)


def _pool_kernel(x_ref, xprev_ref, g_ref, w_ref, b_ref, sc_ref, o_ref, *, ts, gdim):
    i = pl.program_id(1)
    g = g_ref[...]
    xc = x_ref[0]
    hn = _rms(xc, g)
    hp = _rms(xprev_ref[0], g)
    hp = jnp.where(i == 0, 0.0, hp)
    ext = jnp.concatenate([hp, hn], axis=0)
    rows = ts + POOL_HALO
    tau = lax.broadcasted_iota(jnp.int32, (ts, gdim), 0) + i * ts
    for gi, win in enumerate(POOL_WINDOWS):
        lo = gi * gdim
        e = ext[:, lo:lo + gdim]
        acc = e
        step = 1
        while step < win:
            acc = acc + pltpu.roll(acc, step, axis=0)
            step *= 2
        wsum = acc[POOL_HALO:rows]
        cnt = jnp.minimum(tau + 1, win).astype(F32)
        pooled = wsum / cnt - hn[:, lo:lo + gdim]
        mixed = _dot(pooled.astype(BF16), w_ref[gi]) + b_ref[:, lo:lo + gdim]
        o_ref[0, :, lo:lo + gdim] = xc[:, lo:lo + gdim] + mixed * sc_ref[:, lo:lo + gdim]


def _pool_layer(x, g, w_bf, b, scale, *, ts=512):
    B, S, D = x.shape
    G, C, _ = w_bf.shape
    ts = min(ts, S)
    halo_blocks = ts // POOL_HALO
    return pl.pallas_call(
        functools.partial(_pool_kernel, ts=ts, gdim=C),
        out_shape=jax.ShapeDtypeStruct((B, S, D), F32),
        grid=(B, S // ts),
        in_specs=[
            pl.BlockSpec((1, ts, D), lambda b, i: (b, i, 0)),
            pl.BlockSpec((1, POOL_HALO, D), lambda b, i: (b, jnp.maximum(i * halo_blocks - 1, 0), 0)),
            pl.BlockSpec((1, D), lambda b, i: (0, 0)),
            pl.BlockSpec((G, C, C), lambda b, i: (0, 0, 0)),
            pl.BlockSpec((1, D), lambda b, i: (0, 0)),
            pl.BlockSpec((1, D), lambda b, i: (0, 0)),
        ],
        out_specs=pl.BlockSpec((1, ts, D), lambda b, i: (b, i, 0)),
        compiler_params=_params(("parallel", "parallel")),
        name="pool_mixer",
    )(x, x, g.reshape(1, D), w_bf, b.reshape(1, D), scale.reshape(1, D))


def _ffn_kernel(h_ref, g_ref, w1_ref, w3_ref, w2_ref, fg_ref, o_ref, hn_ref, *, n_f, final_norm, tn):
    f = pl.program_id(1)

    @pl.when(f == 0)
    def _():
        h = h_ref[...]
        hn_ref[...] = _rms(h, g_ref[...]).astype(BF16)
        o_ref[...] = h

    hn = hn_ref[...]
    a = _dot(hn, w1_ref[...])
    b = _dot(hn, w3_ref[...])
    act = (a * jax.nn.sigmoid(a) * b).astype(BF16)
    d = o_ref.shape[1]
    for n in range(0, d, tn):
        o_ref[:, n:n + tn] += _dot(act, w2_ref[:, n:n + tn])

    if final_norm:
        @pl.when(f == n_f - 1)
        def _():
            o_ref[...] = _rms(o_ref[...], fg_ref[...])


def _ffn_layer(h, g, w1, w3, w2, final_g, *, final_norm, tm=1024, tf=256, tn=512):
    T, D = h.shape
    F = w1.shape[1]
    tm, tf, tn = min(tm, T), min(tf, F), min(tn, D)
    n_f = F // tf
    return pl.pallas_call(
        functools.partial(_ffn_kernel, n_f=n_f, final_norm=final_norm, tn=tn),
        out_shape=jax.ShapeDtypeStruct((T, D), F32),
        grid=(T // tm, n_f),
        in_specs=[
            pl.BlockSpec((tm, D), lambda i, f: (i, 0)),
            pl.BlockSpec((1, D), lambda i, f: (0, 0)),
            pl.BlockSpec((D, tf), lambda i, f: (0, f)),
            pl.BlockSpec((D, tf), lambda i, f: (0, f)),
            pl.BlockSpec((tf, D), lambda i, f: (f, 0)),
            pl.BlockSpec((1, D), lambda i, f: (0, 0)),
        ],
        out_specs=pl.BlockSpec((tm, D), lambda i, f: (i, 0)),
        scratch_shapes=[pltpu.VMEM((tm, D), BF16)],
        compiler_params=_params(("parallel", "arbitrary")),
        name="ffn_final" if final_norm else "ffn",
    )(h, g.reshape(1, D), w1, w3, w2, final_g.reshape(1, D))


def _proj_kernel(h_ref, hprev_ref, g_ref, mu_ref, wla_ref, ala_ref, gla_ref,
                 wr_ref, wk_ref, wv_ref, wlb_ref, alb_ref, glb_ref,
                 w0_ref, a0_ref, kk_ref, ka_ref, hsum_ref,
                 r_out, lw_out, k_out, v_out, an_out, bb_out, g_out,
                 xr_s, xk_s, xv_s, tw_s, ta_s, tg_s, *, tm, seq):
    i = pl.program_id(0)
    n = pl.program_id(1)

    @pl.when(n == 0)
    def _():
        g = g_ref[...]
        hn = _rms(h_ref[...], g)
        hp = _rms(hprev_ref[7:8, :], g)
        hp = jnp.where((i * tm) % seq == 0, 0.0, hp)
        row = lax.broadcasted_iota(jnp.int32, hn.shape, 0)
        shifted = jnp.where(row == 0, hp, pltpu.roll(hn, 1, axis=0))
        xx = shifted - hn
        xr_s[...] = (hn + xx * mu_ref[0:1, :]).astype(BF16)
        xk_s[...] = (hn + xx * mu_ref[2:3, :]).astype(BF16)
        xv_s[...] = (hn + xx * mu_ref[3:4, :]).astype(BF16)
        xw = (hn + xx * mu_ref[1:2, :]).astype(BF16)
        tw_s[...] = jnp.tanh(_dot(xw, wla_ref[...])).astype(BF16)
        xa = (hn + xx * mu_ref[4:5, :]).astype(BF16)
        ta_s[...] = _dot(xa, ala_ref[...]).astype(BF16)
        xg = (hn + xx * mu_ref[5:6, :]).astype(BF16)
        tg_s[...] = jax.nn.sigmoid(_dot(xg, gla_ref[...])).astype(BF16)

    slab = hsum_ref.shape[0]
    for lo in range(0, r_out.shape[1], slab):
        cs = slice(lo, lo + slab)
        r = _dot(xr_s[...], wr_ref[:, cs])
        k = _dot(xk_s[...], wk_ref[:, cs])
        v = _dot(xv_s[...], wv_ref[:, cs])
        wl = _dot(tw_s[...], wlb_ref[:, cs]) + w0_ref[:, cs]
        al = _dot(ta_s[...], alb_ref[:, cs]) + a0_ref[:, cs]
        g_out[:, cs] = _dot(tg_s[...], glb_ref[:, cs])

        w_log = jnp.minimum(wl, 0.0) - jnp.log(1.0 + jnp.exp(-jnp.abs(wl))) - 0.5
        lw_out[:, cs] = -jnp.exp(w_log)
        a_sig = jax.nn.sigmoid(al)
        kk = k * kk_ref[:, cs]
        ss = _split_dot(kk * kk, hsum_ref[...])
        kk = kk / jnp.maximum(jnp.sqrt(ss), L2_EPS)
        r_out[:, cs] = r
        k_out[:, cs] = k * (1.0 + (a_sig - 1.0) * ka_ref[:, cs])
        v_out[:, cs] = v
        an_out[:, cs] = -kk
        bb_out[:, cs] = kk * a_sig


def _head_sum_matrix(n, value=1.0):
    hi = lax.broadcasted_iota(jnp.int32, (n, n), 0) // HEAD
    hj = lax.broadcasted_iota(jnp.int32, (n, n), 1) // HEAD
    return jnp.where(hi == hj, value, 0.0).astype(BF16)


def _proj_layer(h, seq, g, mu, w_r, w_k, w_v, w0, w_la, w_lb, a0, a_la, a_lb, g_la, g_lb, k_k, k_a,
                *, tm=256, tn=512):
    T, D = h.shape
    tm, tn = min(tm, T), min(tn, D)
    slab = min(MXU_WIDTH, tn)
    dl, al, gl = w_la.shape[1], a_la.shape[1], g_la.shape[1]
    row = lambda x: x.reshape(1, D)
    full = lambda shp: pl.BlockSpec(shp, lambda i, n: (0, 0))
    col = lambda rows: pl.BlockSpec((rows, tn), lambda i, n: (0, n))
    out_spec = pl.BlockSpec((tm, tn), lambda i, n: (i, n))
    prev_blocks = tm // 8
    outs = pl.pallas_call(
        functools.partial(_proj_kernel, tm=tm, seq=seq),
        out_shape=[jax.ShapeDtypeStruct((T, D), F32)] * 7,
        grid=(T // tm, D // tn),
        in_specs=[
            pl.BlockSpec((tm, D), lambda i, n: (i, 0)),
            pl.BlockSpec((8, D), lambda i, n: (jnp.maximum(i * prev_blocks - 1, 0), 0)),
            full((1, D)), full((8, D)), full((D, dl)), full((D, al)), full((D, gl)),
            col(D), col(D), col(D), col(dl), col(al), col(gl),
            col(1), col(1), col(1), col(1),
            full((slab, slab)),
        ],
        out_specs=[out_spec] * 7,
        scratch_shapes=[pltpu.VMEM((tm, D), BF16)] * 3
        + [pltpu.VMEM((tm, dl), BF16), pltpu.VMEM((tm, al), BF16), pltpu.VMEM((tm, gl), BF16)],
        compiler_params=_params(("parallel", "arbitrary")),
        name="rwkv_proj",
    )(h, h, row(g), jnp.pad(mu, ((0, 8 - mu.shape[0]), (0, 0))), w_la, a_la, g_la,
      w_r, w_k, w_v, w_lb, a_lb, g_lb, row(w0), row(a0), row(k_k), row(k_a),
      _head_sum_matrix(slab))
    return outs


def _scan_kernel(r_ref, lw_ref, k_ref, v_ref, an_ref, bb_ref, g_ref, rk_ref, lnw_ref, lnb_ref,
                 hmean_ref, hsum_ref, z_ref,
                 s_ref, y_ref, q_s, yl_s, gm_s, nm_s, et_s, bg_s, *, n_chunks, blocks_per_seq):
    step = pl.program_id(0)

    @pl.when(step == 0)
    def _():
        for ref in (s_ref, q_s, yl_s, gm_s, nm_s, et_s, bg_s):
            ref[...] = jnp.zeros_like(ref)

    two = 2 * CHUNK
    lane_head0 = lax.broadcasted_iota(jnp.int32, (CHUNK, LANES), 1) < HEAD
    ri = lax.broadcasted_iota(jnp.int32, (two, two), 0)
    ci = lax.broadcasted_iota(jnp.int32, (two, two), 1)
    same = (ri >= CHUNK) == (ci >= CHUNK)
    strict = same & (ci < ri)
    incl = same & (ci <= ri)
    eye = jnp.where(ri == ci, 1.0, 0.0).astype(F32)

    def stack(x):
        return jnp.concatenate([jnp.where(lane_head0, x, 0.0), jnp.where(lane_head0, 0.0, x)], axis=0)

    def unstack(x):
        return x[:CHUNK] + x[CHUNK:]

    rng = range(n_chunks)
    sls = [pl.ds(c * CHUNK, CHUNK) for c in rng]

    prev_first = (step + blocks_per_seq - 1) % blocks_per_seq == 0
    carry = {"state": jnp.where(prev_first, 0.0, s_ref[...]), "next": 0}

    def tail_steps(count):
        for c in range(carry["next"], min(carry["next"] + count, n_chunks)):
            state = carry["state"]
            sb = state.astype(BF16)
            y_ref[sls[c], :] = _dot_nt(q_s[c], sb) + yl_s[c]
            carry["state"] = state * et_s[c, 0:1, :] + _dot_nt(sb, gm_s[c]) + nm_s[c]
            carry["next"] = c + 1

    per_stage = -(-n_chunks // 6)
    tail_steps(per_stage)

    lw_all = lw_ref[0]
    cum_all = _chunk_cumsum(lw_all)
    e_in_all = jnp.exp(cum_all)
    e_ex_all = jnp.exp(cum_all - lw_all)
    e_neg_all = jnp.exp(-cum_all)
    a_s, r_s, v_sb, bk_t, e_tot, grams = [], [], [], [], [], []
    for c in rng:
        sl = slice(c * CHUNK, (c + 1) * CHUNK)
        cum = cum_all[sl]
        e_rem = jnp.exp(cum[CHUNK - 1:CHUNK, :] - cum)
        k = k_ref[0, sls[c], :]
        bb = bb_ref[0, sls[c], :]
        a_s.append(stack(an_ref[0, sls[c], :] * e_ex_all[sl]))
        r_s.append(stack(r_ref[0, sls[c], :] * e_in_all[sl]))
        v_sb.append(stack(v_ref[0, sls[c], :]).astype(BF16))
        bk_t.append(jnp.concatenate([stack(bb * e_rem).T, stack(k * e_rem).T], axis=1).astype(BF16))
        e_tot.append(e_in_all[sl][CHUNK - 1:CHUNK, :])
        b_t = (bb * e_neg_all[sl]).astype(BF16)
        k_t = (k * e_neg_all[sl]).astype(BF16)
        lhs = jnp.concatenate([a_s[c], r_s[c]], axis=0).astype(BF16)
        rhs = jnp.concatenate([b_t, b_t, k_t, k_t], axis=0)
        grams.append(_dot_nt(lhs, rhs))
    tail_steps(per_stage)
    p_m = [jnp.where(strict, gm[:two, :two], 0.0) for gm in grams]
    a_34 = [jnp.concatenate([jnp.where(incl, gm[two:, :two], 0.0),
                             jnp.where(incl, gm[two:, two:], 0.0)], axis=1).astype(BF16)
            for gm in grams]
    a2v = [_dot(jnp.where(strict, grams[c][:two, two:], 0.0).astype(BF16), v_sb[c]) for c in rng]

    t_m = [eye + p for p in p_m]
    span = 2
    rounds = 0
    while span < CHUNK:
        tail_steps(per_stage if rounds < 3 else n_chunks)
        if rounds == 3:
            s_ref[...] = carry["state"]
            y = y_ref[...]
            d = y - _split_dot(y, hmean_ref[...])
        if rounds == 4:
            var = _split_dot(d * d, hmean_ref[...])
        p_b = [p.astype(BF16) for p in p_m]
        p_m = [_dot(pb, pb) for pb in p_b]
        t_m = [t_m[c] + _dot(p_m[c].astype(BF16), t_m[c].astype(BF16)) for c in rng]
        span *= 2
        rounds += 1

    x_b = [_dot(t_m[c].astype(BF16), jnp.concatenate([a_s[c], a2v[c]], axis=1).astype(BF16)).astype(BF16)
           for c in rng]
    yn = d * lax.rsqrt(var + GN_EPS) * lnw_ref[...] + lnb_ref[...]
    z_ref[0] = ((yn + bg_s[0]) * bg_s[1]).astype(z_ref.dtype)
    bg_s[0] = _split_dot(r_ref[0] * k_ref[0] * rk_ref[...], hsum_ref[...]) * v_ref[0]
    bg_s[1] = g_ref[0]

    for c in rng:
        rhs = jnp.concatenate([x_b[c], jnp.concatenate([jnp.zeros_like(v_sb[c]), v_sb[c]], axis=1)], axis=0)
        out = _dot(jnp.concatenate([a_34[c], bk_t[c]], axis=0), rhs)
        q_s[c] = unstack(r_s[c] + out[:two, :LANES]).astype(BF16)
        yl_s[c] = unstack(out[:two, LANES:])
        gm_s[c] = out[two:, :LANES].astype(BF16)
        nm_s[c] = out[two:, LANES:].T
        et_s[c] = jnp.broadcast_to(e_tot[c], (8, LANES))


def _chunk_cumsum(x):
    t = lax.broadcasted_iota(jnp.int32, x.shape, 0) % CHUNK
    sh = 1
    while sh < CHUNK:
        x = x + jnp.where(t >= sh, pltpu.roll(x, sh, axis=0), 0.0)
        sh *= 2
    return x


def _scan_layer(r, lw, k, v, an, bb, g, r_k, lnx_w, lnx_b, *, n_chunks=8):
    B, S, D = r.shape
    n_chunks = min(n_chunks, S // CHUNK)
    L = n_chunks * CHUNK
    nb, nj = S // L, D // LANES
    n_blocks = B * nj * nb

    def coords(f):
        return f // (nj * nb), f % nb, (f // nb) % nj

    cur = lambda s: coords(jnp.minimum(s, n_blocks - 1))
    prev = lambda s: coords(jnp.maximum(s - 1, 0))
    blk_in = pl.BlockSpec((1, L, LANES), cur)
    vec_in = pl.BlockSpec((1, LANES), lambda s: (0, cur(s)[2]))
    vec_out = pl.BlockSpec((1, LANES), lambda s: (0, prev(s)[2]))
    const = lambda n: pl.BlockSpec((n, n), lambda s: (0, 0))
    two = 2 * CHUNK
    return pl.pallas_call(
        functools.partial(_scan_kernel, n_chunks=n_chunks, blocks_per_seq=nb),
        out_shape=jax.ShapeDtypeStruct((B, S, D), BF16),
        grid=(n_blocks + 1,),
        in_specs=[blk_in] * 7 + [vec_in, vec_out, vec_out] + [const(LANES), const(LANES)],
        out_specs=pl.BlockSpec((1, L, LANES), prev),
        scratch_shapes=[
            pltpu.VMEM((LANES, LANES), F32),
            pltpu.VMEM((L, LANES), F32),
            pltpu.VMEM((n_chunks, CHUNK, LANES), BF16),
            pltpu.VMEM((n_chunks, CHUNK, LANES), F32),
            pltpu.VMEM((n_chunks, two, LANES), BF16),
            pltpu.VMEM((n_chunks, two, LANES), F32),
            pltpu.VMEM((n_chunks, 8, LANES), F32),
            pltpu.VMEM((2, L, LANES), F32),
        ],
        compiler_params=_params(("arbitrary",)),
        name="rwkv_scan",
    )(r, lw, k, v, an, bb, g, r_k.reshape(1, D), lnx_w.reshape(1, D), lnx_b.reshape(1, D),
      _head_sum_matrix(LANES, 1.0 / HEAD), _head_sum_matrix(LANES))


def _oproj_kernel(z_ref, w_ref, h_ref, o_ref):
    o_ref[...] = h_ref[...] + _dot(z_ref[...], w_ref[...])


def _oproj_layer(z, w, h, *, tm=1024, tn=512):
    T, D = h.shape
    tm, tn = min(tm, T), min(tn, D)
    return pl.pallas_call(
        _oproj_kernel,
        out_shape=jax.ShapeDtypeStruct((T, D), F32),
        grid=(T // tm, D // tn),
        in_specs=[
            pl.BlockSpec((tm, D), lambda i, n: (i, 0)),
            pl.BlockSpec((D, tn), lambda i, n: (0, n)),
            pl.BlockSpec((tm, tn), lambda i, n: (i, n)),
        ],
        out_specs=pl.BlockSpec((tm, tn), lambda i, n: (i, n)),
        compiler_params=_params(("parallel", "parallel")),
        name="rwkv_oproj",
    )(z, w, h)


def _cast_kernel(w_ref, o_ref):
    o_ref[...] = w_ref[0].astype(o_ref.dtype)


def _cast_weight(w, layer, *, block_bytes=4 * 1024 * 1024):
    _, R, C = w.shape
    tr = R
    while tr * C * 4 > block_bytes and tr % 16 == 0:
        tr //= 2
    return pl.pallas_call(
        _cast_kernel,
        out_shape=jax.ShapeDtypeStruct((R, C), BF16),
        grid=(R // tr,),
        in_specs=[pl.BlockSpec((1, tr, C), lambda i: (layer, i, 0))],
        out_specs=pl.BlockSpec((tr, C), lambda i: (i, 0)),
        compiler_params=_params(("parallel",)),
        name="cast_bf16",
    )(w)


def kernel(x, norm1_g, norm2_g, final_g, pool_w, pool_b, pool_scale, rw_mu, rw_r, rw_k, rw_v, rw_o, rw_w0, rw_w_la, rw_w_lb, rw_a0, rw_a_la, rw_a_lb, rw_g_la, rw_g_lb, rw_k_k, rw_k_a, rw_r_k, rw_lnx_w, rw_lnx_b, ffn_w1, ffn_w3, ffn_w2):
    B, S, D = x.shape
    T = B * S
    bf = lambda w: w.astype(BF16)
    cast = _cast_weight

    h = _pool_layer(x, norm1_g[0], bf(pool_w[0]), pool_b[0].reshape(-1), pool_scale[0])
    h = _ffn_layer(h.reshape(T, D), norm2_g[0], cast(ffn_w1, 0), cast(ffn_w3, 0), cast(ffn_w2, 0), final_g,
                   final_norm=False)

    r, lw, k, v, an, bb, g = _proj_layer(
        h, S, norm1_g[1], rw_mu[0], cast(rw_r, 0), cast(rw_k, 0), cast(rw_v, 0), rw_w0[0],
        bf(rw_w_la[0]), bf(rw_w_lb[0]), rw_a0[0], bf(rw_a_la[0]), bf(rw_a_lb[0]),
        bf(rw_g_la[0]), bf(rw_g_lb[0]), rw_k_k[0], rw_k_a[0])
    s3 = lambda t: t.reshape(B, S, D)
    z = _scan_layer(s3(r), s3(lw), s3(k), s3(v), s3(an), s3(bb), s3(g),
                    rw_r_k[0].reshape(-1), rw_lnx_w[0], rw_lnx_b[0])
    h = _oproj_layer(z.reshape(T, D), cast(rw_o, 0), h)
    h = _ffn_layer(h, norm2_g[1], cast(ffn_w1, 1), cast(ffn_w3, 1), cast(ffn_w2, 1), final_g,
                   final_norm=True)
    return h.reshape(B, S, D)
```

```python
import functools

import jax
import jax.numpy as jnp
from jax import lax
from jax.experimental import pallas as pl
from jax.experimental.pallas import tpu as pltpu

F32 = jnp.float32
BF16 = jnp.bfloat16

RMS_EPS = 1e-6
GN_EPS = 64e-5
L2_EPS = 1e-12
DECAY_SCALE = 0.6065306597126334
POOL_WINDOWS = (2, 4, 8, 16)
POOL_HALO = 16
HEAD = 64
LANES = 128
MXU_WIDTH = 256
CHUNK = 64
VMEM_LIMIT = 56 * 1024 * 1024


def _rms(x, g):
    return x * lax.rsqrt(jnp.mean(x * x, axis=-1, keepdims=True) + RMS_EPS) * g


def _dot(a, b):
    return jnp.dot(a, b, preferred_element_type=F32)


def _dot_nt(a, b):
    return lax.dot_general(a, b, (((1,), (1,)), ((), ())), preferred_element_type=F32)


def _split_dot(x, w, pieces=2):
    acc = None
    rem = x
    for _ in range(pieces):
        p = rem.astype(BF16)
        rem = rem - p.astype(F32)
        t = _dot(p, w)
        acc = t if acc is None else acc + t
    return acc


def _params(sem):
    return pltpu.CompilerParams(dimension_semantics=sem, vmem_limit_bytes=VMEM_LIMIT)


def _pool_kernel(x_ref, xprev_ref, g_ref, w_ref, b_ref, sc_ref, o_ref, *, ts, gdim):
    i = pl.program_id(1)
    g = g_ref[...]
    xc = x_ref[0]
    hn = _rms(xc, g)
    hp = _rms(xprev_ref[0], g)
    hp = jnp.where(i == 0, 0.0, hp)
    ext = jnp.concatenate([hp, hn], axis=0)
    rows = ts + POOL_HALO
    tau = lax.broadcasted_iota(jnp.int32, (ts, gdim), 0) + i * ts
    for gi, win in enumerate(POOL_WINDOWS):
        lo = gi * gdim
        e = ext[:, lo:lo + gdim]
        acc = e
        step = 1
        while step < win:
            acc = acc + pltpu.roll(acc, step, axis=0)
            step *= 2
        wsum = acc[POOL_HALO:rows]
        cnt = jnp.minimum(tau + 1, win).astype(F32)
        pooled = wsum / cnt - hn[:, lo:lo + gdim]
        mixed = _dot(pooled.astype(BF16), w_ref[gi]) + b_ref[:, lo:lo + gdim]
        o_ref[0, :, lo:lo + gdim] = xc[:, lo:lo + gdim] + mixed * sc_ref[:, lo:lo + gdim]


def _pool_layer(x, g, w_bf, b, scale, *, ts=512):
    B, S, D = x.shape
    G, C, _ = w_bf.shape
    ts = min(ts, S)
    halo_blocks = ts // POOL_HALO
    return pl.pallas_call(
        functools.partial(_pool_kernel, ts=ts, gdim=C),
        out_shape=jax.ShapeDtypeStruct((B, S, D), F32),
        grid=(B, S // ts),
        in_specs=[
            pl.BlockSpec((1, ts, D), lambda b, i: (b, i, 0)),
            pl.BlockSpec((1, POOL_HALO, D), lambda b, i: (b, jnp.maximum(i * halo_blocks - 1, 0), 0)),
            pl.BlockSpec((1, D), lambda b, i: (0, 0)),
            pl.BlockSpec((G, C, C), lambda b, i: (0, 0, 0)),
            pl.BlockSpec((1, D), lambda b, i: (0, 0)),
            pl.BlockSpec((1, D), lambda b, i: (0, 0)),
        ],
        out_specs=pl.BlockSpec((1, ts, D), lambda b, i: (b, i, 0)),
        compiler_params=_params(("parallel", "parallel")),
        name="pool_mixer",
    )(x, x, g.reshape(1, D), w_bf, b.reshape(1, D), scale.reshape(1, D))


def _ffn_kernel(h_ref, g_ref, w1_ref, w3_ref, w2_ref, fg_ref, o_ref, hn_ref, *, n_f, final_norm, tn):
    f = pl.program_id(1)

    @pl.when(f == 0)
    def _():
        h = h_ref[...]
        hn_ref[...] = _rms(h, g_ref[...]).astype(BF16)
        o_ref[...] = h

    hn = hn_ref[...]
    a = _dot(hn, w1_ref[...])
    b = _dot(hn, w3_ref[...])
    act = (a * jax.nn.sigmoid(a) * b).astype(BF16)
    d = o_ref.shape[1]
    for n in range(0, d, tn):
        o_ref[:, n:n + tn] += _dot(act, w2_ref[:, n:n + tn])

    if final_norm:
        @pl.when(f == n_f - 1)
        def _():
            o_ref[...] = _rms(o_ref[...], fg_ref[...])


def _ffn_layer(h, g, w1, w3, w2, final_g, *, final_norm, tm=1024, tf=256, tn=512):
    T, D = h.shape
    F = w1.shape[1]
    tm, tf, tn = min(tm, T), min(tf, F), min(tn, D)
    n_f = F // tf
    return pl.pallas_call(
        functools.partial(_ffn_kernel, n_f=n_f, final_norm=final_norm, tn=tn),
        out_shape=jax.ShapeDtypeStruct((T, D), F32),
        grid=(T // tm, n_f),
        in_specs=[
            pl.BlockSpec((tm, D), lambda i, f: (i, 0)),
            pl.BlockSpec((1, D), lambda i, f: (0, 0)),
            pl.BlockSpec((D, tf), lambda i, f: (0, f)),
            pl.BlockSpec((D, tf), lambda i, f: (0, f)),
            pl.BlockSpec((tf, D), lambda i, f: (f, 0)),
            pl.BlockSpec((1, D), lambda i, f: (0, 0)),
        ],
        out_specs=pl.BlockSpec((tm, D), lambda i, f: (i, 0)),
        scratch_shapes=[pltpu.VMEM((tm, D), BF16)],
        compiler_params=_params(("parallel", "arbitrary")),
        name="ffn_final" if final_norm else "ffn",
    )(h, g.reshape(1, D), w1, w3, w2, final_g.reshape(1, D))


def _proj_kernel(h_ref, hprev_ref, g_ref, mu_ref, wla_ref, ala_ref, gla_ref,
                 wr_ref, wk_ref, wv_ref, wlb_ref, alb_ref, glb_ref,
                 w0_ref, a0_ref, kk_ref, ka_ref, hsum_ref,
                 r_out, lw_out, k_out, v_out, an_out, bb_out, g_out,
                 xr_s, xk_s, xv_s, tw_s, ta_s, tg_s, *, tm, seq, n_tiles, n_steps):
    i = pl.program_id(0)
    n = pl.program_id(1)
    rq = tm // n_steps
    slab = hsum_ref.shape[0]
    slabs = [slice(lo, lo + slab) for lo in range(0, r_out.shape[1], slab)]

    def prep():
        slot = i % 2
        r0 = pl.multiple_of(n * rq, rq)
        dst = pl.ds(r0, rq)
        g = g_ref[...]
        hn = _rms(h_ref[dst, :], g)
        inside = h_ref[pl.ds(pl.multiple_of(jnp.maximum(r0 - 8, 0), 8), 8), :]
        prev8 = jnp.where(n == 0, hprev_ref[...], inside)
        hp = _rms(prev8[7:8, :], g)
        hp = jnp.where((n == 0) & ((i * tm) % seq == 0), 0.0, hp)
        row = lax.broadcasted_iota(jnp.int32, hn.shape, 0)
        shifted = jnp.where(row == 0, hp, pltpu.roll(hn, 1, axis=0))
        xx = shifted - hn
        xr_s[slot, dst, :] = (hn + xx * mu_ref[0:1, :]).astype(BF16)
        xk_s[slot, dst, :] = (hn + xx * mu_ref[2:3, :]).astype(BF16)
        xv_s[slot, dst, :] = (hn + xx * mu_ref[3:4, :]).astype(BF16)
        xw = (hn + xx * mu_ref[1:2, :]).astype(BF16)
        tw_s[slot, dst, :] = jnp.tanh(_dot(xw, wla_ref[...])).astype(BF16)
        xa = (hn + xx * mu_ref[4:5, :]).astype(BF16)
        ta_s[slot, dst, :] = _dot(xa, ala_ref[...]).astype(BF16)
        xg = (hn + xx * mu_ref[5:6, :]).astype(BF16)
        tg_s[slot, dst, :] = jax.nn.sigmoid(_dot(xg, gla_ref[...])).astype(BF16)

    def project(between):
        slot = (i + 1) % 2
        half = tm
        units = [(pl.ds(r0, half), cs) for r0 in range(0, tm, half) for cs in slabs]

        def matmuls(rows, cs):
            r = _dot(xr_s[slot, rows, :], wr_ref[:, cs])
            k = _dot(xk_s[slot, rows, :], wk_ref[:, cs])
            v = _dot(xv_s[slot, rows, :], wv_ref[:, cs])
            wl = _dot(tw_s[slot, rows, :], wlb_ref[:, cs])
            al = _dot(ta_s[slot, rows, :], alb_ref[:, cs])
            gg = _dot(tg_s[slot, rows, :], glb_ref[:, cs])
            return r, k, v, wl, al, gg

        def tail(rows, cs, r, k, v, wl, al, gg):
            lw_out[rows, cs] = -DECAY_SCALE * jax.nn.sigmoid(wl + w0_ref[:, cs])
            a_sig = jax.nn.sigmoid(al + a0_ref[:, cs])
            kk = k * kk_ref[:, cs]
            ss = _split_dot(kk * kk, hsum_ref[...])
            kk = kk * jnp.minimum(lax.rsqrt(ss), 1.0 / L2_EPS)
            r_out[rows, cs] = r.astype(r_out.dtype)
            k_out[rows, cs] = (k * (1.0 + (a_sig - 1.0) * ka_ref[:, cs])).astype(k_out.dtype)
            v_out[rows, cs] = v.astype(v_out.dtype)
            an_out[rows, cs] = (-kk).astype(an_out.dtype)
            bb_out[rows, cs] = (kk * a_sig).astype(bb_out.dtype)
            g_out[rows, cs] = gg.astype(g_out.dtype)

        pending = None
        for u, (rows, cs) in enumerate(units):
            res = matmuls(rows, cs)
            if pending is not None:
                tail(*pending)
            pending = (rows, cs) + res
            if u == 0:
                between()
        tail(*pending)

    @pl.when(i == 0)
    def _():
        prep()

    @pl.when((i > 0) & (i < n_tiles))
    def _():
        project(prep)

    @pl.when(i == n_tiles)
    def _():
        project(lambda: None)


def _head_sum_matrix(n, value=1.0):
    hi = lax.broadcasted_iota(jnp.int32, (n, n), 0) // HEAD
    hj = lax.broadcasted_iota(jnp.int32, (n, n), 1) // HEAD
    return jnp.where(hi == hj, value, 0.0).astype(BF16)


def _proj_layer(h, seq, g, mu, w_r, w_k, w_v, w0, w_la, w_lb, a0, a_la, a_lb, g_la, g_lb, k_k, k_a,
                *, tm=512, tn=512):
    T, D = h.shape
    tm, tn = min(tm, T), min(tn, D)
    slab = min(MXU_WIDTH, tn)
    n_tiles, n_steps = T // tm, D // tn
    dl, al, gl = w_la.shape[1], a_la.shape[1], g_la.shape[1]
    row = lambda x: x.reshape(1, D)
    full = lambda shp: pl.BlockSpec(shp, lambda i, n: (0, 0))
    col = lambda rows: pl.BlockSpec((rows, tn), lambda i, n: (0, n))
    out_spec = pl.BlockSpec((tm, tn), lambda i, n: (jnp.maximum(i - 1, 0), jnp.where(i == 0, 0, n)))
    prev_blocks = tm // 8
    tile = lambda i: jnp.minimum(i, n_tiles - 1)
    out_dtypes = [BF16, F32, BF16, BF16, BF16, BF16, BF16]
    outs = pl.pallas_call(
        functools.partial(_proj_kernel, tm=tm, seq=seq, n_tiles=n_tiles, n_steps=n_steps),
        out_shape=[jax.ShapeDtypeStruct((T, D), dt) for dt in out_dtypes],
        grid=(n_tiles + 1, n_steps),
        in_specs=[
            pl.BlockSpec((tm, D), lambda i, n: (tile(i), 0)),
            pl.BlockSpec((8, D), lambda i, n: (jnp.maximum(tile(i) * prev_blocks - 1, 0), 0)),
            full((1, D)), full((8, D)), full((D, dl)), full((D, al)), full((D, gl)),
            col(D), col(D), col(D), col(dl), col(al), col(gl),
            col(1), col(1), col(1), col(1),
            full((slab, slab)),
        ],
        out_specs=[out_spec] * 7,
        scratch_shapes=[pltpu.VMEM((2, tm, D), BF16)] * 3
        + [pltpu.VMEM((2, tm, dl), BF16), pltpu.VMEM((2, tm, al), BF16), pltpu.VMEM((2, tm, gl), BF16)],
        compiler_params=_params(("arbitrary", "arbitrary")),
        name="rwkv_proj",
    )(h, h, row(g), jnp.pad(mu, ((0, 8 - mu.shape[0]), (0, 0))), w_la, a_la, g_la,
      w_r, w_k, w_v, w_lb, a_lb, g_lb, row(w0), row(a0), row(k_k), row(k_a),
      _head_sum_matrix(slab))
    return outs


def _scan_kernel(r_ref, lw_ref, k_ref, v_ref, an_ref, bb_ref, g_ref, rk_ref, lnw_ref, lnb_ref,
                 hmean_ref, hsum_ref, z_ref,
                 s_ref, y_ref, q_s, yl_s, gm_s, nm_s, et_s, bg_s, *, n_chunks, blocks_per_seq):
    step = pl.program_id(0)

    @pl.when(step == 0)
    def _():
        for ref in (s_ref, q_s, yl_s, gm_s, nm_s, et_s, bg_s):
            ref[...] = jnp.zeros_like(ref)

    two = 2 * CHUNK
    lane_head0 = lax.broadcasted_iota(jnp.int32, (CHUNK, LANES), 1) < HEAD
    ri = lax.broadcasted_iota(jnp.int32, (two, two), 0)
    ci = lax.broadcasted_iota(jnp.int32, (two, two), 1)
    same = (ri >= CHUNK) == (ci >= CHUNK)
    strict = same & (ci < ri)
    incl = same & (ci <= ri)
    eye = jnp.where(ri == ci, 1.0, 0.0).astype(F32)

    def stack(x):
        return jnp.concatenate([jnp.where(lane_head0, x, 0.0), jnp.where(lane_head0, 0.0, x)], axis=0)

    def unstack(x):
        return x[:CHUNK] + x[CHUNK:]

    rng = range(n_chunks)
    sls = [pl.ds(c * CHUNK, CHUNK) for c in rng]

    prev_first = (step + blocks_per_seq - 1) % blocks_per_seq == 0
    carry = {"state": jnp.where(prev_first, 0.0, s_ref[...]), "next": 0}

    def tail_steps(count):
        for c in range(carry["next"], min(carry["next"] + count, n_chunks)):
            state = carry["state"]
            sb = state.astype(BF16)
            y_ref[sls[c], :] = _dot_nt(q_s[c], sb) + yl_s[c]
            carry["state"] = state * et_s[c, 0:1, :] + _dot_nt(sb, gm_s[c]) + nm_s[c]
            carry["next"] = c + 1

    per_stage = -(-n_chunks // 6)
    tail_steps(per_stage)

    lw_all = lw_ref[0]
    cum_all = _chunk_cumsum(lw_all)
    e_in_all = jnp.exp(cum_all)
    e_ex_all = jnp.exp(cum_all - lw_all)
    e_neg_all = jnp.exp(-cum_all)
    a_s, r_s, v_sb, bk_t, e_tot, grams = [], [], [], [], [], []
    for c in rng:
        sl = slice(c * CHUNK, (c + 1) * CHUNK)
        cum = cum_all[sl]
        e_rem = jnp.exp(cum[CHUNK - 1:CHUNK, :] - cum)
        k = k_ref[0, sls[c], :]
        bb = bb_ref[0, sls[c], :]
        a_s.append(stack(an_ref[0, sls[c], :] * e_ex_all[sl]))
        r_s.append(stack(r_ref[0, sls[c], :] * e_in_all[sl]))
        v_sb.append(stack(v_ref[0, sls[c], :]).astype(BF16))
        bk_t.append(jnp.concatenate([stack(bb * e_rem).T, stack(k * e_rem).T], axis=1).astype(BF16))
        e_tot.append(e_in_all[sl][CHUNK - 1:CHUNK, :])
        b_t = (bb * e_neg_all[sl]).astype(BF16)
        k_t = (k * e_neg_all[sl]).astype(BF16)
        lhs = jnp.concatenate([a_s[c], r_s[c]], axis=0).astype(BF16)
        rhs = jnp.concatenate([b_t, b_t, k_t, k_t], axis=0)
        grams.append(_dot_nt(lhs, rhs))
    tail_steps(per_stage)
    p_m = [jnp.where(strict, gm[:two, :two], 0.0) for gm in grams]
    a_34 = [jnp.concatenate([jnp.where(incl, gm[two:, :two], 0.0),
                             jnp.where(incl, gm[two:, two:], 0.0)], axis=1).astype(BF16)
            for gm in grams]
    a2v = [_dot(jnp.where(strict, grams[c][:two, two:], 0.0).astype(BF16), v_sb[c]) for c in rng]

    t_m = [eye + p for p in p_m]
    span = 2
    rounds = 0
    while span < CHUNK:
        tail_steps(per_stage if rounds < 3 else n_chunks)
        if rounds == 3:
            s_ref[...] = carry["state"]
            y = y_ref[...]
            d = y - _split_dot(y, hmean_ref[...])
        if rounds == 4:
            var = _split_dot(d * d, hmean_ref[...])
        p_b = [p.astype(BF16) for p in p_m]
        p_m = [_dot(pb, pb) for pb in p_b]
        t_m = [t_m[c] + _dot(p_m[c].astype(BF16), t_m[c].astype(BF16)) for c in rng]
        span *= 2
        rounds += 1

    x_b = [_dot(t_m[c].astype(BF16), jnp.concatenate([a_s[c], a2v[c]], axis=1).astype(BF16)).astype(BF16)
           for c in rng]
    yn = d * lax.rsqrt(var + GN_EPS) * lnw_ref[...] + lnb_ref[...]
    z_ref[0] = ((yn + bg_s[0]) * bg_s[1]).astype(z_ref.dtype)
    rk = r_ref[0].astype(F32) * k_ref[0].astype(F32) * rk_ref[...]
    bg_s[0] = _split_dot(rk, hsum_ref[...]) * v_ref[0].astype(F32)
    bg_s[1] = g_ref[0].astype(F32)

    for c in rng:
        rhs = jnp.concatenate([x_b[c], jnp.concatenate([jnp.zeros_like(v_sb[c]), v_sb[c]], axis=1)], axis=0)
        out = _dot(jnp.concatenate([a_34[c], bk_t[c]], axis=0), rhs)
        q_s[c] = unstack(r_s[c] + out[:two, :LANES]).astype(BF16)
        yl_s[c] = unstack(out[:two, LANES:])
        gm_s[c] = out[two:, :LANES].astype(BF16)
        nm_s[c] = out[two:, LANES:].T
        et_s[c] = jnp.broadcast_to(e_tot[c], (8, LANES))


def _chunk_cumsum(x):
    t = lax.broadcasted_iota(jnp.int32, x.shape, 0) % CHUNK
    sh = 1
    while sh < CHUNK:
        x = x + jnp.where(t >= sh, pltpu.roll(x, sh, axis=0), 0.0)
        sh *= 2
    return x


def _scan_layer(r, lw, k, v, an, bb, g, r_k, lnx_w, lnx_b, *, n_chunks=8):
    B, S, D = r.shape
    n_chunks = min(n_chunks, S // CHUNK)
    L = n_chunks * CHUNK
    nb, nj = S // L, D // LANES
    n_blocks = B * nj * nb

    def coords(f):
        return f // (nj * nb), f % nb, (f // nb) % nj

    cur = lambda s: coords(jnp.minimum(s, n_blocks - 1))
    prev = lambda s: coords(jnp.maximum(s - 1, 0))
    blk_in = pl.BlockSpec((1, L, LANES), cur)
    vec_in = pl.BlockSpec((1, LANES), lambda s: (0, cur(s)[2]))
    vec_out = pl.BlockSpec((1, LANES), lambda s: (0, prev(s)[2]))
    const = lambda n: pl.BlockSpec((n, n), lambda s: (0, 0))
    two = 2 * CHUNK
    return pl.pallas_call(
        functools.partial(_scan_kernel, n_chunks=n_chunks, blocks_per_seq=nb),
        out_shape=jax.ShapeDtypeStruct((B, S, D), BF16),
        grid=(n_blocks + 1,),
        in_specs=[blk_in] * 7 + [vec_in, vec_out, vec_out] + [const(LANES), const(LANES)],
        out_specs=pl.BlockSpec((1, L, LANES), prev),
        scratch_shapes=[
            pltpu.VMEM((LANES, LANES), F32),
            pltpu.VMEM((L, LANES), F32),
            pltpu.VMEM((n_chunks, CHUNK, LANES), BF16),
            pltpu.VMEM((n_chunks, CHUNK, LANES), F32),
            pltpu.VMEM((n_chunks, two, LANES), BF16),
            pltpu.VMEM((n_chunks, two, LANES), F32),
            pltpu.VMEM((n_chunks, 8, LANES), F32),
            pltpu.VMEM((2, L, LANES), F32),
        ],
        compiler_params=_params(("arbitrary",)),
        name="rwkv_scan",
    )(r, lw, k, v, an, bb, g, r_k.reshape(1, D), lnx_w.reshape(1, D), lnx_b.reshape(1, D),
      _head_sum_matrix(LANES, 1.0 / HEAD), _head_sum_matrix(LANES))


def _oproj_kernel(z_ref, w_ref, h_ref, o_ref):
    o_ref[...] = h_ref[...] + _dot(z_ref[...], w_ref[...])


def _oproj_layer(z, w, h, *, tm=1024, tn=512):
    T, D = h.shape
    tm, tn = min(tm, T), min(tn, D)
    return pl.pallas_call(
        _oproj_kernel,
        out_shape=jax.ShapeDtypeStruct((T, D), F32),
        grid=(T // tm, D // tn),
        in_specs=[
            pl.BlockSpec((tm, D), lambda i, n: (i, 0)),
            pl.BlockSpec((D, tn), lambda i, n: (0, n)),
            pl.BlockSpec((tm, tn), lambda i, n: (i, n)),
        ],
        out_specs=pl.BlockSpec((tm, tn), lambda i, n: (i, n)),
        compiler_params=_params(("parallel", "parallel")),
        name="rwkv_oproj",
    )(z, w, h)


def _cast_kernel(w_ref, o_ref):
    o_ref[...] = w_ref[0].astype(o_ref.dtype)


def _cast_weight(w, layer, *, block_bytes=4 * 1024 * 1024):
    _, R, C = w.shape
    tr = R
    while tr * C * 4 > block_bytes and tr % 16 == 0:
        tr //= 2
    return pl.pallas_call(
        _cast_kernel,
        out_shape=jax.ShapeDtypeStruct((R, C), BF16),
        grid=(R // tr,),
        in_specs=[pl.BlockSpec((1, tr, C), lambda i: (layer, i, 0))],
        out_specs=pl.BlockSpec((tr, C), lambda i: (i, 0)),
        compiler_params=_params(("parallel",)),
        name="cast_bf16",
    )(w)


def kernel(x, norm1_g, norm2_g, final_g, pool_w, pool_b, pool_scale, rw_mu, rw_r, rw_k, rw_v, rw_o, rw_w0, rw_w_la, rw_w_lb, rw_a0, rw_a_la, rw_a_lb, rw_g_la, rw_g_lb, rw_k_k, rw_k_a, rw_r_k, rw_lnx_w, rw_lnx_b, ffn_w1, ffn_w3, ffn_w2):
    B, S, D = x.shape
    T = B * S
    bf = lambda w: w.astype(BF16)
    cast = _cast_weight

    h = _pool_layer(x, norm1_g[0], bf(pool_w[0]), pool_b[0].reshape(-1), pool_scale[0])
    h = _ffn_layer(h.reshape(T, D), norm2_g[0], cast(ffn_w1, 0), cast(ffn_w3, 0), cast(ffn_w2, 0), final_g,
                   final_norm=False)

    r, lw, k, v, an, bb, g = _proj_layer(
        h, S, norm1_g[1], rw_mu[0], cast(rw_r, 0), cast(rw_k, 0), cast(rw_v, 0), rw_w0[0],
        bf(rw_w_la[0]), bf(rw_w_lb[0]), rw_a0[0], bf(rw_a_la[0]), bf(rw_a_lb[0]),
        bf(rw_g_la[0]), bf(rw_g_lb[0]), rw_k_k[0], rw_k_a[0])
    s3 = lambda t: t.reshape(B, S, D)
    z = _scan_layer(s3(r), s3(lw), s3(k), s3(v), s3(an), s3(bb), s3(g),
                    rw_r_k[0].reshape(-1), rw_lnx_w[0], rw_lnx_b[0])
    h = _oproj_layer(z.reshape(T, D), cast(rw_o, 0), h)
    h = _ffn_layer(h, norm2_g[1], cast(ffn_w1, 1), cast(ffn_w3, 1), cast(ffn_w2, 1), final_g,
                   final_norm=True)
    return h.reshape(B, S, D)
```

```python
import functools

import jax
import jax.numpy as jnp
from jax import lax
from jax.experimental import pallas as pl
from jax.experimental.pallas import tpu as pltpu

F32 = jnp.float32
BF16 = jnp.bfloat16

RMS_EPS = 1e-6
GN_EPS = 64e-5
L2_EPS = 1e-12
DECAY_SCALE = 0.6065306597126334
POOL_WINDOWS = (2, 4, 8, 16)
POOL_HALO = 16
HEAD = 64
LANES = 128
MXU_WIDTH = 256
CHUNK = 64
GROUP_LAG = 3
VMEM_LIMIT = 56 * 1024 * 1024


def _rms(x, g):
    return x * lax.rsqrt(jnp.mean(x * x, axis=-1, keepdims=True) + RMS_EPS) * g


def _dot(a, b):
    return jnp.dot(a, b, preferred_element_type=F32)


def _dot_nt(a, b):
    return lax.dot_general(a, b, (((1,), (1,)), ((), ())), preferred_element_type=F32)


def _split_dot(x, w, pieces=2):
    acc = None
    rem = x
    for _ in range(pieces):
        p = rem.astype(BF16)
        rem = rem - p.astype(F32)
        t = _dot(p, w)
        acc = t if acc is None else acc + t
    return acc


def _params(sem):
    return pltpu.CompilerParams(dimension_semantics=sem, vmem_limit_bytes=VMEM_LIMIT)


def _pool_kernel(x_ref, xprev_ref, g_ref, w_ref, b_ref, sc_ref, o_ref, *, ts, gdim):
    i = pl.program_id(1)
    g = g_ref[...]
    xc = x_ref[0]
    hn = _rms(xc, g)
    hp = _rms(xprev_ref[0], g)
    hp = jnp.where(i == 0, 0.0, hp)
    ext = jnp.concatenate([hp, hn], axis=0)
    rows = ts + POOL_HALO
    tau = lax.broadcasted_iota(jnp.int32, (ts, gdim), 0) + i * ts
    for gi, win in enumerate(POOL_WINDOWS):
        lo = gi * gdim
        e = ext[:, lo:lo + gdim]
        acc = e
        step = 1
        while step < win:
            acc = acc + pltpu.roll(acc, step, axis=0)
            step *= 2
        wsum = acc[POOL_HALO:rows]
        cnt = jnp.minimum(tau + 1, win).astype(F32)
        pooled = wsum / cnt - hn[:, lo:lo + gdim]
        mixed = _dot(pooled.astype(BF16), w_ref[gi]) + b_ref[:, lo:lo + gdim]
        o_ref[0, :, lo:lo + gdim] = xc[:, lo:lo + gdim] + mixed * sc_ref[:, lo:lo + gdim]


def _pool_layer(x, g, w_bf, b, scale, *, ts=512):
    B, S, D = x.shape
    G, C, _ = w_bf.shape
    ts = min(ts, S)
    halo_blocks = ts // POOL_HALO
    return pl.pallas_call(
        functools.partial(_pool_kernel, ts=ts, gdim=C),
        out_shape=jax.ShapeDtypeStruct((B, S, D), F32),
        grid=(B, S // ts),
        in_specs=[
            pl.BlockSpec((1, ts, D), lambda b, i: (b, i, 0)),
            pl.BlockSpec((1, POOL_HALO, D), lambda b, i: (b, jnp.maximum(i * halo_blocks - 1, 0), 0)),
            pl.BlockSpec((1, D), lambda b, i: (0, 0)),
            pl.BlockSpec((G, C, C), lambda b, i: (0, 0, 0)),
            pl.BlockSpec((1, D), lambda b, i: (0, 0)),
            pl.BlockSpec((1, D), lambda b, i: (0, 0)),
        ],
        out_specs=pl.BlockSpec((1, ts, D), lambda b, i: (b, i, 0)),
        compiler_params=_params(("parallel", "parallel")),
        name="pool_mixer",
    )(x, x, g.reshape(1, D), w_bf, b.reshape(1, D), scale.reshape(1, D))


def _ffn_kernel(h_ref, g_ref, w1_ref, w3_ref, w2_ref, fg_ref, o_ref, hn_ref, *, n_f, final_norm, tn):
    f = pl.program_id(1)

    @pl.when(f == 0)
    def _():
        h = h_ref[...]
        hn_ref[...] = _rms(h, g_ref[...]).astype(BF16)
        o_ref[...] = h

    hn = hn_ref[...]
    a = _dot(hn, w1_ref[...])
    b = _dot(hn, w3_ref[...])
    act = (a * jax.nn.sigmoid(a) * b).astype(BF16)
    d = o_ref.shape[1]
    for n in range(0, d, tn):
        o_ref[:, n:n + tn] += _dot(act, w2_ref[:, n:n + tn])

    if final_norm:
        @pl.when(f == n_f - 1)
        def _():
            o_ref[...] = _rms(o_ref[...], fg_ref[...])


def _ffn_layer(h, g, w1, w3, w2, final_g, *, final_norm, tm=1024, tf=256, tn=512):
    T, D = h.shape
    F = w1.shape[1]
    tm, tf, tn = min(tm, T), min(tf, F), min(tn, D)
    n_f = F // tf
    return pl.pallas_call(
        functools.partial(_ffn_kernel, n_f=n_f, final_norm=final_norm, tn=tn),
        out_shape=jax.ShapeDtypeStruct((T, D), F32),
        grid=(T // tm, n_f),
        in_specs=[
            pl.BlockSpec((tm, D), lambda i, f: (i, 0)),
            pl.BlockSpec((1, D), lambda i, f: (0, 0)),
            pl.BlockSpec((D, tf), lambda i, f: (0, f)),
            pl.BlockSpec((D, tf), lambda i, f: (0, f)),
            pl.BlockSpec((tf, D), lambda i, f: (f, 0)),
            pl.BlockSpec((1, D), lambda i, f: (0, 0)),
        ],
        out_specs=pl.BlockSpec((tm, D), lambda i, f: (i, 0)),
        scratch_shapes=[pltpu.VMEM((tm, D), BF16)],
        compiler_params=_params(("parallel", "arbitrary")),
        name="ffn_final" if final_norm else "ffn",
    )(h, g.reshape(1, D), w1, w3, w2, final_g.reshape(1, D))


def _proj_kernel(h_ref, hprev_ref, g_ref, mu_ref, wla_ref, ala_ref, gla_ref,
                 wr_ref, wk_ref, wv_ref, wlb_ref, alb_ref, glb_ref,
                 w0_ref, a0_ref, kk_ref, ka_ref, hsum_ref,
                 r_out, lw_out, k_out, v_out, an_out, bb_out, g_out,
                 xr_s, xk_s, xv_s, tw_s, ta_s, tg_s, *, tm, seq, n_tiles, n_steps):
    i = pl.program_id(0)
    n = pl.program_id(1)
    rq = tm // n_steps
    slab = hsum_ref.shape[0]
    slabs = [slice(lo, lo + slab) for lo in range(0, r_out.shape[1], slab)]

    def prep():
        slot = i % 2
        r0 = pl.multiple_of(n * rq, rq)
        dst = pl.ds(r0, rq)
        g = g_ref[...]
        hn = _rms(h_ref[dst, :], g)
        inside = h_ref[pl.ds(pl.multiple_of(jnp.maximum(r0 - 8, 0), 8), 8), :]
        prev8 = jnp.where(n == 0, hprev_ref[...], inside)
        hp = _rms(prev8[7:8, :], g)
        hp = jnp.where((n == 0) & ((i * tm) % seq == 0), 0.0, hp)
        row = lax.broadcasted_iota(jnp.int32, hn.shape, 0)
        shifted = jnp.where(row == 0, hp, pltpu.roll(hn, 1, axis=0))
        xx = shifted - hn
        xr_s[slot, dst, :] = (hn + xx * mu_ref[0:1, :]).astype(BF16)
        xk_s[slot, dst, :] = (hn + xx * mu_ref[2:3, :]).astype(BF16)
        xv_s[slot, dst, :] = (hn + xx * mu_ref[3:4, :]).astype(BF16)
        xw = (hn + xx * mu_ref[1:2, :]).astype(BF16)
        tw_s[slot, dst, :] = jnp.tanh(_dot(xw, wla_ref[...])).astype(BF16)
        xa = (hn + xx * mu_ref[4:5, :]).astype(BF16)
        ta_s[slot, dst, :] = _dot(xa, ala_ref[...]).astype(BF16)
        xg = (hn + xx * mu_ref[5:6, :]).astype(BF16)
        tg_s[slot, dst, :] = jax.nn.sigmoid(_dot(xg, gla_ref[...])).astype(BF16)

    def project(between):
        slot = (i + 1) % 2
        half = tm
        units = [(pl.ds(r0, half), cs) for r0 in range(0, tm, half) for cs in slabs]

        def matmuls(rows, cs):
            r = _dot(xr_s[slot, rows, :], wr_ref[:, cs])
            k = _dot(xk_s[slot, rows, :], wk_ref[:, cs])
            v = _dot(xv_s[slot, rows, :], wv_ref[:, cs])
            wl = _dot(tw_s[slot, rows, :], wlb_ref[:, cs])
            al = _dot(ta_s[slot, rows, :], alb_ref[:, cs])
            gg = _dot(tg_s[slot, rows, :], glb_ref[:, cs])
            return r, k, v, wl, al, gg

        def tail(rows, cs, r, k, v, wl, al, gg):
            lw_out[rows, cs] = -DECAY_SCALE * jax.nn.sigmoid(wl + w0_ref[:, cs])
            a_sig = jax.nn.sigmoid(al + a0_ref[:, cs])
            kk = k * kk_ref[:, cs]
            ss = _split_dot(kk * kk, hsum_ref[...])
            kk = kk * jnp.minimum(lax.rsqrt(ss), 1.0 / L2_EPS)
            r_out[rows, cs] = r.astype(r_out.dtype)
            k_out[rows, cs] = (k * (1.0 + (a_sig - 1.0) * ka_ref[:, cs])).astype(k_out.dtype)
            v_out[rows, cs] = v.astype(v_out.dtype)
            an_out[rows, cs] = (-kk).astype(an_out.dtype)
            bb_out[rows, cs] = (kk * a_sig).astype(bb_out.dtype)
            g_out[rows, cs] = gg.astype(g_out.dtype)

        pending = None
        for u, (rows, cs) in enumerate(units):
            res = matmuls(rows, cs)
            if pending is not None:
                tail(*pending)
            pending = (rows, cs) + res
            if u == 0:
                between()
        tail(*pending)

    @pl.when(i == 0)
    def _():
        prep()

    @pl.when((i > 0) & (i < n_tiles))
    def _():
        project(prep)

    @pl.when(i == n_tiles)
    def _():
        project(lambda: None)


def _head_sum_matrix(n, value=1.0):
    hi = lax.broadcasted_iota(jnp.int32, (n, n), 0) // HEAD
    hj = lax.broadcasted_iota(jnp.int32, (n, n), 1) // HEAD
    return jnp.where(hi == hj, value, 0.0).astype(BF16)


def _proj_layer(h, seq, g, mu, w_r, w_k, w_v, w0, w_la, w_lb, a0, a_la, a_lb, g_la, g_lb, k_k, k_a,
                *, tm=512, tn=512):
    T, D = h.shape
    tm, tn = min(tm, T), min(tn, D)
    slab = min(MXU_WIDTH, tn)
    n_tiles, n_steps = T // tm, D // tn
    dl, al, gl = w_la.shape[1], a_la.shape[1], g_la.shape[1]
    row = lambda x: x.reshape(1, D)
    full = lambda shp: pl.BlockSpec(shp, lambda i, n: (0, 0))
    col = lambda rows: pl.BlockSpec((rows, tn), lambda i, n: (0, n))
    out_spec = pl.BlockSpec((tm, tn), lambda i, n: (jnp.maximum(i - 1, 0), jnp.where(i == 0, 0, n)))
    prev_blocks = tm // 8
    tile = lambda i: jnp.minimum(i, n_tiles - 1)
    out_dtypes = [BF16, F32, BF16, BF16, BF16, BF16, BF16]
    outs = pl.pallas_call(
        functools.partial(_proj_kernel, tm=tm, seq=seq, n_tiles=n_tiles, n_steps=n_steps),
        out_shape=[jax.ShapeDtypeStruct((T, D), dt) for dt in out_dtypes],
        grid=(n_tiles + 1, n_steps),
        in_specs=[
            pl.BlockSpec((tm, D), lambda i, n: (tile(i), 0)),
            pl.BlockSpec((8, D), lambda i, n: (jnp.maximum(tile(i) * prev_blocks - 1, 0), 0)),
            full((1, D)), full((8, D)), full((D, dl)), full((D, al)), full((D, gl)),
            col(D), col(D), col(D), col(dl), col(al), col(gl),
            col(1), col(1), col(1), col(1),
            full((slab, slab)),
        ],
        out_specs=[out_spec] * 7,
        scratch_shapes=[pltpu.VMEM((2, tm, D), BF16)] * 3
        + [pltpu.VMEM((2, tm, dl), BF16), pltpu.VMEM((2, tm, al), BF16), pltpu.VMEM((2, tm, gl), BF16)],
        compiler_params=_params(("arbitrary", "arbitrary")),
        name="rwkv_proj",
    )(h, h, row(g), jnp.pad(mu, ((0, 8 - mu.shape[0]), (0, 0))), w_la, a_la, g_la,
      w_r, w_k, w_v, w_lb, a_lb, g_lb, row(w0), row(a0), row(k_k), row(k_a),
      _head_sum_matrix(slab))
    return outs


def _scan_kernel(r_ref, lw_ref, k_ref, v_ref, an_ref, bb_ref, g_ref, rk_ref, lnw_ref, lnb_ref,
                 hmean_ref, hsum_ref, z_ref,
                 s_ref, y_ref, q_s, yl_s, gm_s, nm_s, et_s, bg_s, *, n_chunks, n_groups, blocks_per_seq):
    step = pl.program_id(0)

    @pl.when(step == 0)
    def _():
        for ref in (s_ref, q_s, yl_s, gm_s, nm_s, et_s, bg_s):
            ref[...] = jnp.zeros_like(ref)

    two = 2 * CHUNK
    lane_head0 = lax.broadcasted_iota(jnp.int32, (CHUNK, LANES), 1) < HEAD
    ri = lax.broadcasted_iota(jnp.int32, (two, two), 0)
    ci = lax.broadcasted_iota(jnp.int32, (two, two), 1)
    same = (ri >= CHUNK) == (ci >= CHUNK)
    strict = same & (ci < ri)
    incl = same & (ci <= ri)
    eye = jnp.where(ri == ci, 1.0, 0.0).astype(F32)

    def stack(x):
        return jnp.concatenate([jnp.where(lane_head0, x, 0.0), jnp.where(lane_head0, 0.0, x)], axis=0)

    def unstack(x):
        return x[:CHUNK] + x[CHUNK:]

    sls = [pl.ds(c * CHUNK, CHUNK) for c in range(n_chunks)]

    prev_first = (step + blocks_per_seq - 1) % blocks_per_seq == 0
    carry = {"state": jnp.where(prev_first, 0.0, s_ref[...]), "next": 0}

    def tail_steps(count):
        for c in range(carry["next"], min(carry["next"] + count, n_chunks)):
            state = carry["state"]
            sb = state.astype(BF16)
            y_ref[sls[c], :] = _dot_nt(q_s[c], sb) + yl_s[c]
            carry["state"] = state * et_s[c, 0:1, :] + _dot_nt(sb, gm_s[c]) + nm_s[c]
            carry["next"] = c + 1

    def group_stages(chunks):
        n = len(chunks)
        rng = range(n)
        v = {}

        def front():
            rows = pl.ds(chunks[0] * CHUNK, n * CHUNK)
            lw_all = lw_ref[0, rows, :]
            cum_all = _chunk_cumsum(lw_all)
            e_in_all = jnp.exp(cum_all)
            e_ex_all = jnp.exp(cum_all - lw_all)
            e_neg_all = jnp.exp(-cum_all)
            v["a_s"], v["r_s"], v["v_sb"], v["bk_t"], v["e_tot"], v["gram"] = [], [], [], [], [], []
            for c in rng:
                sl = slice(c * CHUNK, (c + 1) * CHUNK)
                src = sls[chunks[c]]
                cum = cum_all[sl]
                e_rem = jnp.exp(cum[CHUNK - 1:CHUNK, :] - cum)
                k = k_ref[0, src, :]
                bb = bb_ref[0, src, :]
                a_s = stack(an_ref[0, src, :] * e_ex_all[sl])
                r_s = stack(r_ref[0, src, :] * e_in_all[sl])
                v["a_s"].append(a_s)
                v["r_s"].append(r_s)
                v["v_sb"].append(stack(v_ref[0, src, :]).astype(BF16))
                v["bk_t"].append(
                    jnp.concatenate([stack(bb * e_rem).T, stack(k * e_rem).T], axis=1).astype(BF16))
                v["e_tot"].append(e_in_all[sl][CHUNK - 1:CHUNK, :])
                b_t = (bb * e_neg_all[sl]).astype(BF16)
                k_t = (k * e_neg_all[sl]).astype(BF16)
                lhs = jnp.concatenate([a_s, r_s], axis=0).astype(BF16)
                rhs = jnp.concatenate([b_t, b_t, k_t, k_t], axis=0)
                v["gram"].append(_dot_nt(lhs, rhs))

        def masks():
            gram = v.pop("gram")
            v["p_m"] = [jnp.where(strict, gm[:two, :two], 0.0) for gm in gram]
            v["a_34"] = [jnp.concatenate([jnp.where(incl, gm[two:, :two], 0.0),
                                          jnp.where(incl, gm[two:, two:], 0.0)], axis=1).astype(BF16)
                         for gm in gram]
            v["a2v"] = [_dot(jnp.where(strict, gram[c][:two, two:], 0.0).astype(BF16), v["v_sb"][c])
                        for c in rng]
            v["t_m"] = [eye + p for p in v["p_m"]]

        def square():
            p_b = [p.astype(BF16) for p in v["p_m"]]
            v["p_m"] = [_dot(pb, pb) for pb in p_b]

        def doubling():
            p_b = [p.astype(BF16) for p in v["p_m"]]
            out = [_dot(jnp.concatenate([v["t_m"][c].astype(BF16), p_b[c]], axis=0), p_b[c]) for c in rng]
            v["t_m"] = [v["t_m"][c] + out[c][:two] for c in rng]
            v["p_m"] = [o[two:] for o in out]

        def last_doubling():
            v["t_m"] = [v["t_m"][c] + _dot(v["t_m"][c].astype(BF16), v["p_m"][c].astype(BF16)) for c in rng]

        def solve():
            v["x_b"] = [_dot(v["t_m"][c].astype(BF16),
                             jnp.concatenate([v["a_s"][c], v["a2v"][c]], axis=1).astype(BF16)).astype(BF16)
                        for c in rng]

        def handover():
            for c in rng:
                v_sb = v["v_sb"][c]
                rhs = jnp.concatenate([v["x_b"][c], jnp.concatenate([jnp.zeros_like(v_sb), v_sb], axis=1)],
                                      axis=0)
                out = _dot(jnp.concatenate([v["a_34"][c], v["bk_t"][c]], axis=0), rhs)
                dst = chunks[c]
                q_s[dst] = unstack(v["r_s"][c] + out[:two, :LANES]).astype(BF16)
                yl_s[dst] = unstack(out[:two, LANES:])
                gm_s[dst] = out[two:, :LANES].astype(BF16)
                nm_s[dst] = out[two:, LANES:].T
                et_s[dst] = jnp.broadcast_to(v["e_tot"][c], (8, LANES))

        n_rounds = CHUNK.bit_length() - 2
        return [front, masks, square] + [doubling] * (n_rounds - 1) + [last_doubling, solve, handover]

    ep = {}

    def epilogue_mean():
        s_ref[...] = carry["state"]
        y = y_ref[...]
        ep["d"] = y - _split_dot(y, hmean_ref[...])

    def epilogue_var():
        ep["var"] = _split_dot(ep["d"] * ep["d"], hmean_ref[...])

    def epilogue_store():
        yn = ep["d"] * lax.rsqrt(ep["var"] + GN_EPS) * lnw_ref[...] + lnb_ref[...]
        z_ref[0] = ((yn + bg_s[0]) * bg_s[1]).astype(z_ref.dtype)
        rk = r_ref[0].astype(F32) * k_ref[0].astype(F32) * rk_ref[...]
        bg_s[0] = _split_dot(rk, hsum_ref[...]) * v_ref[0].astype(F32)
        bg_s[1] = g_ref[0].astype(F32)

    size = -(-n_chunks // n_groups)
    groups = [group_stages(list(range(lo, min(lo + size, n_chunks)))) for lo in range(0, n_chunks, size)]
    n_stages = len(groups[0])
    slots = n_stages - 1 + GROUP_LAG * (len(groups) - 1)
    per_slot = -(-n_chunks // max(slots - 3, 1))
    for t in range(slots):
        tail_steps(per_slot)
        if t == slots - 3:
            tail_steps(n_chunks)
            epilogue_mean()
        if t == slots - 2:
            epilogue_var()
        for gi, stages in enumerate(groups):
            k_stage = t - GROUP_LAG * gi
            if 0 <= k_stage < n_stages - 1:
                stages[k_stage]()
    epilogue_store()
    for stages in groups:
        stages[-1]()


def _chunk_cumsum(x):
    t = lax.broadcasted_iota(jnp.int32, x.shape, 0) % CHUNK
    sh = 1
    while sh < CHUNK:
        x = x + jnp.where(t >= sh, pltpu.roll(x, sh, axis=0), 0.0)
        sh *= 2
    return x


def _scan_layer(r, lw, k, v, an, bb, g, r_k, lnx_w, lnx_b, *, n_chunks=16, n_groups=2):
    B, S, D = r.shape
    n_chunks = min(n_chunks, S // CHUNK)
    L = n_chunks * CHUNK
    nb, nj = S // L, D // LANES
    n_blocks = B * nj * nb

    def coords(f):
        return f // (nj * nb), f % nb, (f // nb) % nj

    cur = lambda s: coords(jnp.minimum(s, n_blocks - 1))
    prev = lambda s: coords(jnp.maximum(s - 1, 0))
    blk_in = pl.BlockSpec((1, L, LANES), cur)
    vec_in = pl.BlockSpec((1, LANES), lambda s: (0, cur(s)[2]))
    vec_out = pl.BlockSpec((1, LANES), lambda s: (0, prev(s)[2]))
    const = lambda n: pl.BlockSpec((n, n), lambda s: (0, 0))
    two = 2 * CHUNK
    return pl.pallas_call(
        functools.partial(_scan_kernel, n_chunks=n_chunks, n_groups=n_groups, blocks_per_seq=nb),
        out_shape=jax.ShapeDtypeStruct((B, S, D), BF16),
        grid=(n_blocks + 1,),
        in_specs=[blk_in] * 7 + [vec_in, vec_out, vec_out] + [const(LANES), const(LANES)],
        out_specs=pl.BlockSpec((1, L, LANES), prev),
        scratch_shapes=[
            pltpu.VMEM((LANES, LANES), F32),
            pltpu.VMEM((L, LANES), F32),
            pltpu.VMEM((n_chunks, CHUNK, LANES), BF16),
            pltpu.VMEM((n_chunks, CHUNK, LANES), F32),
            pltpu.VMEM((n_chunks, two, LANES), BF16),
            pltpu.VMEM((n_chunks, two, LANES), F32),
            pltpu.VMEM((n_chunks, 8, LANES), F32),
            pltpu.VMEM((2, L, LANES), F32),
        ],
        compiler_params=_params(("arbitrary",)),
        name="rwkv_scan",
    )(r, lw, k, v, an, bb, g, r_k.reshape(1, D), lnx_w.reshape(1, D), lnx_b.reshape(1, D),
      _head_sum_matrix(LANES, 1.0 / HEAD), _head_sum_matrix(LANES))


def _oproj_kernel(z_ref, w_ref, h_ref, o_ref):
    o_ref[...] = h_ref[...] + _dot(z_ref[...], w_ref[...])


def _oproj_layer(z, w, h, *, tm=1024, tn=512):
    T, D = h.shape
    tm, tn = min(tm, T), min(tn, D)
    return pl.pallas_call(
        _oproj_kernel,
        out_shape=jax.ShapeDtypeStruct((T, D), F32),
        grid=(T // tm, D // tn),
        in_specs=[
            pl.BlockSpec((tm, D), lambda i, n: (i, 0)),
            pl.BlockSpec((D, tn), lambda i, n: (0, n)),
            pl.BlockSpec((tm, tn), lambda i, n: (i, n)),
        ],
        out_specs=pl.BlockSpec((tm, tn), lambda i, n: (i, n)),
        compiler_params=_params(("parallel", "parallel")),
        name="rwkv_oproj",
    )(z, w, h)


def _cast_kernel(w_ref, o_ref):
    o_ref[...] = w_ref[0].astype(o_ref.dtype)


def _cast_weight(w, layer, *, block_bytes=4 * 1024 * 1024):
    _, R, C = w.shape
    tr = R
    while tr * C * 4 > block_bytes and tr % 16 == 0:
        tr //= 2
    return pl.pallas_call(
        _cast_kernel,
        out_shape=jax.ShapeDtypeStruct((R, C), BF16),
        grid=(R // tr,),
        in_specs=[pl.BlockSpec((1, tr, C), lambda i: (layer, i, 0))],
        out_specs=pl.BlockSpec((tr, C), lambda i: (i, 0)),
        compiler_params=_params(("parallel",)),
        name="cast_bf16",
    )(w)


def kernel(x, norm1_g, norm2_g, final_g, pool_w, pool_b, pool_scale, rw_mu, rw_r, rw_k, rw_v, rw_o, rw_w0, rw_w_la, rw_w_lb, rw_a0, rw_a_la, rw_a_lb, rw_g_la, rw_g_lb, rw_k_k, rw_k_a, rw_r_k, rw_lnx_w, rw_lnx_b, ffn_w1, ffn_w3, ffn_w2):
    B, S, D = x.shape
    T = B * S
    bf = lambda w: w.astype(BF16)
    cast = _cast_weight

    h = _pool_layer(x, norm1_g[0], bf(pool_w[0]), pool_b[0].reshape(-1), pool_scale[0])
    h = _ffn_layer(h.reshape(T, D), norm2_g[0], cast(ffn_w1, 0), cast(ffn_w3, 0), cast(ffn_w2, 0), final_g,
                   final_norm=False)

    r, lw, k, v, an, bb, g = _proj_layer(
        h, S, norm1_g[1], rw_mu[0], cast(rw_r, 0), cast(rw_k, 0), cast(rw_v, 0), rw_w0[0],
        bf(rw_w_la[0]), bf(rw_w_lb[0]), rw_a0[0], bf(rw_a_la[0]), bf(rw_a_lb[0]),
        bf(rw_g_la[0]), bf(rw_g_lb[0]), rw_k_k[0], rw_k_a[0])
    s3 = lambda t: t.reshape(B, S, D)
    z = _scan_layer(s3(r), s3(lw), s3(k), s3(v), s3(an), s3(bb), s3(g),
                    rw_r_k[0].reshape(-1), rw_lnx_w[0], rw_lnx_b[0])
    h = _oproj_layer(z.reshape(T, D), cast(rw_o, 0), h)
    h = _ffn_layer(h, norm2_g[1], cast(ffn_w1, 1), cast(ffn_w3, 1), cast(ffn_w2, 1), final_g,
                   final_norm=True)
    return h.reshape(B, S, D)
```

```python
import functools

import jax
import jax.numpy as jnp
from jax import lax
from jax.experimental import pallas as pl
from jax.experimental.pallas import tpu as pltpu

F32 = jnp.float32
BF16 = jnp.bfloat16

RMS_EPS = 1e-6
GN_EPS = 64e-5
L2_EPS = 1e-12
DECAY_SCALE = 0.6065306597126334
POOL_WINDOWS = (2, 4, 8, 16)
POOL_HALO = 16
HEAD = 64
LANES = 128
MXU_WIDTH = 256
CHUNK = 64
GROUP_LAG = 3
VMEM_LIMIT = 56 * 1024 * 1024


def _rms(x, g):
    return x * lax.rsqrt(jnp.mean(x * x, axis=-1, keepdims=True) + RMS_EPS) * g


def _dot(a, b):
    return jnp.dot(a, b, preferred_element_type=F32)


def _dot_nt(a, b):
    return lax.dot_general(a, b, (((1,), (1,)), ((), ())), preferred_element_type=F32)


def _split_dot(x, w, pieces=2):
    acc = None
    rem = x
    for _ in range(pieces):
        p = rem.astype(BF16)
        rem = rem - p.astype(F32)
        t = _dot(p, w)
        acc = t if acc is None else acc + t
    return acc


def _params(sem):
    return pltpu.CompilerParams(dimension_semantics=sem, vmem_limit_bytes=VMEM_LIMIT)


def _pool_kernel(x_ref, xprev_ref, g_ref, w_ref, b_ref, sc_ref, o_ref, *, ts, gdim):
    i = pl.program_id(1)
    g = g_ref[...]
    xc = x_ref[0]
    hn = _rms(xc, g)
    hp = _rms(xprev_ref[0], g)
    hp = jnp.where(i == 0, 0.0, hp)
    ext = jnp.concatenate([hp, hn], axis=0)
    rows = ts + POOL_HALO
    tau = lax.broadcasted_iota(jnp.int32, (ts, gdim), 0) + i * ts
    for gi, win in enumerate(POOL_WINDOWS):
        lo = gi * gdim
        e = ext[:, lo:lo + gdim]
        acc = e
        step = 1
        while step < win:
            acc = acc + pltpu.roll(acc, step, axis=0)
            step *= 2
        wsum = acc[POOL_HALO:rows]
        cnt = jnp.minimum(tau + 1, win).astype(F32)
        pooled = wsum / cnt - hn[:, lo:lo + gdim]
        mixed = _dot(pooled.astype(BF16), w_ref[gi]) + b_ref[:, lo:lo + gdim]
        o_ref[0, :, lo:lo + gdim] = xc[:, lo:lo + gdim] + mixed * sc_ref[:, lo:lo + gdim]


def _pool_layer(x, g, w_bf, b, scale, *, ts=512):
    B, S, D = x.shape
    G, C, _ = w_bf.shape
    ts = min(ts, S)
    halo_blocks = ts // POOL_HALO
    return pl.pallas_call(
        functools.partial(_pool_kernel, ts=ts, gdim=C),
        out_shape=jax.ShapeDtypeStruct((B, S, D), F32),
        grid=(B, S // ts),
        in_specs=[
            pl.BlockSpec((1, ts, D), lambda b, i: (b, i, 0)),
            pl.BlockSpec((1, POOL_HALO, D), lambda b, i: (b, jnp.maximum(i * halo_blocks - 1, 0), 0)),
            pl.BlockSpec((1, D), lambda b, i: (0, 0)),
            pl.BlockSpec((G, C, C), lambda b, i: (0, 0, 0)),
            pl.BlockSpec((1, D), lambda b, i: (0, 0)),
            pl.BlockSpec((1, D), lambda b, i: (0, 0)),
        ],
        out_specs=pl.BlockSpec((1, ts, D), lambda b, i: (b, i, 0)),
        compiler_params=_params(("parallel", "parallel")),
        name="pool_mixer",
    )(x, x, g.reshape(1, D), w_bf, b.reshape(1, D), scale.reshape(1, D))


def _ffn_kernel(h_ref, g_ref, w1_ref, w3_ref, w2_ref, fg_ref, *rest, n_f, final_norm, tn):
    n_side = (len(rest) - 2) // 2
    side_in, o_ref, side_out, hn_ref = rest[:n_side], rest[n_side], rest[n_side + 1:-1], rest[-1]
    _side_casts(side_in, side_out)
    f = pl.program_id(1)

    @pl.when(f == 0)
    def _():
        h = h_ref[...]
        hn_ref[...] = _rms(h, g_ref[...]).astype(BF16)
        o_ref[...] = h

    hn = hn_ref[...]
    a = _dot(hn, w1_ref[...])
    b = _dot(hn, w3_ref[...])
    act = (a * jax.nn.sigmoid(a) * b).astype(BF16)
    d = o_ref.shape[1]
    for n in range(0, d, tn):
        o_ref[:, n:n + tn] += _dot(act, w2_ref[:, n:n + tn])

    if final_norm:
        @pl.when(f == n_f - 1)
        def _():
            o_ref[...] = _rms(o_ref[...], fg_ref[...])


def _ffn_layer(h, g, w1, w3, w2, final_g, *, final_norm, side=(), tm=1024, tf=256, tn=512):
    T, D = h.shape
    F = w1.shape[1]
    tm, tf, tn = min(tm, T), min(tf, F), min(tn, D)
    n_f = F // tf
    n_i = T // tm
    side_in, side_out, side_shapes = _side_cast_specs(side, n_i * n_f, lambda i, f: i * n_f + f)
    outs = pl.pallas_call(
        functools.partial(_ffn_kernel, n_f=n_f, final_norm=final_norm, tn=tn),
        out_shape=[jax.ShapeDtypeStruct((T, D), F32)] + side_shapes,
        grid=(n_i, n_f),
        in_specs=[
            pl.BlockSpec((tm, D), lambda i, f: (i, 0)),
            pl.BlockSpec((1, D), lambda i, f: (0, 0)),
            pl.BlockSpec((D, tf), lambda i, f: (0, f)),
            pl.BlockSpec((D, tf), lambda i, f: (0, f)),
            pl.BlockSpec((tf, D), lambda i, f: (f, 0)),
            pl.BlockSpec((1, D), lambda i, f: (0, 0)),
        ] + side_in,
        out_specs=[pl.BlockSpec((tm, D), lambda i, f: (i, 0))] + side_out,
        scratch_shapes=[pltpu.VMEM((tm, D), BF16)],
        compiler_params=_params(("arbitrary", "arbitrary")),
        name="ffn_final" if final_norm else "ffn",
    )(h, g.reshape(1, D), w1, w3, w2, final_g.reshape(1, D), *[w for w, _ in side])
    return outs


def _side_casts(side_in, side_out):
    for w_ref, o_ref in zip(side_in, side_out):
        o_ref[...] = w_ref[0].astype(o_ref.dtype)


def _side_cast_specs(side, n_steps, flat_step):
    in_specs, out_specs, shapes = [], [], []
    for w, layer in side:
        _, R, C = w.shape
        tr = next(t for t in range(16, R + 1, 16) if R % t == 0 and R // t <= n_steps)
        last = R // tr - 1

        def blk(*idx, last=last):
            return jnp.minimum(flat_step(*idx), last)

        in_specs.append(pl.BlockSpec((1, tr, C), lambda *idx, blk=blk, layer=layer: (layer, blk(*idx), 0)))
        out_specs.append(pl.BlockSpec((tr, C), lambda *idx, blk=blk: (blk(*idx), 0)))
        shapes.append(jax.ShapeDtypeStruct((R, C), BF16))
    return in_specs, out_specs, shapes


def _proj_kernel(h_ref, hprev_ref, g_ref, mu_ref, wla_ref, ala_ref, gla_ref,
                 wr_ref, wk_ref, wv_ref, wlb_ref, alb_ref, glb_ref,
                 w0_ref, a0_ref, kk_ref, ka_ref, hsum_ref,
                 r_out, lw_out, k_out, v_out, an_out, bb_out, g_out,
                 xr_s, xk_s, xv_s, tw_s, ta_s, tg_s, *, tm, seq, n_tiles, n_steps):
    i = pl.program_id(0)
    n = pl.program_id(1)
    rq = tm // n_steps
    slab = hsum_ref.shape[0]
    slabs = [slice(lo, lo + slab) for lo in range(0, r_out.shape[1], slab)]

    def prep():
        slot = i % 2
        r0 = pl.multiple_of(n * rq, rq)
        dst = pl.ds(r0, rq)
        g = g_ref[...]
        hn = _rms(h_ref[dst, :], g)
        inside = h_ref[pl.ds(pl.multiple_of(jnp.maximum(r0 - 8, 0), 8), 8), :]
        prev8 = jnp.where(n == 0, hprev_ref[...], inside)
        hp = _rms(prev8[7:8, :], g)
        hp = jnp.where((n == 0) & ((i * tm) % seq == 0), 0.0, hp)
        row = lax.broadcasted_iota(jnp.int32, hn.shape, 0)
        shifted = jnp.where(row == 0, hp, pltpu.roll(hn, 1, axis=0))
        xx = shifted - hn
        xr_s[slot, dst, :] = (hn + xx * mu_ref[0:1, :]).astype(BF16)
        xk_s[slot, dst, :] = (hn + xx * mu_ref[2:3, :]).astype(BF16)
        xv_s[slot, dst, :] = (hn + xx * mu_ref[3:4, :]).astype(BF16)
        xw = (hn + xx * mu_ref[1:2, :]).astype(BF16)
        tw_s[slot, dst, :] = jnp.tanh(_dot(xw, wla_ref[...])).astype(BF16)
        xa = (hn + xx * mu_ref[4:5, :]).astype(BF16)
        ta_s[slot, dst, :] = _dot(xa, ala_ref[...]).astype(BF16)
        xg = (hn + xx * mu_ref[5:6, :]).astype(BF16)
        tg_s[slot, dst, :] = jax.nn.sigmoid(_dot(xg, gla_ref[...])).astype(BF16)

    def project(between):
        slot = (i + 1) % 2
        half = tm
        units = [(pl.ds(r0, half), cs) for r0 in range(0, tm, half) for cs in slabs]

        def matmuls(rows, cs):
            r = _dot(xr_s[slot, rows, :], wr_ref[:, cs])
            k = _dot(xk_s[slot, rows, :], wk_ref[:, cs])
            v = _dot(xv_s[slot, rows, :], wv_ref[:, cs])
            wl = _dot(tw_s[slot, rows, :], wlb_ref[:, cs])
            al = _dot(ta_s[slot, rows, :], alb_ref[:, cs])
            gg = _dot(tg_s[slot, rows, :], glb_ref[:, cs])
            return r, k, v, wl, al, gg

        def tail(rows, cs, r, k, v, wl, al, gg):
            lw_out[rows, cs] = -DECAY_SCALE * jax.nn.sigmoid(wl + w0_ref[:, cs])
            a_sig = jax.nn.sigmoid(al + a0_ref[:, cs])
            kk = k * kk_ref[:, cs]
            ss = _split_dot(kk * kk, hsum_ref[...])
            kk = kk * jnp.minimum(lax.rsqrt(ss), 1.0 / L2_EPS)
            r_out[rows, cs] = r.astype(r_out.dtype)
            k_out[rows, cs] = (k * (1.0 + (a_sig - 1.0) * ka_ref[:, cs])).astype(k_out.dtype)
            v_out[rows, cs] = v.astype(v_out.dtype)
            an_out[rows, cs] = (-kk).astype(an_out.dtype)
            bb_out[rows, cs] = (kk * a_sig).astype(bb_out.dtype)
            g_out[rows, cs] = gg.astype(g_out.dtype)

        pending = None
        for u, (rows, cs) in enumerate(units):
            res = matmuls(rows, cs)
            if pending is not None:
                tail(*pending)
            pending = (rows, cs) + res
            if u == 0:
                between()
        tail(*pending)

    @pl.when(i == 0)
    def _():
        prep()

    @pl.when((i > 0) & (i < n_tiles))
    def _():
        project(prep)

    @pl.when(i == n_tiles)
    def _():
        project(lambda: None)


def _head_sum_matrix(n, value=1.0):
    hi = lax.broadcasted_iota(jnp.int32, (n, n), 0) // HEAD
    hj = lax.broadcasted_iota(jnp.int32, (n, n), 1) // HEAD
    return jnp.where(hi == hj, value, 0.0).astype(BF16)


def _proj_layer(h, seq, g, mu, w_r, w_k, w_v, w0, w_la, w_lb, a0, a_la, a_lb, g_la, g_lb, k_k, k_a,
                *, tm=512, tn=512):
    T, D = h.shape
    tm, tn = min(tm, T), min(tn, D)
    slab = min(MXU_WIDTH, tn)
    n_tiles, n_steps = T // tm, D // tn
    dl, al, gl = w_la.shape[1], a_la.shape[1], g_la.shape[1]
    row = lambda x: x.reshape(1, D)
    full = lambda shp: pl.BlockSpec(shp, lambda i, n: (0, 0))
    col = lambda rows: pl.BlockSpec((rows, tn), lambda i, n: (0, n))
    out_spec = pl.BlockSpec((tm, tn), lambda i, n: (jnp.maximum(i - 1, 0), jnp.where(i == 0, 0, n)))
    prev_blocks = tm // 8
    tile = lambda i: jnp.minimum(i, n_tiles - 1)
    out_dtypes = [BF16, F32, BF16, BF16, BF16, BF16, BF16]
    outs = pl.pallas_call(
        functools.partial(_proj_kernel, tm=tm, seq=seq, n_tiles=n_tiles, n_steps=n_steps),
        out_shape=[jax.ShapeDtypeStruct((T, D), dt) for dt in out_dtypes],
        grid=(n_tiles + 1, n_steps),
        in_specs=[
            pl.BlockSpec((tm, D), lambda i, n: (tile(i), 0)),
            pl.BlockSpec((8, D), lambda i, n: (jnp.maximum(tile(i) * prev_blocks - 1, 0), 0)),
            full((1, D)), full((8, D)), full((D, dl)), full((D, al)), full((D, gl)),
            col(D), col(D), col(D), col(dl), col(al), col(gl),
            col(1), col(1), col(1), col(1),
            full((slab, slab)),
        ],
        out_specs=[out_spec] * 7,
        scratch_shapes=[pltpu.VMEM((2, tm, D), BF16)] * 3
        + [pltpu.VMEM((2, tm, dl), BF16), pltpu.VMEM((2, tm, al), BF16), pltpu.VMEM((2, tm, gl), BF16)],
        compiler_params=_params(("arbitrary", "arbitrary")),
        name="rwkv_proj",
    )(h, h, row(g), jnp.pad(mu, ((0, 8 - mu.shape[0]), (0, 0))), w_la, a_la, g_la,
      w_r, w_k, w_v, w_lb, a_lb, g_lb, row(w0), row(a0), row(k_k), row(k_a),
      _head_sum_matrix(slab))
    return outs


def _scan_kernel(r_ref, lw_ref, k_ref, v_ref, an_ref, bb_ref, g_ref, rk_ref, lnw_ref, lnb_ref,
                 hmean_ref, hsum_ref, *rest, n_chunks, n_groups, blocks_per_seq):
    n_side = (len(rest) - 9) // 2
    side_in, z_ref, side_out = rest[:n_side], rest[n_side], rest[n_side + 1:2 * n_side + 1]
    s_ref, y_ref, q_s, yl_s, gm_s, nm_s, et_s, bg_s = rest[2 * n_side + 1:]
    _side_casts(side_in, side_out)
    step = pl.program_id(0)

    @pl.when(step == 0)
    def _():
        for ref in (s_ref, q_s, yl_s, gm_s, nm_s, et_s, bg_s):
            ref[...] = jnp.zeros_like(ref)

    two = 2 * CHUNK
    lane_head0 = lax.broadcasted_iota(jnp.int32, (CHUNK, LANES), 1) < HEAD
    ri = lax.broadcasted_iota(jnp.int32, (two, two), 0)
    ci = lax.broadcasted_iota(jnp.int32, (two, two), 1)
    same = (ri >= CHUNK) == (ci >= CHUNK)
    strict = same & (ci < ri)
    incl = same & (ci <= ri)
    eye = jnp.where(ri == ci, 1.0, 0.0).astype(F32)

    def stack(x):
        return jnp.concatenate([jnp.where(lane_head0, x, 0.0), jnp.where(lane_head0, 0.0, x)], axis=0)

    def unstack(x):
        return x[:CHUNK] + x[CHUNK:]

    sls = [pl.ds(c * CHUNK, CHUNK) for c in range(n_chunks)]

    prev_first = (step + blocks_per_seq - 1) % blocks_per_seq == 0
    carry = {"state": jnp.where(prev_first, 0.0, s_ref[...]), "next": 0}

    def tail_steps(count):
        for c in range(carry["next"], min(carry["next"] + count, n_chunks)):
            state = carry["state"]
            sb = state.astype(BF16)
            y_ref[sls[c], :] = _dot_nt(q_s[c], sb) + yl_s[c]
            carry["state"] = state * et_s[c, 0:1, :] + _dot_nt(sb, gm_s[c]) + nm_s[c]
            carry["next"] = c + 1

    def group_stages(chunks):
        n = len(chunks)
        rng = range(n)
        v = {}

        def front():
            rows = pl.ds(chunks[0] * CHUNK, n * CHUNK)
            lw_all = lw_ref[0, rows, :]
            cum_all = _chunk_cumsum(lw_all)
            e_in_all = jnp.exp(cum_all)
            e_ex_all = jnp.exp(cum_all - lw_all)
            e_neg_all = jnp.exp(-cum_all)
            v["a_s"], v["r_s"], v["v_sb"], v["bk_t"], v["e_tot"], v["gram"] = [], [], [], [], [], []
            for c in rng:
                sl = slice(c * CHUNK, (c + 1) * CHUNK)
                src = sls[chunks[c]]
                cum = cum_all[sl]
                e_rem = jnp.exp(cum[CHUNK - 1:CHUNK, :] - cum)
                k = k_ref[0, src, :]
                bb = bb_ref[0, src, :]
                a_s = stack(an_ref[0, src, :] * e_ex_all[sl])
                r_s = stack(r_ref[0, src, :] * e_in_all[sl])
                v["a_s"].append(a_s)
                v["r_s"].append(r_s)
                v["v_sb"].append(stack(v_ref[0, src, :]).astype(BF16))
                v["bk_t"].append(
                    jnp.concatenate([stack(bb * e_rem).T, stack(k * e_rem).T], axis=1).astype(BF16))
                v["e_tot"].append(e_in_all[sl][CHUNK - 1:CHUNK, :])
                b_t = (bb * e_neg_all[sl]).astype(BF16)
                k_t = (k * e_neg_all[sl]).astype(BF16)
                lhs = jnp.concatenate([a_s, r_s], axis=0).astype(BF16)
                rhs = jnp.concatenate([b_t, b_t, k_t, k_t], axis=0)
                v["gram"].append(_dot_nt(lhs, rhs))

        def masks():
            gram = v.pop("gram")
            v["p_m"] = [jnp.where(strict, gm[:two, :two], 0.0) for gm in gram]
            v["a_34"] = [jnp.concatenate([jnp.where(incl, gm[two:, :two], 0.0),
                                          jnp.where(incl, gm[two:, two:], 0.0)], axis=1).astype(BF16)
                         for gm in gram]
            v["a2v"] = [_dot(jnp.where(strict, gram[c][:two, two:], 0.0).astype(BF16), v["v_sb"][c])
                        for c in rng]
            v["t_m"] = [eye + p for p in v["p_m"]]

        def square():
            p_b = [p.astype(BF16) for p in v["p_m"]]
            v["p_m"] = [_dot(pb, pb) for pb in p_b]

        def doubling():
            p_b = [p.astype(BF16) for p in v["p_m"]]
            out = [_dot(jnp.concatenate([v["t_m"][c].astype(BF16), p_b[c]], axis=0), p_b[c]) for c in rng]
            v["t_m"] = [v["t_m"][c] + out[c][:two] for c in rng]
            v["p_m"] = [o[two:] for o in out]

        def last_doubling():
            v["t_m"] = [v["t_m"][c] + _dot(v["t_m"][c].astype(BF16), v["p_m"][c].astype(BF16)) for c in rng]

        def solve():
            v["x_b"] = [_dot(v["t_m"][c].astype(BF16),
                             jnp.concatenate([v["a_s"][c], v["a2v"][c]], axis=1).astype(BF16)).astype(BF16)
                        for c in rng]

        def handover():
            for c in rng:
                v_sb = v["v_sb"][c]
                rhs = jnp.concatenate([v["x_b"][c], jnp.concatenate([jnp.zeros_like(v_sb), v_sb], axis=1)],
                                      axis=0)
                out = _dot(jnp.concatenate([v["a_34"][c], v["bk_t"][c]], axis=0), rhs)
                dst = chunks[c]
                q_s[dst] = unstack(v["r_s"][c] + out[:two, :LANES]).astype(BF16)
                yl_s[dst] = unstack(out[:two, LANES:])
                gm_s[dst] = out[two:, :LANES].astype(BF16)
                nm_s[dst] = out[two:, LANES:].T
                et_s[dst] = jnp.broadcast_to(v["e_tot"][c], (8, LANES))

        n_rounds = CHUNK.bit_length() - 2
        return [front, masks, square] + [doubling] * (n_rounds - 1) + [last_doubling, solve, handover]

    ep = {}

    def epilogue_mean():
        s_ref[...] = carry["state"]
        y = y_ref[...]
        ep["d"] = y - _split_dot(y, hmean_ref[...])

    def epilogue_var():
        ep["var"] = _split_dot(ep["d"] * ep["d"], hmean_ref[...])

    def epilogue_store():
        yn = ep["d"] * lax.rsqrt(ep["var"] + GN_EPS) * lnw_ref[...] + lnb_ref[...]
        z_ref[0] = ((yn + bg_s[0]) * bg_s[1]).astype(z_ref.dtype)
        rk = r_ref[0].astype(F32) * k_ref[0].astype(F32) * rk_ref[...]
        bg_s[0] = _split_dot(rk, hsum_ref[...]) * v_ref[0].astype(F32)
        bg_s[1] = g_ref[0].astype(F32)

    size = -(-n_chunks // n_groups)
    groups = [group_stages(list(range(lo, min(lo + size, n_chunks)))) for lo in range(0, n_chunks, size)]
    n_stages = len(groups[0])
    slots = n_stages - 1 + GROUP_LAG * (len(groups) - 1)
    per_slot = -(-n_chunks // max(slots - 3, 1))
    for t in range(slots):
        tail_steps(per_slot)
        if t == slots - 3:
            tail_steps(n_chunks)
            epilogue_mean()
        if t == slots - 2:
            epilogue_var()
        for gi, stages in enumerate(groups):
            k_stage = t - GROUP_LAG * gi
            if 0 <= k_stage < n_stages - 1:
                stages[k_stage]()
    epilogue_store()
    for stages in groups:
        stages[-1]()


def _chunk_cumsum(x):
    t = lax.broadcasted_iota(jnp.int32, x.shape, 0) % CHUNK
    sh = 1
    while sh < CHUNK:
        x = x + jnp.where(t >= sh, pltpu.roll(x, sh, axis=0), 0.0)
        sh *= 2
    return x


def _scan_layer(r, lw, k, v, an, bb, g, r_k, lnx_w, lnx_b, *, side=(), n_chunks=16, n_groups=2):
    B, S, D = r.shape
    n_chunks = min(n_chunks, S // CHUNK)
    L = n_chunks * CHUNK
    nb, nj = S // L, D // LANES
    n_blocks = B * nj * nb

    def coords(f):
        return f // (nj * nb), f % nb, (f // nb) % nj

    cur = lambda s: coords(jnp.minimum(s, n_blocks - 1))
    prev = lambda s: coords(jnp.maximum(s - 1, 0))
    blk_in = pl.BlockSpec((1, L, LANES), cur)
    vec_in = pl.BlockSpec((1, LANES), lambda s: (0, cur(s)[2]))
    vec_out = pl.BlockSpec((1, LANES), lambda s: (0, prev(s)[2]))
    const = lambda n: pl.BlockSpec((n, n), lambda s: (0, 0))
    two = 2 * CHUNK
    side_in, side_out, side_shapes = _side_cast_specs(side, n_blocks + 1, lambda s: s)
    return pl.pallas_call(
        functools.partial(_scan_kernel, n_chunks=n_chunks, n_groups=n_groups, blocks_per_seq=nb),
        out_shape=[jax.ShapeDtypeStruct((B, S, D), BF16)] + side_shapes,
        grid=(n_blocks + 1,),
        in_specs=[blk_in] * 7 + [vec_in, vec_out, vec_out] + [const(LANES), const(LANES)] + side_in,
        out_specs=[pl.BlockSpec((1, L, LANES), prev)] + side_out,
        scratch_shapes=[
            pltpu.VMEM((LANES, LANES), F32),
            pltpu.VMEM((L, LANES), F32),
            pltpu.VMEM((n_chunks, CHUNK, LANES), BF16),
            pltpu.VMEM((n_chunks, CHUNK, LANES), F32),
            pltpu.VMEM((n_chunks, two, LANES), BF16),
            pltpu.VMEM((n_chunks, two, LANES), F32),
            pltpu.VMEM((n_chunks, 8, LANES), F32),
            pltpu.VMEM((2, L, LANES), F32),
        ],
        compiler_params=_params(("arbitrary",)),
        name="rwkv_scan",
    )(r, lw, k, v, an, bb, g, r_k.reshape(1, D), lnx_w.reshape(1, D), lnx_b.reshape(1, D),
      _head_sum_matrix(LANES, 1.0 / HEAD), _head_sum_matrix(LANES), *[w for w, _ in side])


def _oproj_kernel(z_ref, w_ref, h_ref, o_ref):
    o_ref[...] = h_ref[...] + _dot(z_ref[...], w_ref[...])


def _oproj_layer(z, w, h, *, tm=1024, tn=512):
    T, D = h.shape
    tm, tn = min(tm, T), min(tn, D)
    return pl.pallas_call(
        _oproj_kernel,
        out_shape=jax.ShapeDtypeStruct((T, D), F32),
        grid=(T // tm, D // tn),
        in_specs=[
            pl.BlockSpec((tm, D), lambda i, n: (i, 0)),
            pl.BlockSpec((D, tn), lambda i, n: (0, n)),
            pl.BlockSpec((tm, tn), lambda i, n: (i, n)),
        ],
        out_specs=pl.BlockSpec((tm, tn), lambda i, n: (i, n)),
        compiler_params=_params(("parallel", "parallel")),
        name="rwkv_oproj",
    )(z, w, h)


def _cast_kernel(w_ref, o_ref):
    o_ref[...] = w_ref[0].astype(o_ref.dtype)


def _cast_weight(w, layer, *, block_bytes=4 * 1024 * 1024):
    _, R, C = w.shape
    tr = R
    while tr * C * 4 > block_bytes and tr % 16 == 0:
        tr //= 2
    return pl.pallas_call(
        _cast_kernel,
        out_shape=jax.ShapeDtypeStruct((R, C), BF16),
        grid=(R // tr,),
        in_specs=[pl.BlockSpec((1, tr, C), lambda i: (layer, i, 0))],
        out_specs=pl.BlockSpec((tr, C), lambda i: (i, 0)),
        compiler_params=_params(("parallel",)),
        name="cast_bf16",
    )(w)


def kernel(x, norm1_g, norm2_g, final_g, pool_w, pool_b, pool_scale, rw_mu, rw_r, rw_k, rw_v, rw_o, rw_w0, rw_w_la, rw_w_lb, rw_a0, rw_a_la, rw_a_lb, rw_g_la, rw_g_lb, rw_k_k, rw_k_a, rw_r_k, rw_lnx_w, rw_lnx_b, ffn_w1, ffn_w3, ffn_w2):
    B, S, D = x.shape
    T = B * S
    bf = lambda w: w.astype(BF16)
    cast = _cast_weight

    h = _pool_layer(x, norm1_g[0], bf(pool_w[0]), pool_b[0].reshape(-1), pool_scale[0])
    h, w_r, w_k, w_v, w_o = _ffn_layer(
        h.reshape(T, D), norm2_g[0], cast(ffn_w1, 0), cast(ffn_w3, 0), cast(ffn_w2, 0), final_g,
        final_norm=False, side=((rw_r, 0), (rw_k, 0), (rw_v, 0), (rw_o, 0)))

    r, lw, k, v, an, bb, g = _proj_layer(
        h, S, norm1_g[1], rw_mu[0], w_r, w_k, w_v, rw_w0[0],
        bf(rw_w_la[0]), bf(rw_w_lb[0]), rw_a0[0], bf(rw_a_la[0]), bf(rw_a_lb[0]),
        bf(rw_g_la[0]), bf(rw_g_lb[0]), rw_k_k[0], rw_k_a[0])
    s3 = lambda t: t.reshape(B, S, D)
    z, w1, w3, w2 = _scan_layer(s3(r), s3(lw), s3(k), s3(v), s3(an), s3(bb), s3(g),
                                rw_r_k[0].reshape(-1), rw_lnx_w[0], rw_lnx_b[0],
                                side=((ffn_w1, 1), (ffn_w3, 1), (ffn_w2, 1)))
    h = _oproj_layer(z.reshape(T, D), w_o, h)
    h, = _ffn_layer(h, norm2_g[1], w1, w3, w2, final_g, final_norm=True)
    return h.reshape(B, S, D)
```

```python
import functools

import jax
import jax.numpy as jnp
from jax import lax
from jax.experimental import pallas as pl
from jax.experimental.pallas import tpu as pltpu

F32 = jnp.float32
BF16 = jnp.bfloat16

RMS_EPS = 1e-6
GN_EPS = 64e-5
L2_EPS = 1e-12
DECAY_SCALE = 0.6065306597126334
POOL_WINDOWS = (2, 4, 8, 16)
POOL_HALO = 16
HEAD = 64
LANES = 128
MXU_WIDTH = 256
CHUNK = 64
GROUP_LAG = 3
VMEM_LIMIT = 56 * 1024 * 1024


def _rms(x, g):
    return x * lax.rsqrt(jnp.mean(x * x, axis=-1, keepdims=True) + RMS_EPS) * g


def _dot(a, b):
    return jnp.dot(a, b, preferred_element_type=F32)


def _dot_nt(a, b):
    return lax.dot_general(a, b, (((1,), (1,)), ((), ())), preferred_element_type=F32)


def _split_dot(x, w, pieces=2):
    acc = None
    rem = x
    for _ in range(pieces):
        p = rem.astype(BF16)
        rem = rem - p.astype(F32)
        t = _dot(p, w)
        acc = t if acc is None else acc + t
    return acc


def _params(sem):
    return pltpu.CompilerParams(dimension_semantics=sem, vmem_limit_bytes=VMEM_LIMIT)


def _pool_kernel(x_ref, xprev_ref, g_ref, w_ref, b_ref, sc_ref, o_ref, *, ts, gdim):
    i = pl.program_id(1)
    g = g_ref[...]
    xc = x_ref[0]
    hn = _rms(xc, g)
    hp = _rms(xprev_ref[0], g)
    hp = jnp.where(i == 0, 0.0, hp)
    ext = jnp.concatenate([hp, hn], axis=0)
    rows = ts + POOL_HALO
    tau = lax.broadcasted_iota(jnp.int32, (ts, gdim), 0) + i * ts
    for gi, win in enumerate(POOL_WINDOWS):
        lo = gi * gdim
        e = ext[:, lo:lo + gdim]
        acc = e
        step = 1
        while step < win:
            acc = acc + pltpu.roll(acc, step, axis=0)
            step *= 2
        wsum = acc[POOL_HALO:rows]
        cnt = jnp.minimum(tau + 1, win).astype(F32)
        pooled = wsum / cnt - hn[:, lo:lo + gdim]
        mixed = _dot(pooled.astype(BF16), w_ref[gi]) + b_ref[:, lo:lo + gdim]
        o_ref[0, :, lo:lo + gdim] = xc[:, lo:lo + gdim] + mixed * sc_ref[:, lo:lo + gdim]


def _pool_layer(x, g, w_bf, b, scale, *, ts=512):
    B, S, D = x.shape
    G, C, _ = w_bf.shape
    ts = min(ts, S)
    halo_blocks = ts // POOL_HALO
    return pl.pallas_call(
        functools.partial(_pool_kernel, ts=ts, gdim=C),
        out_shape=jax.ShapeDtypeStruct((B, S, D), F32),
        grid=(B, S // ts),
        in_specs=[
            pl.BlockSpec((1, ts, D), lambda b, i: (b, i, 0)),
            pl.BlockSpec((1, POOL_HALO, D), lambda b, i: (b, jnp.maximum(i * halo_blocks - 1, 0), 0)),
            pl.BlockSpec((1, D), lambda b, i: (0, 0)),
            pl.BlockSpec((G, C, C), lambda b, i: (0, 0, 0)),
            pl.BlockSpec((1, D), lambda b, i: (0, 0)),
            pl.BlockSpec((1, D), lambda b, i: (0, 0)),
        ],
        out_specs=pl.BlockSpec((1, ts, D), lambda b, i: (b, i, 0)),
        compiler_params=_params(("parallel", "parallel")),
        name="pool_mixer",
    )(x, x, g.reshape(1, D), w_bf, b.reshape(1, D), scale.reshape(1, D))


def _ffn_kernel(h_ref, g_ref, w1_ref, w3_ref, w2_ref, fg_ref, *rest, n_i, n_f, final_norm, tn):
    n_side = (len(rest) - 4) // 2
    side_in, o_hbm, side_out = rest[:n_side], rest[n_side], rest[n_side + 1:2 * n_side + 1]
    hn_ref, acc_ref, sem = rest[2 * n_side + 1:]
    _side_casts(side_in, side_out)
    i = pl.program_id(0)
    f = pl.program_id(1)
    tm, d = acc_ref.shape
    slabs = [slice(n, n + tn) for n in range(0, d, tn)]

    def out_copy(k, tile):
        return pltpu.make_async_copy(acc_ref.at[:, slabs[k]], o_hbm.at[pl.ds(tile * tm, tm), slabs[k]],
                                     sem.at[k])

    def wait_out(tile):
        for k in range(len(slabs)):
            out_copy(k, tile).wait()

    @pl.when(f == 0)
    def _():
        @pl.when(i > 0)
        def _():
            wait_out(i - 1)
        h = h_ref[...]
        hn_ref[...] = _rms(h, g_ref[...]).astype(BF16)
        acc_ref[...] = h

    def body(is_last):
        hn = hn_ref[...]
        a = _dot(hn, w1_ref[...])
        b = _dot(hn, w3_ref[...])
        act = (a * jax.nn.sigmoid(a) * b).astype(BF16)
        for k, cs in enumerate(slabs):
            acc_ref[:, cs] += _dot(act, w2_ref[:, cs])
            if is_last and not final_norm:
                out_copy(k, i).start()
        if is_last and final_norm:
            x = acc_ref[...]
            scale = lax.rsqrt(jnp.mean(x * x, axis=-1, keepdims=True) + RMS_EPS)
            for k, cs in enumerate(slabs):
                acc_ref[:, cs] = x[:, cs] * scale * fg_ref[:, cs]
                out_copy(k, i).start()

    @pl.when(f < n_f - 1)
    def _():
        body(False)

    @pl.when(f == n_f - 1)
    def _():
        body(True)

        @pl.when(i == n_i - 1)
        def _():
            wait_out(i)


def _ffn_layer(h, g, w1, w3, w2, final_g, *, final_norm, side=(), tm=1024, tf=512, tn=512):
    T, D = h.shape
    F = w1.shape[1]
    tm, tf, tn = min(tm, T), min(tf, F), min(tn, D)
    while F % tf:
        tf -= LANES
    assert T % tm == 0 and D % tn == 0 and tf > 0
    n_f = F // tf
    n_i = T // tm
    side_in, side_out, side_shapes = _side_cast_specs(side, n_i * n_f, lambda i, f: i * n_f + f)
    outs = pl.pallas_call(
        functools.partial(_ffn_kernel, n_i=n_i, n_f=n_f, final_norm=final_norm, tn=tn),
        out_shape=[jax.ShapeDtypeStruct((T, D), F32)] + side_shapes,
        grid=(n_i, n_f),
        in_specs=[
            pl.BlockSpec((tm, D), lambda i, f: (i, 0)),
            pl.BlockSpec((1, D), lambda i, f: (0, 0)),
            pl.BlockSpec((D, tf), lambda i, f: (0, f)),
            pl.BlockSpec((D, tf), lambda i, f: (0, f)),
            pl.BlockSpec((tf, D), lambda i, f: (f, 0)),
            pl.BlockSpec((1, D), lambda i, f: (0, 0)),
        ] + side_in,
        out_specs=[pl.BlockSpec(memory_space=pl.ANY)] + side_out,
        scratch_shapes=[pltpu.VMEM((tm, D), BF16), pltpu.VMEM((tm, D), F32),
                        pltpu.SemaphoreType.DMA((D // tn,))],
        compiler_params=_params(("arbitrary", "arbitrary")),
        name="ffn_final" if final_norm else "ffn",
    )(h, g.reshape(1, D), w1, w3, w2, final_g.reshape(1, D), *[w for w, _ in side])
    return outs


def _side_casts(side_in, side_out):
    for w_ref, o_ref in zip(side_in, side_out):
        o_ref[...] = w_ref[0].astype(o_ref.dtype)


def _side_cast_specs(side, n_steps, flat_step):
    in_specs, out_specs, shapes = [], [], []
    for w, layer in side:
        _, R, C = w.shape
        tr = next(t for t in range(16, R + 1, 16) if R % t == 0 and R // t <= n_steps)
        last = R // tr - 1

        def blk(*idx, last=last):
            return jnp.minimum(flat_step(*idx), last)

        in_specs.append(pl.BlockSpec((1, tr, C), lambda *idx, blk=blk, layer=layer: (layer, blk(*idx), 0)))
        out_specs.append(pl.BlockSpec((tr, C), lambda *idx, blk=blk: (blk(*idx), 0)))
        shapes.append(jax.ShapeDtypeStruct((R, C), BF16))
    return in_specs, out_specs, shapes


def _proj_kernel(h_ref, hprev_ref, g_ref, mu_ref, wla_ref, ala_ref, gla_ref,
                 wr_ref, wk_ref, wv_ref, wlb_ref, alb_ref, glb_ref,
                 w0_ref, a0_ref, kk_ref, ka_ref, hsum_ref,
                 r_out, lw_out, k_out, v_out, an_out, bb_out, g_out,
                 xr_s, xk_s, xv_s, tw_s, ta_s, tg_s, *, tm, seq, n_tiles, n_steps):
    i = pl.program_id(0)
    n = pl.program_id(1)
    rq = tm // n_steps
    slab = hsum_ref.shape[0]
    slabs = [slice(lo, lo + slab) for lo in range(0, r_out.shape[1], slab)]

    def prep():
        slot = i % 2
        r0 = pl.multiple_of(n * rq, rq)
        dst = pl.ds(r0, rq)
        g = g_ref[...]
        hn = _rms(h_ref[dst, :], g)
        inside = h_ref[pl.ds(pl.multiple_of(jnp.maximum(r0 - 8, 0), 8), 8), :]
        prev8 = jnp.where(n == 0, hprev_ref[...], inside)
        hp = _rms(prev8[7:8, :], g)
        hp = jnp.where((n == 0) & ((i * tm) % seq == 0), 0.0, hp)
        row = lax.broadcasted_iota(jnp.int32, hn.shape, 0)
        shifted = jnp.where(row == 0, hp, pltpu.roll(hn, 1, axis=0))
        xx = shifted - hn
        xr_s[slot, dst, :] = (hn + xx * mu_ref[0:1, :]).astype(BF16)
        xk_s[slot, dst, :] = (hn + xx * mu_ref[2:3, :]).astype(BF16)
        xv_s[slot, dst, :] = (hn + xx * mu_ref[3:4, :]).astype(BF16)
        xw = (hn + xx * mu_ref[1:2, :]).astype(BF16)
        tw_s[slot, dst, :] = jnp.tanh(_dot(xw, wla_ref[...])).astype(BF16)
        xa = (hn + xx * mu_ref[4:5, :]).astype(BF16)
        ta_s[slot, dst, :] = _dot(xa, ala_ref[...]).astype(BF16)
        xg = (hn + xx * mu_ref[5:6, :]).astype(BF16)
        tg_s[slot, dst, :] = jax.nn.sigmoid(_dot(xg, gla_ref[...])).astype(BF16)

    def project(between):
        slot = (i + 1) % 2
        half = tm
        units = [(pl.ds(r0, half), cs) for r0 in range(0, tm, half) for cs in slabs]

        def matmuls(rows, cs):
            r = _dot(xr_s[slot, rows, :], wr_ref[:, cs])
            k = _dot(xk_s[slot, rows, :], wk_ref[:, cs])
            v = _dot(xv_s[slot, rows, :], wv_ref[:, cs])
            wl = _dot(tw_s[slot, rows, :], wlb_ref[:, cs])
            al = _dot(ta_s[slot, rows, :], alb_ref[:, cs])
            gg = _dot(tg_s[slot, rows, :], glb_ref[:, cs])
            return r, k, v, wl, al, gg

        def tail(rows, cs, r, k, v, wl, al, gg):
            lw_out[rows, cs] = -DECAY_SCALE * jax.nn.sigmoid(wl + w0_ref[:, cs])
            a_sig = jax.nn.sigmoid(al + a0_ref[:, cs])
            kk = k * kk_ref[:, cs]
            ss = _split_dot(kk * kk, hsum_ref[...])
            kk = kk * jnp.minimum(lax.rsqrt(ss), 1.0 / L2_EPS)
            r_out[rows, cs] = r.astype(r_out.dtype)
            k_out[rows, cs] = (k * (1.0 + (a_sig - 1.0) * ka_ref[:, cs])).astype(k_out.dtype)
            v_out[rows, cs] = v.astype(v_out.dtype)
            an_out[rows, cs] = (-kk).astype(an_out.dtype)
            bb_out[rows, cs] = (kk * a_sig).astype(bb_out.dtype)
            g_out[rows, cs] = gg.astype(g_out.dtype)

        pending = None
        for u, (rows, cs) in enumerate(units):
            res = matmuls(rows, cs)
            if pending is not None:
                tail(*pending)
            pending = (rows, cs) + res
            if u == 0:
                between()
        tail(*pending)

    @pl.when(i == 0)
    def _():
        prep()

    @pl.when((i > 0) & (i < n_tiles))
    def _():
        project(prep)

    @pl.when(i == n_tiles)
    def _():
        project(lambda: None)


def _head_sum_matrix(n, value=1.0):
    hi = lax.broadcasted_iota(jnp.int32, (n, n), 0) // HEAD
    hj = lax.broadcasted_iota(jnp.int32, (n, n), 1) // HEAD
    return jnp.where(hi == hj, value, 0.0).astype(BF16)


def _proj_layer(h, seq, g, mu, w_r, w_k, w_v, w0, w_la, w_lb, a0, a_la, a_lb, g_la, g_lb, k_k, k_a,
                *, tm=512, tn=512):
    T, D = h.shape
    tm, tn = min(tm, T), min(tn, D)
    slab = min(MXU_WIDTH, tn)
    n_tiles, n_steps = T // tm, D // tn
    dl, al, gl = w_la.shape[1], a_la.shape[1], g_la.shape[1]
    row = lambda x: x.reshape(1, D)
    full = lambda shp: pl.BlockSpec(shp, lambda i, n: (0, 0))
    col = lambda rows: pl.BlockSpec((rows, tn), lambda i, n: (0, n))
    out_spec = pl.BlockSpec((tm, tn), lambda i, n: (jnp.maximum(i - 1, 0), jnp.where(i == 0, 0, n)))
    prev_blocks = tm // 8
    tile = lambda i: jnp.minimum(i, n_tiles - 1)
    out_dtypes = [BF16, F32, BF16, BF16, BF16, BF16, BF16]
    outs = pl.pallas_call(
        functools.partial(_proj_kernel, tm=tm, seq=seq, n_tiles=n_tiles, n_steps=n_steps),
        out_shape=[jax.ShapeDtypeStruct((T, D), dt) for dt in out_dtypes],
        grid=(n_tiles + 1, n_steps),
        in_specs=[
            pl.BlockSpec((tm, D), lambda i, n: (tile(i), 0)),
            pl.BlockSpec((8, D), lambda i, n: (jnp.maximum(tile(i) * prev_blocks - 1, 0), 0)),
            full((1, D)), full((8, D)), full((D, dl)), full((D, al)), full((D, gl)),
            col(D), col(D), col(D), col(dl), col(al), col(gl),
            col(1), col(1), col(1), col(1),
            full((slab, slab)),
        ],
        out_specs=[out_spec] * 7,
        scratch_shapes=[pltpu.VMEM((2, tm, D), BF16)] * 3
        + [pltpu.VMEM((2, tm, dl), BF16), pltpu.VMEM((2, tm, al), BF16), pltpu.VMEM((2, tm, gl), BF16)],
        compiler_params=_params(("arbitrary", "arbitrary")),
        name="rwkv_proj",
    )(h, h, row(g), jnp.pad(mu, ((0, 8 - mu.shape[0]), (0, 0))), w_la, a_la, g_la,
      w_r, w_k, w_v, w_lb, a_lb, g_lb, row(w0), row(a0), row(k_k), row(k_a),
      _head_sum_matrix(slab))
    return outs


def _scan_kernel(r_ref, lw_ref, k_ref, v_ref, an_ref, bb_ref, g_ref, rk_ref, lnw_ref, lnb_ref,
                 hmean_ref, hsum_ref, *rest, n_chunks, n_groups, blocks_per_seq):
    n_side = (len(rest) - 9) // 2
    side_in, z_ref, side_out = rest[:n_side], rest[n_side], rest[n_side + 1:2 * n_side + 1]
    s_ref, y_ref, q_s, yl_s, gm_s, nm_s, et_s, bg_s = rest[2 * n_side + 1:]
    _side_casts(side_in, side_out)
    step = pl.program_id(0)

    @pl.when(step == 0)
    def _():
        for ref in (s_ref, q_s, yl_s, gm_s, nm_s, et_s, bg_s):
            ref[...] = jnp.zeros_like(ref)

    two = 2 * CHUNK
    lane_head0 = lax.broadcasted_iota(jnp.int32, (CHUNK, LANES), 1) < HEAD
    ri = lax.broadcasted_iota(jnp.int32, (two, two), 0)
    ci = lax.broadcasted_iota(jnp.int32, (two, two), 1)
    same = (ri >= CHUNK) == (ci >= CHUNK)
    strict = same & (ci < ri)
    incl = same & (ci <= ri)
    eye = jnp.where(ri == ci, 1.0, 0.0).astype(F32)

    def stack(x):
        return jnp.concatenate([jnp.where(lane_head0, x, 0.0), jnp.where(lane_head0, 0.0, x)], axis=0)

    def unstack(x):
        return x[:CHUNK] + x[CHUNK:]

    sls = [pl.ds(c * CHUNK, CHUNK) for c in range(n_chunks)]

    prev_first = (step + blocks_per_seq - 1) % blocks_per_seq == 0
    carry = {"state": jnp.where(prev_first, 0.0, s_ref[...]), "next": 0}

    def tail_steps(count):
        for c in range(carry["next"], min(carry["next"] + count, n_chunks)):
            state = carry["state"]
            sb = state.astype(BF16)
            y_ref[sls[c], :] = _dot_nt(q_s[c], sb) + yl_s[c]
            carry["state"] = state * et_s[c, 0:1, :] + _dot_nt(sb, gm_s[c]) + nm_s[c]
            carry["next"] = c + 1

    def group_stages(chunks):
        n = len(chunks)
        rng = range(n)
        v = {}

        def front():
            rows = pl.ds(chunks[0] * CHUNK, n * CHUNK)
            lw_all = lw_ref[0, rows, :]
            cum_all = _chunk_cumsum(lw_all)
            e_in_all = jnp.exp(cum_all)
            e_ex_all = jnp.exp(cum_all - lw_all)
            e_neg_all = jnp.exp(-cum_all)
            v["a_s"], v["r_s"], v["v_sb"], v["bk_t"], v["e_tot"], v["gram"] = [], [], [], [], [], []
            for c in rng:
                sl = slice(c * CHUNK, (c + 1) * CHUNK)
                src = sls[chunks[c]]
                cum = cum_all[sl]
                e_rem = jnp.exp(cum[CHUNK - 1:CHUNK, :] - cum)
                k = k_ref[0, src, :]
                bb = bb_ref[0, src, :]
                a_s = stack(an_ref[0, src, :] * e_ex_all[sl])
                r_s = stack(r_ref[0, src, :] * e_in_all[sl])
                v["a_s"].append(a_s)
                v["r_s"].append(r_s)
                v["v_sb"].append(stack(v_ref[0, src, :]).astype(BF16))
                v["bk_t"].append(
                    jnp.concatenate([stack(bb * e_rem).T, stack(k * e_rem).T], axis=1).astype(BF16))
                v["e_tot"].append(e_in_all[sl][CHUNK - 1:CHUNK, :])
                b_t = (bb * e_neg_all[sl]).astype(BF16)
                k_t = (k * e_neg_all[sl]).astype(BF16)
                lhs = jnp.concatenate([a_s, r_s], axis=0).astype(BF16)
                rhs = jnp.concatenate([b_t, b_t, k_t, k_t], axis=0)
                v["gram"].append(_dot_nt(lhs, rhs))

        def masks():
            gram = v.pop("gram")
            v["p_m"] = [jnp.where(strict, gm[:two, :two], 0.0) for gm in gram]
            v["a_34"] = [jnp.concatenate([jnp.where(incl, gm[two:, :two], 0.0),
                                          jnp.where(incl, gm[two:, two:], 0.0)], axis=1).astype(BF16)
                         for gm in gram]
            v["a2v"] = [_dot(jnp.where(strict, gram[c][:two, two:], 0.0).astype(BF16), v["v_sb"][c])
                        for c in rng]
            v["t_m"] = [eye + p for p in v["p_m"]]

        def square():
            p_b = [p.astype(BF16) for p in v["p_m"]]
            v["p_m"] = [_dot(pb, pb) for pb in p_b]

        def doubling():
            p_b = [p.astype(BF16) for p in v["p_m"]]
            out = [_dot(jnp.concatenate([v["t_m"][c].astype(BF16), p_b[c]], axis=0), p_b[c]) for c in rng]
            v["t_m"] = [v["t_m"][c] + out[c][:two] for c in rng]
            v["p_m"] = [o[two:] for o in out]

        def last_doubling():
            v["t_m"] = [v["t_m"][c] + _dot(v["t_m"][c].astype(BF16), v["p_m"][c].astype(BF16)) for c in rng]

        def solve():
            v["x_b"] = [_dot(v["t_m"][c].astype(BF16),
                             jnp.concatenate([v["a_s"][c], v["a2v"][c]], axis=1).astype(BF16)).astype(BF16)
                        for c in rng]

        def handover():
            for c in rng:
                v_sb = v["v_sb"][c]
                rhs = jnp.concatenate([v["x_b"][c], jnp.concatenate([jnp.zeros_like(v_sb), v_sb], axis=1)],
                                      axis=0)
                out = _dot(jnp.concatenate([v["a_34"][c], v["bk_t"][c]], axis=0), rhs)
                dst = chunks[c]
                q_s[dst] = unstack(v["r_s"][c] + out[:two, :LANES]).astype(BF16)
                yl_s[dst] = unstack(out[:two, LANES:])
                gm_s[dst] = out[two:, :LANES].astype(BF16)
                nm_s[dst] = out[two:, LANES:].T
                et_s[dst] = jnp.broadcast_to(v["e_tot"][c], (8, LANES))

        n_rounds = CHUNK.bit_length() - 2
        return [front, masks, square] + [doubling] * (n_rounds - 1) + [last_doubling, solve, handover]

    ep = {}

    def epilogue_mean():
        s_ref[...] = carry["state"]
        y = y_ref[...]
        ep["d"] = y - _split_dot(y, hmean_ref[...])

    def epilogue_var():
        ep["var"] = _split_dot(ep["d"] * ep["d"], hmean_ref[...])

    def epilogue_store():
        yn = ep["d"] * lax.rsqrt(ep["var"] + GN_EPS) * lnw_ref[...] + lnb_ref[...]
        z_ref[0] = ((yn + bg_s[0]) * bg_s[1]).astype(z_ref.dtype)
        rk = r_ref[0].astype(F32) * k_ref[0].astype(F32) * rk_ref[...]
        bg_s[0] = _split_dot(rk, hsum_ref[...]) * v_ref[0].astype(F32)
        bg_s[1] = g_ref[0].astype(F32)

    size = -(-n_chunks // n_groups)
    groups = [group_stages(list(range(lo, min(lo + size, n_chunks)))) for lo in range(0, n_chunks, size)]
    n_stages = len(groups[0])
    slots = n_stages - 1 + GROUP_LAG * (len(groups) - 1)
    per_slot = -(-n_chunks // max(slots - 3, 1))
    for t in range(slots):
        tail_steps(per_slot)
        if t == slots - 3:
            tail_steps(n_chunks)
            epilogue_mean()
        if t == slots - 2:
            epilogue_var()
        for gi, stages in enumerate(groups):
            k_stage = t - GROUP_LAG * gi
            if 0 <= k_stage < n_stages - 1:
                stages[k_stage]()
    epilogue_store()
    for stages in groups:
        stages[-1]()


def _chunk_cumsum(x):
    t = lax.broadcasted_iota(jnp.int32, x.shape, 0) % CHUNK
    sh = 1
    while sh < CHUNK:
        x = x + jnp.where(t >= sh, pltpu.roll(x, sh, axis=0), 0.0)
        sh *= 2
    return x


def _scan_layer(r, lw, k, v, an, bb, g, r_k, lnx_w, lnx_b, *, side=(), n_chunks=16, n_groups=2):
    B, S, D = r.shape
    n_chunks = min(n_chunks, S // CHUNK)
    L = n_chunks * CHUNK
    nb, nj = S // L, D // LANES
    n_blocks = B * nj * nb

    def coords(f):
        return f // (nj * nb), f % nb, (f // nb) % nj

    cur = lambda s: coords(jnp.minimum(s, n_blocks - 1))
    prev = lambda s: coords(jnp.maximum(s - 1, 0))
    blk_in = pl.BlockSpec((1, L, LANES), cur)
    vec_in = pl.BlockSpec((1, LANES), lambda s: (0, cur(s)[2]))
    vec_out = pl.BlockSpec((1, LANES), lambda s: (0, prev(s)[2]))
    const = lambda n: pl.BlockSpec((n, n), lambda s: (0, 0))
    two = 2 * CHUNK
    side_in, side_out, side_shapes = _side_cast_specs(side, n_blocks + 1, lambda s: s)
    return pl.pallas_call(
        functools.partial(_scan_kernel, n_chunks=n_chunks, n_groups=n_groups, blocks_per_seq=nb),
        out_shape=[jax.ShapeDtypeStruct((B, S, D), BF16)] + side_shapes,
        grid=(n_blocks + 1,),
        in_specs=[blk_in] * 7 + [vec_in, vec_out, vec_out] + [const(LANES), const(LANES)] + side_in,
        out_specs=[pl.BlockSpec((1, L, LANES), prev)] + side_out,
        scratch_shapes=[
            pltpu.VMEM((LANES, LANES), F32),
            pltpu.VMEM((L, LANES), F32),
            pltpu.VMEM((n_chunks, CHUNK, LANES), BF16),
            pltpu.VMEM((n_chunks, CHUNK, LANES), F32),
            pltpu.VMEM((n_chunks, two, LANES), BF16),
            pltpu.VMEM((n_chunks, two, LANES), F32),
            pltpu.VMEM((n_chunks, 8, LANES), F32),
            pltpu.VMEM((2, L, LANES), F32),
        ],
        compiler_params=_params(("arbitrary",)),
        name="rwkv_scan",
    )(r, lw, k, v, an, bb, g, r_k.reshape(1, D), lnx_w.reshape(1, D), lnx_b.reshape(1, D),
      _head_sum_matrix(LANES, 1.0 / HEAD), _head_sum_matrix(LANES), *[w for w, _ in side])


def _oproj_kernel(z_ref, w_ref, h_ref, o_ref):
    o_ref[...] = h_ref[...] + _dot(z_ref[...], w_ref[...])


def _oproj_layer(z, w, h, *, tm=1024, tn=1024):
    T, D = h.shape
    tm, tn = min(tm, T), min(tn, D)
    return pl.pallas_call(
        _oproj_kernel,
        out_shape=jax.ShapeDtypeStruct((T, D), F32),
        grid=(T // tm, D // tn),
        in_specs=[
            pl.BlockSpec((tm, D), lambda i, n: (i, 0)),
            pl.BlockSpec((D, tn), lambda i, n: (0, n)),
            pl.BlockSpec((tm, tn), lambda i, n: (i, n)),
        ],
        out_specs=pl.BlockSpec((tm, tn), lambda i, n: (i, n)),
        compiler_params=_params(("parallel", "parallel")),
        name="rwkv_oproj",
    )(z, w, h)


def _cast_kernel(w_ref, o_ref):
    o_ref[...] = w_ref[0].astype(o_ref.dtype)


def _cast_weight(w, layer, *, block_bytes=4 * 1024 * 1024):
    _, R, C = w.shape
    tr = R
    while tr * C * 4 > block_bytes and tr % 16 == 0:
        tr //= 2
    return pl.pallas_call(
        _cast_kernel,
        out_shape=jax.ShapeDtypeStruct((R, C), BF16),
        grid=(R // tr,),
        in_specs=[pl.BlockSpec((1, tr, C), lambda i: (layer, i, 0))],
        out_specs=pl.BlockSpec((tr, C), lambda i: (i, 0)),
        compiler_params=_params(("parallel",)),
        name="cast_bf16",
    )(w)


def kernel(x, norm1_g, norm2_g, final_g, pool_w, pool_b, pool_scale, rw_mu, rw_r, rw_k, rw_v, rw_o, rw_w0, rw_w_la, rw_w_lb, rw_a0, rw_a_la, rw_a_lb, rw_g_la, rw_g_lb, rw_k_k, rw_k_a, rw_r_k, rw_lnx_w, rw_lnx_b, ffn_w1, ffn_w3, ffn_w2):
    B, S, D = x.shape
    T = B * S
    bf = lambda w: w.astype(BF16)
    cast = _cast_weight

    h = _pool_layer(x, norm1_g[0], bf(pool_w[0]), pool_b[0].reshape(-1), pool_scale[0])
    h, w_r, w_k, w_v, w_o = _ffn_layer(
        h.reshape(T, D), norm2_g[0], cast(ffn_w1, 0), cast(ffn_w3, 0), cast(ffn_w2, 0), final_g,
        final_norm=False, side=((rw_r, 0), (rw_k, 0), (rw_v, 0), (rw_o, 0)))

    r, lw, k, v, an, bb, g = _proj_layer(
        h, S, norm1_g[1], rw_mu[0], w_r, w_k, w_v, rw_w0[0],
        bf(rw_w_la[0]), bf(rw_w_lb[0]), rw_a0[0], bf(rw_a_la[0]), bf(rw_a_lb[0]),
        bf(rw_g_la[0]), bf(rw_g_lb[0]), rw_k_k[0], rw_k_a[0])
    s3 = lambda t: t.reshape(B, S, D)
    z, w1, w3, w2 = _scan_layer(s3(r), s3(lw), s3(k), s3(v), s3(an), s3(bb), s3(g),
                                rw_r_k[0].reshape(-1), rw_lnx_w[0], rw_lnx_b[0],
                                side=((ffn_w1, 1), (ffn_w3, 1), (ffn_w2, 1)))
    h = _oproj_layer(z.reshape(T, D), w_o, h)
    h, = _ffn_layer(h, norm2_g[1], w1, w3, w2, final_g, final_norm=True)
    return h.reshape(B, S, D)
```

```python
import functools

import jax
import jax.numpy as jnp
from jax import lax
from jax.experimental import pallas as pl
from jax.experimental.pallas import tpu as pltpu

F32 = jnp.float32
BF16 = jnp.bfloat16

RMS_EPS = 1e-6
GN_EPS = 64e-5
L2_EPS = 1e-12
DECAY_SCALE = 0.6065306597126334
POOL_WINDOWS = (2, 4, 8, 16)
POOL_HALO = 16
HEAD = 64
LANES = 128
MXU_WIDTH = 256
CHUNK = 64
GROUP_LAG = 3
VMEM_LIMIT = 56 * 1024 * 1024


def _rms(x, g):
    return x * lax.rsqrt(jnp.mean(x * x, axis=-1, keepdims=True) + RMS_EPS) * g


def _dot(a, b):
    return jnp.dot(a, b, preferred_element_type=F32)


def _dot_nt(a, b):
    return lax.dot_general(a, b, (((1,), (1,)), ((), ())), preferred_element_type=F32)


def _split_dot(x, w, pieces=2):
    acc = None
    rem = x
    for _ in range(pieces):
        p = rem.astype(BF16)
        rem = rem - p.astype(F32)
        t = _dot(p, w)
        acc = t if acc is None else acc + t
    return acc


def _params(sem):
    return pltpu.CompilerParams(dimension_semantics=sem, vmem_limit_bytes=VMEM_LIMIT)


def _pool_kernel(x_ref, xprev_ref, g_ref, w_ref, b_ref, sc_ref, o_ref, *, ts, gdim):
    i = pl.program_id(1)
    g = g_ref[...]
    xc = x_ref[0]
    hn = _rms(xc, g)
    hp = _rms(xprev_ref[0], g)
    hp = jnp.where(i == 0, 0.0, hp)
    ext = jnp.concatenate([hp, hn], axis=0)
    rows = ts + POOL_HALO
    tau = lax.broadcasted_iota(jnp.int32, (ts, gdim), 0) + i * ts
    for gi, win in enumerate(POOL_WINDOWS):
        lo = gi * gdim
        e = ext[:, lo:lo + gdim]
        acc = e
        step = 1
        while step < win:
            acc = acc + pltpu.roll(acc, step, axis=0)
            step *= 2
        wsum = acc[POOL_HALO:rows]
        cnt = jnp.minimum(tau + 1, win).astype(F32)
        pooled = wsum / cnt - hn[:, lo:lo + gdim]
        mixed = _dot(pooled.astype(BF16), w_ref[gi]) + b_ref[:, lo:lo + gdim]
        o_ref[0, :, lo:lo + gdim] = xc[:, lo:lo + gdim] + mixed * sc_ref[:, lo:lo + gdim]


def _pool_layer(x, g, w_bf, b, scale, *, ts=512):
    B, S, D = x.shape
    G, C, _ = w_bf.shape
    ts = min(ts, S)
    halo_blocks = ts // POOL_HALO
    return pl.pallas_call(
        functools.partial(_pool_kernel, ts=ts, gdim=C),
        out_shape=jax.ShapeDtypeStruct((B, S, D), F32),
        grid=(B, S // ts),
        in_specs=[
            pl.BlockSpec((1, ts, D), lambda b, i: (b, i, 0)),
            pl.BlockSpec((1, POOL_HALO, D), lambda b, i: (b, jnp.maximum(i * halo_blocks - 1, 0), 0)),
            pl.BlockSpec((1, D), lambda b, i: (0, 0)),
            pl.BlockSpec((G, C, C), lambda b, i: (0, 0, 0)),
            pl.BlockSpec((1, D), lambda b, i: (0, 0)),
            pl.BlockSpec((1, D), lambda b, i: (0, 0)),
        ],
        out_specs=pl.BlockSpec((1, ts, D), lambda b, i: (b, i, 0)),
        compiler_params=_params(("parallel", "parallel")),
        name="pool_mixer",
    )(x, x, g.reshape(1, D), w_bf, b.reshape(1, D), scale.reshape(1, D))


def _ffn_kernel(h_ref, g_ref, w1_ref, w3_ref, w2_ref, fg_ref, *rest, n_i, n_f, final_norm, tn):
    n_side = (len(rest) - 4) // 2
    side_in, o_hbm, side_out = rest[:n_side], rest[n_side], rest[n_side + 1:2 * n_side + 1]
    hn_ref, acc_ref, sem = rest[2 * n_side + 1:]
    _side_casts(side_in, side_out)
    i = pl.program_id(0)
    f = pl.program_id(1)
    tm, d = acc_ref.shape
    slabs = [slice(n, n + tn) for n in range(0, d, tn)]

    def out_copy(k, tile):
        return pltpu.make_async_copy(acc_ref.at[:, slabs[k]], o_hbm.at[pl.ds(tile * tm, tm), slabs[k]],
                                     sem.at[k])

    def wait_out(tile):
        for k in range(len(slabs)):
            out_copy(k, tile).wait()

    @pl.when(f == 0)
    def _():
        hn_ref[...] = _rms(h_ref[...], g_ref[...]).astype(BF16)

        @pl.when(i > 0)
        def _():
            wait_out(i - 1)
        acc_ref[...] = h_ref[...]

    def body(is_last):
        hn = hn_ref[...]
        a = _dot(hn, w1_ref[...])
        b = _dot(hn, w3_ref[...])
        act = (a * jax.nn.sigmoid(a) * b).astype(BF16)
        for k, cs in enumerate(slabs):
            acc_ref[:, cs] += _dot(act, w2_ref[:, cs])
            if is_last and not final_norm:
                out_copy(k, i).start()
        if is_last and final_norm:
            x = acc_ref[...]
            scale = lax.rsqrt(jnp.mean(x * x, axis=-1, keepdims=True) + RMS_EPS)
            for k, cs in enumerate(slabs):
                acc_ref[:, cs] = x[:, cs] * scale * fg_ref[:, cs]
                out_copy(k, i).start()

    @pl.when(f < n_f - 1)
    def _():
        body(False)

    @pl.when(f == n_f - 1)
    def _():
        body(True)

        @pl.when(i == n_i - 1)
        def _():
            wait_out(i)


def _ffn_layer(h, g, w1, w3, w2, final_g, *, final_norm, side=(), tm=1024, tf=512, tn=512):
    T, D = h.shape
    F = w1.shape[1]
    tm, tf, tn = min(tm, T), min(tf, F), min(tn, D)
    while F % tf:
        tf -= LANES
    assert T % tm == 0 and D % tn == 0 and tf > 0
    n_f = F // tf
    n_i = T // tm
    side_in, side_out, side_shapes = _side_cast_specs(side, n_i * n_f, lambda i, f: i * n_f + f)
    outs = pl.pallas_call(
        functools.partial(_ffn_kernel, n_i=n_i, n_f=n_f, final_norm=final_norm, tn=tn),
        out_shape=[jax.ShapeDtypeStruct((T, D), F32)] + side_shapes,
        grid=(n_i, n_f),
        in_specs=[
            pl.BlockSpec((tm, D), lambda i, f: (i, 0)),
            pl.BlockSpec((1, D), lambda i, f: (0, 0)),
            pl.BlockSpec((D, tf), lambda i, f: (0, f)),
            pl.BlockSpec((D, tf), lambda i, f: (0, f)),
            pl.BlockSpec((tf, D), lambda i, f: (f, 0)),
            pl.BlockSpec((1, D), lambda i, f: (0, 0)),
        ] + side_in,
        out_specs=[pl.BlockSpec(memory_space=pl.ANY)] + side_out,
        scratch_shapes=[pltpu.VMEM((tm, D), BF16), pltpu.VMEM((tm, D), F32),
                        pltpu.SemaphoreType.DMA((D // tn,))],
        compiler_params=_params(("arbitrary", "arbitrary")),
        name="ffn_final" if final_norm else "ffn",
    )(h, g.reshape(1, D), w1, w3, w2, final_g.reshape(1, D), *[w for w, _ in side])
    return outs


def _side_casts(side_in, side_out):
    for w_ref, o_ref in zip(side_in, side_out):
        o_ref[...] = w_ref[0].astype(o_ref.dtype)


def _side_cast_specs(side, n_steps, flat_step):
    in_specs, out_specs, shapes = [], [], []
    for w, layer in side:
        _, R, C = w.shape
        tr = next(t for t in range(16, R + 1, 16) if R % t == 0 and R // t <= n_steps)
        last = R // tr - 1

        def blk(*idx, last=last):
            return jnp.minimum(flat_step(*idx), last)

        in_specs.append(pl.BlockSpec((1, tr, C), lambda *idx, blk=blk, layer=layer: (layer, blk(*idx), 0)))
        out_specs.append(pl.BlockSpec((tr, C), lambda *idx, blk=blk: (blk(*idx), 0)))
        shapes.append(jax.ShapeDtypeStruct((R, C), BF16))
    return in_specs, out_specs, shapes


def _proj_kernel(h_ref, hprev_ref, g_ref, mu_ref, wla_ref, ala_ref, gla_ref,
                 wr_ref, wk_ref, wv_ref, wlb_ref, alb_ref, glb_ref,
                 w0_ref, a0_ref, kk_ref, ka_ref, hsum_ref,
                 r_out, lw_out, k_out, v_out, an_out, bb_out, g_out,
                 xr_s, xk_s, xv_s, tw_s, ta_s, tg_s, *, tm, seq, n_tiles, n_steps):
    i = pl.program_id(0)
    n = pl.program_id(1)
    rq = tm // n_steps
    slab = hsum_ref.shape[0]
    slabs = [slice(lo, lo + slab) for lo in range(0, r_out.shape[1], slab)]

    def prep():
        slot = i % 2
        r0 = pl.multiple_of(n * rq, rq)
        dst = pl.ds(r0, rq)
        g = g_ref[...]
        hn = _rms(h_ref[dst, :], g)
        inside = h_ref[pl.ds(pl.multiple_of(jnp.maximum(r0 - 8, 0), 8), 8), :]
        prev8 = jnp.where(n == 0, hprev_ref[...], inside)
        hp = _rms(prev8[7:8, :], g)
        hp = jnp.where((n == 0) & ((i * tm) % seq == 0), 0.0, hp)
        row = lax.broadcasted_iota(jnp.int32, hn.shape, 0)
        shifted = jnp.where(row == 0, hp, pltpu.roll(hn, 1, axis=0))
        xx = shifted - hn
        xr_s[slot, dst, :] = (hn + xx * mu_ref[0:1, :]).astype(BF16)
        xk_s[slot, dst, :] = (hn + xx * mu_ref[2:3, :]).astype(BF16)
        xv_s[slot, dst, :] = (hn + xx * mu_ref[3:4, :]).astype(BF16)
        xw = (hn + xx * mu_ref[1:2, :]).astype(BF16)
        tw_s[slot, dst, :] = jnp.tanh(_dot(xw, wla_ref[...])).astype(BF16)
        xa = (hn + xx * mu_ref[4:5, :]).astype(BF16)
        ta_s[slot, dst, :] = _dot(xa, ala_ref[...]).astype(BF16)
        xg = (hn + xx * mu_ref[5:6, :]).astype(BF16)
        tg_s[slot, dst, :] = jax.nn.sigmoid(_dot(xg, gla_ref[...])).astype(BF16)

    def project(between):
        slot = (i + 1) % 2
        half = tm
        units = [(pl.ds(r0, half), cs) for r0 in range(0, tm, half) for cs in slabs]

        def matmuls(rows, cs):
            r = _dot(xr_s[slot, rows, :], wr_ref[:, cs])
            k = _dot(xk_s[slot, rows, :], wk_ref[:, cs])
            v = _dot(xv_s[slot, rows, :], wv_ref[:, cs])
            wl = _dot(tw_s[slot, rows, :], wlb_ref[:, cs])
            al = _dot(ta_s[slot, rows, :], alb_ref[:, cs])
            gg = _dot(tg_s[slot, rows, :], glb_ref[:, cs])
            return r, k, v, wl, al, gg

        def tail(rows, cs, r, k, v, wl, al, gg):
            lw_out[rows, cs] = -DECAY_SCALE * jax.nn.sigmoid(wl + w0_ref[:, cs])
            a_sig = jax.nn.sigmoid(al + a0_ref[:, cs])
            kk = k * kk_ref[:, cs]
            ss = _split_dot(kk * kk, hsum_ref[...])
            kk = kk * jnp.minimum(lax.rsqrt(ss), 1.0 / L2_EPS)
            r_out[rows, cs] = r.astype(r_out.dtype)
            k_out[rows, cs] = (k * (1.0 + (a_sig - 1.0) * ka_ref[:, cs])).astype(k_out.dtype)
            v_out[rows, cs] = v.astype(v_out.dtype)
            an_out[rows, cs] = (-kk).astype(an_out.dtype)
            bb_out[rows, cs] = (kk * a_sig).astype(bb_out.dtype)
            g_out[rows, cs] = gg.astype(g_out.dtype)

        pending = None
        for u, (rows, cs) in enumerate(units):
            res = matmuls(rows, cs)
            if pending is not None:
                tail(*pending)
            pending = (rows, cs) + res
            if u == 0:
                between()
        tail(*pending)

    @pl.when(i == 0)
    def _():
        prep()

    @pl.when((i > 0) & (i < n_tiles))
    def _():
        project(prep)

    @pl.when(i == n_tiles)
    def _():
        project(lambda: None)


def _head_sum_matrix(n, value=1.0):
    hi = lax.broadcasted_iota(jnp.int32, (n, n), 0) // HEAD
    hj = lax.broadcasted_iota(jnp.int32, (n, n), 1) // HEAD
    return jnp.where(hi == hj, value, 0.0).astype(BF16)


def _proj_layer(h, seq, g, mu, w_r, w_k, w_v, w0, w_la, w_lb, a0, a_la, a_lb, g_la, g_lb, k_k, k_a,
                *, tm=512, tn=512):
    T, D = h.shape
    tm, tn = min(tm, T), min(tn, D)
    slab = min(MXU_WIDTH, tn)
    n_tiles, n_steps = T // tm, D // tn
    dl, al, gl = w_la.shape[1], a_la.shape[1], g_la.shape[1]
    row = lambda x: x.reshape(1, D)
    full = lambda shp: pl.BlockSpec(shp, lambda i, n: (0, 0))
    col = lambda rows: pl.BlockSpec((rows, tn), lambda i, n: (0, n))
    out_spec = pl.BlockSpec((tm, tn), lambda i, n: (jnp.maximum(i - 1, 0), jnp.where(i == 0, 0, n)))
    prev_blocks = tm // 8
    tile = lambda i: jnp.minimum(i, n_tiles - 1)
    out_dtypes = [BF16, F32, BF16, BF16, BF16, BF16, BF16]
    outs = pl.pallas_call(
        functools.partial(_proj_kernel, tm=tm, seq=seq, n_tiles=n_tiles, n_steps=n_steps),
        out_shape=[jax.ShapeDtypeStruct((T, D), dt) for dt in out_dtypes],
        grid=(n_tiles + 1, n_steps),
        in_specs=[
            pl.BlockSpec((tm, D), lambda i, n: (tile(i), 0)),
            pl.BlockSpec((8, D), lambda i, n: (jnp.maximum(tile(i) * prev_blocks - 1, 0), 0)),
            full((1, D)), full((8, D)), full((D, dl)), full((D, al)), full((D, gl)),
            col(D), col(D), col(D), col(dl), col(al), col(gl),
            col(1), col(1), col(1), col(1),
            full((slab, slab)),
        ],
        out_specs=[out_spec] * 7,
        scratch_shapes=[pltpu.VMEM((2, tm, D), BF16)] * 3
        + [pltpu.VMEM((2, tm, dl), BF16), pltpu.VMEM((2, tm, al), BF16), pltpu.VMEM((2, tm, gl), BF16)],
        compiler_params=_params(("arbitrary", "arbitrary")),
        name="rwkv_proj",
    )(h, h, row(g), jnp.pad(mu, ((0, 8 - mu.shape[0]), (0, 0))), w_la, a_la, g_la,
      w_r, w_k, w_v, w_lb, a_lb, g_lb, row(w0), row(a0), row(k_k), row(k_a),
      _head_sum_matrix(slab))
    return outs


def _scan_kernel(r_ref, lw_ref, k_ref, v_ref, an_ref, bb_ref, g_ref, rk_ref, lnw_ref, lnb_ref,
                 hmean_ref, hsum_ref, *rest, n_chunks, n_groups, blocks_per_seq):
    n_side = (len(rest) - 9) // 2
    side_in, z_ref, side_out = rest[:n_side], rest[n_side], rest[n_side + 1:2 * n_side + 1]
    s_ref, y_ref, q_s, yl_s, gm_s, nm_s, et_s, bg_s = rest[2 * n_side + 1:]
    _side_casts(side_in, side_out)
    step = pl.program_id(0)

    @pl.when(step == 0)
    def _():
        for ref in (s_ref, q_s, yl_s, gm_s, nm_s, et_s, bg_s):
            ref[...] = jnp.zeros_like(ref)

    two = 2 * CHUNK
    lane_head0 = lax.broadcasted_iota(jnp.int32, (CHUNK, LANES), 1) < HEAD
    ri = lax.broadcasted_iota(jnp.int32, (two, two), 0)
    ci = lax.broadcasted_iota(jnp.int32, (two, two), 1)
    same = (ri >= CHUNK) == (ci >= CHUNK)
    strict = same & (ci < ri)
    incl = same & (ci <= ri)
    eye = jnp.where(ri == ci, 1.0, 0.0).astype(F32)

    def stack(x):
        return jnp.concatenate([jnp.where(lane_head0, x, 0.0), jnp.where(lane_head0, 0.0, x)], axis=0)

    def unstack(x):
        return x[:CHUNK] + x[CHUNK:]

    sls = [pl.ds(c * CHUNK, CHUNK) for c in range(n_chunks)]

    prev_first = (step + blocks_per_seq - 1) % blocks_per_seq == 0
    carry = {"state": jnp.where(prev_first, 0.0, s_ref[...]), "next": 0}

    def tail_steps(count):
        for c in range(carry["next"], min(carry["next"] + count, n_chunks)):
            state = carry["state"]
            sb = state.astype(BF16)
            y_ref[sls[c], :] = _dot_nt(q_s[c], sb) + yl_s[c]
            carry["state"] = state * et_s[c, 0:1, :] + _dot_nt(sb, gm_s[c]) + nm_s[c]
            carry["next"] = c + 1

    def group_stages(chunks):
        n = len(chunks)
        rng = range(n)
        v = {}

        def front():
            rows = pl.ds(chunks[0] * CHUNK, n * CHUNK)
            lw_all = lw_ref[0, rows, :]
            cum_all = _chunk_cumsum(lw_all)
            e_in_all = jnp.exp(cum_all)
            e_ex_all = jnp.exp(cum_all - lw_all)
            e_neg_all = jnp.exp(-cum_all)
            v["a_s"], v["r_s"], v["v_sb"], v["bk_t"], v["e_tot"], v["gram"] = [], [], [], [], [], []
            for c in rng:
                sl = slice(c * CHUNK, (c + 1) * CHUNK)
                src = sls[chunks[c]]
                cum = cum_all[sl]
                e_rem = jnp.exp(cum[CHUNK - 1:CHUNK, :] - cum)
                k = k_ref[0, src, :]
                bb = bb_ref[0, src, :]
                a_s = stack(an_ref[0, src, :] * e_ex_all[sl])
                r_s = stack(r_ref[0, src, :] * e_in_all[sl])
                v["a_s"].append(a_s)
                v["r_s"].append(r_s)
                v["v_sb"].append(stack(v_ref[0, src, :]).astype(BF16))
                v["bk_t"].append(
                    jnp.concatenate([stack(bb * e_rem).T, stack(k * e_rem).T], axis=1).astype(BF16))
                v["e_tot"].append(e_in_all[sl][CHUNK - 1:CHUNK, :])
                b_t = (bb * e_neg_all[sl]).astype(BF16)
                k_t = (k * e_neg_all[sl]).astype(BF16)
                lhs = jnp.concatenate([a_s, r_s], axis=0).astype(BF16)
                rhs = jnp.concatenate([b_t, b_t, k_t, k_t], axis=0)
                v["gram"].append(_dot_nt(lhs, rhs))

        def masks():
            gram = v.pop("gram")
            v["p_m"] = [jnp.where(strict, gm[:two, :two], 0.0) for gm in gram]
            v["a_34"] = [jnp.concatenate([jnp.where(incl, gm[two:, :two], 0.0),
                                          jnp.where(incl, gm[two:, two:], 0.0)], axis=1).astype(BF16)
                         for gm in gram]
            v["a2v"] = [_dot(jnp.where(strict, gram[c][:two, two:], 0.0).astype(BF16), v["v_sb"][c])
                        for c in rng]
            v["t_m"] = [eye + p for p in v["p_m"]]

        def square():
            p_b = [p.astype(BF16) for p in v["p_m"]]
            v["p_m"] = [_dot(pb, pb) for pb in p_b]

        def doubling():
            p_b = [p.astype(BF16) for p in v["p_m"]]
            out = [_dot(jnp.concatenate([v["t_m"][c].astype(BF16), p_b[c]], axis=0), p_b[c]) for c in rng]
            v["t_m"] = [v["t_m"][c] + out[c][:two] for c in rng]
            v["p_m"] = [o[two:] for o in out]

        def last_doubling():
            v["t_m"] = [v["t_m"][c] + _dot(v["t_m"][c].astype(BF16), v["p_m"][c].astype(BF16)) for c in rng]

        def solve():
            v["x_b"] = [_dot(v["t_m"][c].astype(BF16),
                             jnp.concatenate([v["a_s"][c], v["a2v"][c]], axis=1).astype(BF16)).astype(BF16)
                        for c in rng]

        def handover():
            for c in rng:
                v_sb = v["v_sb"][c]
                rhs = jnp.concatenate([v["x_b"][c], jnp.concatenate([jnp.zeros_like(v_sb), v_sb], axis=1)],
                                      axis=0)
                out = _dot(jnp.concatenate([v["a_34"][c], v["bk_t"][c]], axis=0), rhs)
                dst = chunks[c]
                q_s[dst] = unstack(v["r_s"][c] + out[:two, :LANES]).astype(BF16)
                yl_s[dst] = unstack(out[:two, LANES:])
                gm_s[dst] = out[two:, :LANES].astype(BF16)
                nm_s[dst] = out[two:, LANES:].T
                et_s[dst] = jnp.broadcast_to(v["e_tot"][c], (8, LANES))

        n_rounds = CHUNK.bit_length() - 2
        return [front, masks, square] + [doubling] * (n_rounds - 1) + [last_doubling, solve, handover]

    ep = {}

    def epilogue_mean():
        s_ref[...] = carry["state"]
        y = y_ref[...]
        ep["d"] = y - _split_dot(y, hmean_ref[...])

    def epilogue_var():
        ep["var"] = _split_dot(ep["d"] * ep["d"], hmean_ref[...])

    def epilogue_store():
        yn = ep["d"] * lax.rsqrt(ep["var"] + GN_EPS) * lnw_ref[...] + lnb_ref[...]
        z_ref[0] = ((yn + bg_s[0]) * bg_s[1]).astype(z_ref.dtype)
        rk = r_ref[0].astype(F32) * k_ref[0].astype(F32) * rk_ref[...]
        bg_s[0] = _split_dot(rk, hsum_ref[...]) * v_ref[0].astype(F32)
        bg_s[1] = g_ref[0].astype(F32)

    size = -(-n_chunks // n_groups)
    groups = [group_stages(list(range(lo, min(lo + size, n_chunks)))) for lo in range(0, n_chunks, size)]
    n_stages = len(groups[0])
    slots = n_stages - 1 + GROUP_LAG * (len(groups) - 1)
    per_slot = -(-n_chunks // max(slots - 3, 1))
    for t in range(slots):
        tail_steps(per_slot)
        if t == slots - 3:
            tail_steps(n_chunks)
            epilogue_mean()
        if t == slots - 2:
            epilogue_var()
        for gi, stages in enumerate(groups):
            k_stage = t - GROUP_LAG * gi
            if 0 <= k_stage < n_stages - 1:
                stages[k_stage]()
    epilogue_store()
    for stages in groups:
        stages[-1]()


def _chunk_cumsum(x):
    t = lax.broadcasted_iota(jnp.int32, x.shape, 0) % CHUNK
    sh = 1
    while sh < CHUNK:
        x = x + jnp.where(t >= sh, pltpu.roll(x, sh, axis=0), 0.0)
        sh *= 2
    return x


def _scan_layer(r, lw, k, v, an, bb, g, r_k, lnx_w, lnx_b, *, side=(), n_chunks=16, n_groups=2):
    B, S, D = r.shape
    n_chunks = min(n_chunks, S // CHUNK)
    L = n_chunks * CHUNK
    nb, nj = S // L, D // LANES
    n_blocks = B * nj * nb

    def coords(f):
        return f // (nj * nb), f % nb, (f // nb) % nj

    cur = lambda s: coords(jnp.minimum(s, n_blocks - 1))
    prev = lambda s: coords(jnp.maximum(s - 1, 0))
    blk_in = pl.BlockSpec((1, L, LANES), cur)
    vec_in = pl.BlockSpec((1, LANES), lambda s: (0, cur(s)[2]))
    vec_out = pl.BlockSpec((1, LANES), lambda s: (0, prev(s)[2]))
    const = lambda n: pl.BlockSpec((n, n), lambda s: (0, 0))
    two = 2 * CHUNK
    side_in, side_out, side_shapes = _side_cast_specs(side, n_blocks + 1, lambda s: s)
    return pl.pallas_call(
        functools.partial(_scan_kernel, n_chunks=n_chunks, n_groups=n_groups, blocks_per_seq=nb),
        out_shape=[jax.ShapeDtypeStruct((B, S, D), BF16)] + side_shapes,
        grid=(n_blocks + 1,),
        in_specs=[blk_in] * 7 + [vec_in, vec_out, vec_out] + [const(LANES), const(LANES)] + side_in,
        out_specs=[pl.BlockSpec((1, L, LANES), prev)] + side_out,
        scratch_shapes=[
            pltpu.VMEM((LANES, LANES), F32),
            pltpu.VMEM((L, LANES), F32),
            pltpu.VMEM((n_chunks, CHUNK, LANES), BF16),
            pltpu.VMEM((n_chunks, CHUNK, LANES), F32),
            pltpu.VMEM((n_chunks, two, LANES), BF16),
            pltpu.VMEM((n_chunks, two, LANES), F32),
            pltpu.VMEM((n_chunks, 8, LANES), F32),
            pltpu.VMEM((2, L, LANES), F32),
        ],
        compiler_params=_params(("arbitrary",)),
        name="rwkv_scan",
    )(r, lw, k, v, an, bb, g, r_k.reshape(1, D), lnx_w.reshape(1, D), lnx_b.reshape(1, D),
      _head_sum_matrix(LANES, 1.0 / HEAD), _head_sum_matrix(LANES), *[w for w, _ in side])


def _oproj_kernel(z_ref, w_ref, h_ref, o_ref):
    o_ref[...] = h_ref[...] + _dot(z_ref[...], w_ref[...])


def _oproj_layer(z, w, h, *, tm=1024, tn=1024):
    T, D = h.shape
    tm, tn = min(tm, T), min(tn, D)
    return pl.pallas_call(
        _oproj_kernel,
        out_shape=jax.ShapeDtypeStruct((T, D), F32),
        grid=(T // tm, D // tn),
        in_specs=[
            pl.BlockSpec((tm, D), lambda i, n: (i, 0)),
            pl.BlockSpec((D, tn), lambda i, n: (0, n)),
            pl.BlockSpec((tm, tn), lambda i, n: (i, n)),
        ],
        out_specs=pl.BlockSpec((tm, tn), lambda i, n: (i, n)),
        compiler_params=_params(("parallel", "parallel")),
        name="rwkv_oproj",
    )(z, w, h)


def _cast_kernel(w_ref, o_ref):
    o_ref[...] = w_ref[0].astype(o_ref.dtype)


def _cast_weight(w, layer, *, block_bytes=4 * 1024 * 1024):
    _, R, C = w.shape
    tr = R
    while tr * C * 4 > block_bytes and tr % 16 == 0:
        tr //= 2
    return pl.pallas_call(
        _cast_kernel,
        out_shape=jax.ShapeDtypeStruct((R, C), BF16),
        grid=(R // tr,),
        in_specs=[pl.BlockSpec((1, tr, C), lambda i: (layer, i, 0))],
        out_specs=pl.BlockSpec((tr, C), lambda i: (i, 0)),
        compiler_params=_params(("parallel",)),
        name="cast_bf16",
    )(w)


def kernel(x, norm1_g, norm2_g, final_g, pool_w, pool_b, pool_scale, rw_mu, rw_r, rw_k, rw_v, rw_o, rw_w0, rw_w_la, rw_w_lb, rw_a0, rw_a_la, rw_a_lb, rw_g_la, rw_g_lb, rw_k_k, rw_k_a, rw_r_k, rw_lnx_w, rw_lnx_b, ffn_w1, ffn_w3, ffn_w2):
    B, S, D = x.shape
    T = B * S
    bf = lambda w: w.astype(BF16)
    cast = _cast_weight

    h = _pool_layer(x, norm1_g[0], bf(pool_w[0]), pool_b[0].reshape(-1), pool_scale[0])
    h, w_r, w_k, w_v, w_o = _ffn_layer(
        h.reshape(T, D), norm2_g[0], cast(ffn_w1, 0), cast(ffn_w3, 0), cast(ffn_w2, 0), final_g,
        final_norm=False, side=((rw_r, 0), (rw_k, 0), (rw_v, 0), (rw_o, 0)))

    r, lw, k, v, an, bb, g = _proj_layer(
        h, S, norm1_g[1], rw_mu[0], w_r, w_k, w_v, rw_w0[0],
        bf(rw_w_la[0]), bf(rw_w_lb[0]), rw_a0[0], bf(rw_a_la[0]), bf(rw_a_lb[0]),
        bf(rw_g_la[0]), bf(rw_g_lb[0]), rw_k_k[0], rw_k_a[0])
    s3 = lambda t: t.reshape(B, S, D)
    z, w1, w3, w2 = _scan_layer(s3(r), s3(lw), s3(k), s3(v), s3(an), s3(bb), s3(g),
                                rw_r_k[0].reshape(-1), rw_lnx_w[0], rw_lnx_b[0],
                                side=((ffn_w1, 1), (ffn_w3, 1), (ffn_w2, 1)))
    h = _oproj_layer(z.reshape(T, D), w_o, h)
    h, = _ffn_layer(h, norm2_g[1], w1, w3, w2, final_g, final_norm=True)
    return h.reshape(B, S, D)
```

```python
import functools

import jax
import jax.numpy as jnp
from jax import lax
from jax.experimental import pallas as pl
from jax.experimental.pallas import tpu as pltpu

F32 = jnp.float32
BF16 = jnp.bfloat16

RMS_EPS = 1e-6
GN_EPS = 64e-5
L2_EPS = 1e-12
DECAY_SCALE = 0.6065306597126334
POOL_WINDOWS = (2, 4, 8, 16)
POOL_HALO = 16
HEAD = 64
LANES = 128
MXU_WIDTH = 256
CHUNK = 64
GROUP_LAG = 3
VMEM_LIMIT = 56 * 1024 * 1024


def _rms(x, g):
    return x * lax.rsqrt(jnp.mean(x * x, axis=-1, keepdims=True) + RMS_EPS) * g


def _dot(a, b):
    return jnp.dot(a, b, preferred_element_type=F32)


def _dot_nt(a, b):
    return lax.dot_general(a, b, (((1,), (1,)), ((), ())), preferred_element_type=F32)


def _split_dot(x, w, pieces=2):
    acc = None
    rem = x
    for _ in range(pieces):
        p = rem.astype(BF16)
        rem = rem - p.astype(F32)
        t = _dot(p, w)
        acc = t if acc is None else acc + t
    return acc


def _params(sem):
    return pltpu.CompilerParams(dimension_semantics=sem, vmem_limit_bytes=VMEM_LIMIT)


def _pool_kernel(x_ref, xprev_ref, g_ref, w_ref, b_ref, sc_ref, o_ref, *, ts, gdim):
    i = pl.program_id(1)
    g = g_ref[...]
    xc = x_ref[0]
    hn = _rms(xc, g)
    hp = _rms(xprev_ref[0], g)
    hp = jnp.where(i == 0, 0.0, hp)
    ext = jnp.concatenate([hp, hn], axis=0)
    rows = ts + POOL_HALO
    tau = lax.broadcasted_iota(jnp.int32, (ts, gdim), 0) + i * ts
    for gi, win in enumerate(POOL_WINDOWS):
        lo = gi * gdim
        e = ext[:, lo:lo + gdim]
        acc = e
        step = 1
        while step < win:
            acc = acc + pltpu.roll(acc, step, axis=0)
            step *= 2
        wsum = acc[POOL_HALO:rows]
        cnt = jnp.minimum(tau + 1, win).astype(F32)
        pooled = wsum / cnt - hn[:, lo:lo + gdim]
        mixed = _dot(pooled.astype(BF16), w_ref[gi]) + b_ref[:, lo:lo + gdim]
        o_ref[0, :, lo:lo + gdim] = xc[:, lo:lo + gdim] + mixed * sc_ref[:, lo:lo + gdim]


def _pool_layer(x, g, w_bf, b, scale, *, ts=512):
    B, S, D = x.shape
    G, C, _ = w_bf.shape
    ts = min(ts, S)
    halo_blocks = ts // POOL_HALO
    return pl.pallas_call(
        functools.partial(_pool_kernel, ts=ts, gdim=C),
        out_shape=jax.ShapeDtypeStruct((B, S, D), F32),
        grid=(B, S // ts),
        in_specs=[
            pl.BlockSpec((1, ts, D), lambda b, i: (b, i, 0)),
            pl.BlockSpec((1, POOL_HALO, D), lambda b, i: (b, jnp.maximum(i * halo_blocks - 1, 0), 0)),
            pl.BlockSpec((1, D), lambda b, i: (0, 0)),
            pl.BlockSpec((G, C, C), lambda b, i: (0, 0, 0)),
            pl.BlockSpec((1, D), lambda b, i: (0, 0)),
            pl.BlockSpec((1, D), lambda b, i: (0, 0)),
        ],
        out_specs=pl.BlockSpec((1, ts, D), lambda b, i: (b, i, 0)),
        compiler_params=_params(("parallel", "parallel")),
        name="pool_mixer",
    )(x, x, g.reshape(1, D), w_bf, b.reshape(1, D), scale.reshape(1, D))


def _ffn_kernel(h_ref, g_ref, w1_ref, w3_ref, w2_ref, fg_ref, *rest, n_i, n_f, final_norm, tn):
    n_side = (len(rest) - 4) // 2
    side_in, o_hbm, side_out = rest[:n_side], rest[n_side], rest[n_side + 1:2 * n_side + 1]
    hn_ref, acc_ref, sem = rest[2 * n_side + 1:]
    _side_casts(side_in, side_out)
    i = pl.program_id(0)
    f = pl.program_id(1)
    tm, d = acc_ref.shape
    slabs = [slice(n, n + tn) for n in range(0, d, tn)]

    def out_copy(k, tile):
        return pltpu.make_async_copy(acc_ref.at[:, slabs[k]], o_hbm.at[pl.ds(tile * tm, tm), slabs[k]],
                                     sem.at[k])

    def wait_out(tile):
        for k in range(len(slabs)):
            out_copy(k, tile).wait()

    @pl.when(f == 0)
    def _():
        hn_ref[...] = _rms(h_ref[...], g_ref[...]).astype(BF16)

        @pl.when(i > 0)
        def _():
            wait_out(i - 1)
        acc_ref[...] = h_ref[...]

    def body(is_last):
        hn = hn_ref[...]
        a = _dot(hn, w1_ref[...])
        b = _dot(hn, w3_ref[...])
        act = (a * jax.nn.sigmoid(a) * b).astype(BF16)
        for k, cs in enumerate(slabs):
            acc_ref[:, cs] += _dot(act, w2_ref[:, cs])
            if is_last and not final_norm:
                out_copy(k, i).start()
        if is_last and final_norm:
            x = acc_ref[...]
            scale = lax.rsqrt(jnp.mean(x * x, axis=-1, keepdims=True) + RMS_EPS)
            for k, cs in enumerate(slabs):
                acc_ref[:, cs] = x[:, cs] * scale * fg_ref[:, cs]
                out_copy(k, i).start()

    @pl.when(f < n_f - 1)
    def _():
        body(False)

    @pl.when(f == n_f - 1)
    def _():
        body(True)

        @pl.when(i == n_i - 1)
        def _():
            wait_out(i)


def _ffn_layer(h, g, w1, w3, w2, final_g, *, final_norm, side=(), tm=1024, tf=512, tn=512):
    T, D = h.shape
    F = w1.shape[1]
    tm, tf, tn = min(tm, T), min(tf, F), min(tn, D)
    while F % tf:
        tf -= LANES
    assert T % tm == 0 and D % tn == 0 and tf > 0
    n_f = F // tf
    n_i = T // tm
    side_in, side_out, side_shapes = _side_cast_specs(side, n_i * n_f, lambda i, f: i * n_f + f)
    outs = pl.pallas_call(
        functools.partial(_ffn_kernel, n_i=n_i, n_f=n_f, final_norm=final_norm, tn=tn),
        out_shape=[jax.ShapeDtypeStruct((T, D), F32)] + side_shapes,
        grid=(n_i, n_f),
        in_specs=[
            pl.BlockSpec((tm, D), lambda i, f: (i, 0)),
            pl.BlockSpec((1, D), lambda i, f: (0, 0)),
            pl.BlockSpec((D, tf), lambda i, f: (0, f)),
            pl.BlockSpec((D, tf), lambda i, f: (0, f)),
            pl.BlockSpec((tf, D), lambda i, f: (f, 0)),
            pl.BlockSpec((1, D), lambda i, f: (0, 0)),
        ] + side_in,
        out_specs=[pl.BlockSpec(memory_space=pl.ANY)] + side_out,
        scratch_shapes=[pltpu.VMEM((tm, D), BF16), pltpu.VMEM((tm, D), F32),
                        pltpu.SemaphoreType.DMA((D // tn,))],
        compiler_params=_params(("arbitrary", "arbitrary")),
        name="ffn_final" if final_norm else "ffn",
    )(h, g.reshape(1, D), w1, w3, w2, final_g.reshape(1, D), *[w for w, _ in side])
    return outs


def _side_casts(side_in, side_out):
    for w_ref, o_ref in zip(side_in, side_out):
        o_ref[...] = w_ref[0].astype(o_ref.dtype)


def _side_cast_specs(side, n_steps, flat_step):
    in_specs, out_specs, shapes = [], [], []
    for w, layer in side:
        _, R, C = w.shape
        tr = next(t for t in range(16, R + 1, 16) if R % t == 0 and R // t <= n_steps)
        last = R // tr - 1

        def blk(*idx, last=last):
            return jnp.minimum(flat_step(*idx), last)

        in_specs.append(pl.BlockSpec((1, tr, C), lambda *idx, blk=blk, layer=layer: (layer, blk(*idx), 0)))
        out_specs.append(pl.BlockSpec((tr, C), lambda *idx, blk=blk: (blk(*idx), 0)))
        shapes.append(jax.ShapeDtypeStruct((R, C), BF16))
    return in_specs, out_specs, shapes


def _proj_kernel(h_ref, hprev_ref, g_ref, mu_ref, wla_ref, ala_ref, gla_ref,
                 wr_ref, wk_ref, wv_ref, wlb_ref, alb_ref, glb_ref,
                 w0_ref, a0_ref, kk_ref, ka_ref, hsum_ref,
                 r_out, lw_out, k_out, v_out, an_out, bb_out, g_out,
                 xr_s, xk_s, xv_s, tw_s, ta_s, tg_s, *, tm, seq, n_tiles, n_steps):
    i = pl.program_id(0)
    n = pl.program_id(1)
    rq = tm // n_steps
    slab = hsum_ref.shape[0]
    slabs = [slice(lo, lo + slab) for lo in range(0, r_out.shape[1], slab)]

    def prep():
        slot = i % 2
        r0 = pl.multiple_of(n * rq, rq)
        dst = pl.ds(r0, rq)
        g = g_ref[...]
        hn = _rms(h_ref[dst, :], g)
        inside = h_ref[pl.ds(pl.multiple_of(jnp.maximum(r0 - 8, 0), 8), 8), :]
        prev8 = jnp.where(n == 0, hprev_ref[...], inside)
        hp = _rms(prev8[7:8, :], g)
        hp = jnp.where((n == 0) & ((i * tm) % seq == 0), 0.0, hp)
        row = lax.broadcasted_iota(jnp.int32, hn.shape, 0)
        shifted = jnp.where(row == 0, hp, pltpu.roll(hn, 1, axis=0))
        xx = shifted - hn
        xr_s[slot, dst, :] = (hn + xx * mu_ref[0:1, :]).astype(BF16)
        xk_s[slot, dst, :] = (hn + xx * mu_ref[2:3, :]).astype(BF16)
        xv_s[slot, dst, :] = (hn + xx * mu_ref[3:4, :]).astype(BF16)
        xw = (hn + xx * mu_ref[1:2, :]).astype(BF16)
        tw_s[slot, dst, :] = jnp.tanh(_dot(xw, wla_ref[...])).astype(BF16)
        xa = (hn + xx * mu_ref[4:5, :]).astype(BF16)
        ta_s[slot, dst, :] = _dot(xa, ala_ref[...]).astype(BF16)
        xg = (hn + xx * mu_ref[5:6, :]).astype(BF16)
        tg_s[slot, dst, :] = jax.nn.sigmoid(_dot(xg, gla_ref[...])).astype(BF16)

    def project(between):
        slot = (i + 1) % 2
        half = tm
        units = [(pl.ds(r0, half), cs) for r0 in range(0, tm, half) for cs in slabs]

        def matmuls(rows, cs):
            r = _dot(xr_s[slot, rows, :], wr_ref[:, cs])
            k = _dot(xk_s[slot, rows, :], wk_ref[:, cs])
            v = _dot(xv_s[slot, rows, :], wv_ref[:, cs])
            wl = _dot(tw_s[slot, rows, :], wlb_ref[:, cs])
            al = _dot(ta_s[slot, rows, :], alb_ref[:, cs])
            gg = _dot(tg_s[slot, rows, :], glb_ref[:, cs])
            return r, k, v, wl, al, gg

        def tail(rows, cs, r, k, v, wl, al, gg):
            lw_out[rows, cs] = -DECAY_SCALE * jax.nn.sigmoid(wl + w0_ref[:, cs])
            a_sig = jax.nn.sigmoid(al + a0_ref[:, cs])
            kk = k * kk_ref[:, cs]
            ss = _split_dot(kk * kk, hsum_ref[...])
            kk = kk * jnp.minimum(lax.rsqrt(ss), 1.0 / L2_EPS)
            r_out[rows, cs] = r.astype(r_out.dtype)
            k_out[rows, cs] = (k * (1.0 + (a_sig - 1.0) * ka_ref[:, cs])).astype(k_out.dtype)
            v_out[rows, cs] = v.astype(v_out.dtype)
            an_out[rows, cs] = (-kk).astype(an_out.dtype)
            bb_out[rows, cs] = (kk * a_sig).astype(bb_out.dtype)
            g_out[rows, cs] = gg.astype(g_out.dtype)

        pending = None
        for u, (rows, cs) in enumerate(units):
            res = matmuls(rows, cs)
            if pending is not None:
                tail(*pending)
            pending = (rows, cs) + res
            if u == 0:
                between()
        tail(*pending)

    @pl.when(i == 0)
    def _():
        prep()

    @pl.when((i > 0) & (i < n_tiles))
    def _():
        project(prep)

    @pl.when(i == n_tiles)
    def _():
        project(lambda: None)


def _head_sum_matrix(n, value=1.0):
    hi = lax.broadcasted_iota(jnp.int32, (n, n), 0) // HEAD
    hj = lax.broadcasted_iota(jnp.int32, (n, n), 1) // HEAD
    return jnp.where(hi == hj, value, 0.0).astype(BF16)


def _proj_layer(h, seq, g, mu, w_r, w_k, w_v, w0, w_la, w_lb, a0, a_la, a_lb, g_la, g_lb, k_k, k_a,
                *, tm=512, tn=512):
    T, D = h.shape
    tm, tn = min(tm, T), min(tn, D)
    slab = min(MXU_WIDTH, tn)
    n_tiles, n_steps = T // tm, D // tn
    dl, al, gl = w_la.shape[1], a_la.shape[1], g_la.shape[1]
    row = lambda x: x.reshape(1, D)
    full = lambda shp: pl.BlockSpec(shp, lambda i, n: (0, 0))
    col = lambda rows: pl.BlockSpec((rows, tn), lambda i, n: (0, n))
    out_spec = pl.BlockSpec((tm, tn), lambda i, n: (jnp.maximum(i - 1, 0), jnp.where(i == 0, 0, n)))
    prev_blocks = tm // 8
    tile = lambda i: jnp.minimum(i, n_tiles - 1)
    out_dtypes = [BF16, F32, BF16, BF16, BF16, BF16, BF16]
    outs = pl.pallas_call(
        functools.partial(_proj_kernel, tm=tm, seq=seq, n_tiles=n_tiles, n_steps=n_steps),
        out_shape=[jax.ShapeDtypeStruct((T, D), dt) for dt in out_dtypes],
        grid=(n_tiles + 1, n_steps),
        in_specs=[
            pl.BlockSpec((tm, D), lambda i, n: (tile(i), 0)),
            pl.BlockSpec((8, D), lambda i, n: (jnp.maximum(tile(i) * prev_blocks - 1, 0), 0)),
            full((1, D)), full((8, D)), full((D, dl)), full((D, al)), full((D, gl)),
            col(D), col(D), col(D), col(dl), col(al), col(gl),
            col(1), col(1), col(1), col(1),
            full((slab, slab)),
        ],
        out_specs=[out_spec] * 7,
        scratch_shapes=[pltpu.VMEM((2, tm, D), BF16)] * 3
        + [pltpu.VMEM((2, tm, dl), BF16), pltpu.VMEM((2, tm, al), BF16), pltpu.VMEM((2, tm, gl), BF16)],
        compiler_params=_params(("arbitrary", "arbitrary")),
        name="rwkv_proj",
    )(h, h, row(g), jnp.pad(mu, ((0, 8 - mu.shape[0]), (0, 0))), w_la, a_la, g_la,
      w_r, w_k, w_v, w_lb, a_lb, g_lb, row(w0), row(a0), row(k_k), row(k_a),
      _head_sum_matrix(slab))
    return outs


def _scan_kernel(r_ref, lw_ref, k_ref, v_ref, an_ref, bb_ref, g_ref, rk_ref, lnw_ref, lnb_ref,
                 hmean_ref, hsum_ref, *rest, n_chunks, n_groups, blocks_per_seq):
    n_side = (len(rest) - 7) // 2
    side_in, z_ref, side_out = rest[:n_side], rest[n_side], rest[n_side + 1:2 * n_side + 1]
    s_ref, y_ref, qg_s, yl_s, nm_s, bg_s = rest[2 * n_side + 1:]
    _side_casts(side_in, side_out)
    step = pl.program_id(0)

    @pl.when(step == 0)
    def _():
        for ref in (s_ref, qg_s, yl_s, nm_s, bg_s):
            ref[...] = jnp.zeros_like(ref)

    two = 2 * CHUNK
    lane_head0 = lax.broadcasted_iota(jnp.int32, (CHUNK, LANES), 1) < HEAD
    ri = lax.broadcasted_iota(jnp.int32, (two, two), 0)
    ci = lax.broadcasted_iota(jnp.int32, (two, two), 1)
    same = (ri >= CHUNK) == (ci >= CHUNK)
    strict = same & (ci < ri)
    incl = same & (ci <= ri)
    eye = jnp.where(ri == ci, 1.0, 0.0).astype(F32)

    def stack(x):
        return jnp.concatenate([jnp.where(lane_head0, x, 0.0), jnp.where(lane_head0, 0.0, x)], axis=0)

    def unstack(x):
        return x[:CHUNK] + x[CHUNK:]

    sls = [pl.ds(c * CHUNK, CHUNK) for c in range(n_chunks)]

    prev_first = (step + blocks_per_seq - 1) % blocks_per_seq == 0
    carry = {"state": jnp.where(prev_first, 0.0, s_ref[...]), "next": 0}

    def tail_steps(count):
        for c in range(carry["next"], min(carry["next"] + count, n_chunks)):
            res = _dot(qg_s[c], carry["state"].astype(BF16))
            y_ref[sls[c], :] = res[:CHUNK] + yl_s[c]
            carry["state"] = res[CHUNK:CHUNK + two] + res[CHUNK + two:] + nm_s[c]
            carry["next"] = c + 1

    def group_stages(chunks):
        n = len(chunks)
        rng = range(n)
        v = {}

        def front():
            rows = pl.ds(chunks[0] * CHUNK, n * CHUNK)
            lw_all = lw_ref[0, rows, :]
            cum_all = _chunk_cumsum(lw_all)
            e_in_all = jnp.exp(cum_all)
            e_ex_all = jnp.exp(cum_all - lw_all)
            e_neg_all = jnp.exp(-cum_all)
            v["a_s"], v["r_s"], v["v_sb"], v["bk_t"], v["e_tot"], v["gram"] = [], [], [], [], [], []
            for c in rng:
                sl = slice(c * CHUNK, (c + 1) * CHUNK)
                src = sls[chunks[c]]
                cum = cum_all[sl]
                e_rem = jnp.exp(cum[CHUNK - 1:CHUNK, :] - cum)
                k = k_ref[0, src, :]
                bb = bb_ref[0, src, :]
                a_s = stack(an_ref[0, src, :] * e_ex_all[sl])
                r_s = stack(r_ref[0, src, :] * e_in_all[sl])
                v["a_s"].append(a_s)
                v["r_s"].append(r_s)
                v["v_sb"].append(stack(v_ref[0, src, :]).astype(BF16))
                v["bk_t"].append(
                    jnp.concatenate([stack(bb * e_rem).T, stack(k * e_rem).T], axis=1).astype(BF16))
                v["e_tot"].append(e_in_all[sl][CHUNK - 1:CHUNK, :])
                b_t = (bb * e_neg_all[sl]).astype(BF16)
                k_t = (k * e_neg_all[sl]).astype(BF16)
                lhs = jnp.concatenate([a_s, r_s], axis=0).astype(BF16)
                rhs = jnp.concatenate([b_t, b_t, k_t, k_t], axis=0)
                v["gram"].append(_dot_nt(lhs, rhs))

        def masks():
            gram = v.pop("gram")
            v["p_m"] = [jnp.where(strict, gm[:two, :two], 0.0) for gm in gram]
            v["a_34"] = [jnp.concatenate([jnp.where(incl, gm[two:, :two], 0.0),
                                          jnp.where(incl, gm[two:, two:], 0.0)], axis=1).astype(BF16)
                         for gm in gram]
            v["a2v"] = [_dot(jnp.where(strict, gram[c][:two, two:], 0.0).astype(BF16), v["v_sb"][c])
                        for c in rng]
            v["t_m"] = [eye + p for p in v["p_m"]]

        def square():
            p_b = [p.astype(BF16) for p in v["p_m"]]
            v["p_m"] = [_dot(pb, pb) for pb in p_b]

        def doubling():
            p_b = [p.astype(BF16) for p in v["p_m"]]
            out = [_dot(jnp.concatenate([v["t_m"][c].astype(BF16), p_b[c]], axis=0), p_b[c]) for c in rng]
            v["t_m"] = [v["t_m"][c] + out[c][:two] for c in rng]
            v["p_m"] = [o[two:] for o in out]

        def last_doubling():
            v["t_m"] = [v["t_m"][c] + _dot(v["t_m"][c].astype(BF16), v["p_m"][c].astype(BF16)) for c in rng]

        def solve():
            v["x_b"] = [_dot(v["t_m"][c].astype(BF16),
                             jnp.concatenate([v["a_s"][c], v["a2v"][c]], axis=1).astype(BF16)).astype(BF16)
                        for c in rng]

        def handover():
            for c in rng:
                v_sb = v["v_sb"][c]
                rhs = jnp.concatenate([v["x_b"][c], jnp.concatenate([jnp.zeros_like(v_sb), v_sb], axis=1)],
                                      axis=0)
                out = _dot(jnp.concatenate([v["a_34"][c], v["bk_t"][c]], axis=0), rhs)
                dst = chunks[c]
                q_m = unstack(v["r_s"][c] + out[:two, :LANES]).astype(BF16)
                g_t = out[two:, :LANES] + eye * v["e_tot"][c]
                g_hi = g_t.astype(BF16)
                g_lo = (g_t - g_hi.astype(F32)).astype(BF16)
                qg_s[dst] = jnp.concatenate([q_m, g_hi, g_lo], axis=0)
                yl_s[dst] = unstack(out[:two, LANES:])
                nm_s[dst] = out[two:, LANES:]

        n_rounds = CHUNK.bit_length() - 2
        return [front, masks, square] + [doubling] * (n_rounds - 1) + [last_doubling, solve, handover]

    ep = {}

    def epilogue_mean():
        s_ref[...] = carry["state"]
        y = y_ref[...]
        ep["d"] = y - _split_dot(y, hmean_ref[...])

    def epilogue_var():
        ep["var"] = _split_dot(ep["d"] * ep["d"], hmean_ref[...])

    def epilogue_store():
        yn = ep["d"] * lax.rsqrt(ep["var"] + GN_EPS) * lnw_ref[...] + lnb_ref[...]
        z_ref[0] = ((yn + bg_s[0]) * bg_s[1]).astype(z_ref.dtype)
        rk = r_ref[0].astype(F32) * k_ref[0].astype(F32) * rk_ref[...]
        bg_s[0] = _split_dot(rk, hsum_ref[...]) * v_ref[0].astype(F32)
        bg_s[1] = g_ref[0].astype(F32)

    size = -(-n_chunks // n_groups)
    groups = [group_stages(list(range(lo, min(lo + size, n_chunks)))) for lo in range(0, n_chunks, size)]
    n_stages = len(groups[0])
    slots = n_stages - 1 + GROUP_LAG * (len(groups) - 1)
    per_slot = -(-n_chunks // max(slots - 3, 1))
    for t in range(slots):
        tail_steps(per_slot)
        if t == slots - 3:
            tail_steps(n_chunks)
            epilogue_mean()
        if t == slots - 2:
            epilogue_var()
        for gi, stages in enumerate(groups):
            k_stage = t - GROUP_LAG * gi
            if 0 <= k_stage < n_stages - 1:
                stages[k_stage]()
    epilogue_store()
    for stages in groups:
        stages[-1]()


def _chunk_cumsum(x):
    t = lax.broadcasted_iota(jnp.int32, x.shape, 0) % CHUNK
    sh = 1
    while sh < CHUNK:
        x = x + jnp.where(t >= sh, pltpu.roll(x, sh, axis=0), 0.0)
        sh *= 2
    return x


def _scan_layer(r, lw, k, v, an, bb, g, r_k, lnx_w, lnx_b, *, side=(), n_chunks=16, n_groups=2):
    B, S, D = r.shape
    n_chunks = min(n_chunks, S // CHUNK)
    L = n_chunks * CHUNK
    nb, nj = S // L, D // LANES
    n_blocks = B * nj * nb

    def coords(f):
        return f // (nj * nb), f % nb, (f // nb) % nj

    cur = lambda s: coords(jnp.minimum(s, n_blocks - 1))
    prev = lambda s: coords(jnp.maximum(s - 1, 0))
    blk_in = pl.BlockSpec((1, L, LANES), cur)
    vec_in = pl.BlockSpec((1, LANES), lambda s: (0, cur(s)[2]))
    vec_out = pl.BlockSpec((1, LANES), lambda s: (0, prev(s)[2]))
    const = lambda n: pl.BlockSpec((n, n), lambda s: (0, 0))
    two = 2 * CHUNK
    side_in, side_out, side_shapes = _side_cast_specs(side, n_blocks + 1, lambda s: s)
    return pl.pallas_call(
        functools.partial(_scan_kernel, n_chunks=n_chunks, n_groups=n_groups, blocks_per_seq=nb),
        out_shape=[jax.ShapeDtypeStruct((B, S, D), BF16)] + side_shapes,
        grid=(n_blocks + 1,),
        in_specs=[blk_in] * 7 + [vec_in, vec_out, vec_out] + [const(LANES), const(LANES)] + side_in,
        out_specs=[pl.BlockSpec((1, L, LANES), prev)] + side_out,
        scratch_shapes=[
            pltpu.VMEM((LANES, LANES), F32),
            pltpu.VMEM((L, LANES), F32),
            pltpu.VMEM((n_chunks, CHUNK + 2 * two, LANES), BF16),
            pltpu.VMEM((n_chunks, CHUNK, LANES), F32),
            pltpu.VMEM((n_chunks, two, LANES), F32),
            pltpu.VMEM((2, L, LANES), F32),
        ],
        compiler_params=_params(("arbitrary",)),
        name="rwkv_scan",
    )(r, lw, k, v, an, bb, g, r_k.reshape(1, D), lnx_w.reshape(1, D), lnx_b.reshape(1, D),
      _head_sum_matrix(LANES, 1.0 / HEAD), _head_sum_matrix(LANES), *[w for w, _ in side])


def _oproj_kernel(z_ref, w_ref, h_ref, o_ref):
    o_ref[...] = h_ref[...] + _dot(z_ref[...], w_ref[...])


def _oproj_layer(z, w, h, *, tm=1024, tn=1024):
    T, D = h.shape
    tm, tn = min(tm, T), min(tn, D)
    return pl.pallas_call(
        _oproj_kernel,
        out_shape=jax.ShapeDtypeStruct((T, D), F32),
        grid=(T // tm, D // tn),
        in_specs=[
            pl.BlockSpec((tm, D), lambda i, n: (i, 0)),
            pl.BlockSpec((D, tn), lambda i, n: (0, n)),
            pl.BlockSpec((tm, tn), lambda i, n: (i, n)),
        ],
        out_specs=pl.BlockSpec((tm, tn), lambda i, n: (i, n)),
        compiler_params=_params(("parallel", "parallel")),
        name="rwkv_oproj",
    )(z, w, h)


def _cast_kernel(w_ref, o_ref):
    o_ref[...] = w_ref[0].astype(o_ref.dtype)


def _cast_weight(w, layer, *, block_bytes=4 * 1024 * 1024):
    _, R, C = w.shape
    tr = R
    while tr * C * 4 > block_bytes and tr % 16 == 0:
        tr //= 2
    return pl.pallas_call(
        _cast_kernel,
        out_shape=jax.ShapeDtypeStruct((R, C), BF16),
        grid=(R // tr,),
        in_specs=[pl.BlockSpec((1, tr, C), lambda i: (layer, i, 0))],
        out_specs=pl.BlockSpec((tr, C), lambda i: (i, 0)),
        compiler_params=_params(("parallel",)),
        name="cast_bf16",
    )(w)


def kernel(x, norm1_g, norm2_g, final_g, pool_w, pool_b, pool_scale, rw_mu, rw_r, rw_k, rw_v, rw_o, rw_w0, rw_w_la, rw_w_lb, rw_a0, rw_a_la, rw_a_lb, rw_g_la, rw_g_lb, rw_k_k, rw_k_a, rw_r_k, rw_lnx_w, rw_lnx_b, ffn_w1, ffn_w3, ffn_w2):
    B, S, D = x.shape
    T = B * S
    bf = lambda w: w.astype(BF16)
    cast = _cast_weight

    h = _pool_layer(x, norm1_g[0], bf(pool_w[0]), pool_b[0].reshape(-1), pool_scale[0])
    h, w_r, w_k, w_v, w_o = _ffn_layer(
        h.reshape(T, D), norm2_g[0], cast(ffn_w1, 0), cast(ffn_w3, 0), cast(ffn_w2, 0), final_g,
        final_norm=False, side=((rw_r, 0), (rw_k, 0), (rw_v, 0), (rw_o, 0)))

    r, lw, k, v, an, bb, g = _proj_layer(
        h, S, norm1_g[1], rw_mu[0], w_r, w_k, w_v, rw_w0[0],
        bf(rw_w_la[0]), bf(rw_w_lb[0]), rw_a0[0], bf(rw_a_la[0]), bf(rw_a_lb[0]),
        bf(rw_g_la[0]), bf(rw_g_lb[0]), rw_k_k[0], rw_k_a[0])
    s3 = lambda t: t.reshape(B, S, D)
    z, w1, w3, w2 = _scan_layer(s3(r), s3(lw), s3(k), s3(v), s3(an), s3(bb), s3(g),
                                rw_r_k[0].reshape(-1), rw_lnx_w[0], rw_lnx_b[0],
                                side=((ffn_w1, 1), (ffn_w3, 1), (ffn_w2, 1)))
    h = _oproj_layer(z.reshape(T, D), w_o, h)
    h, = _ffn_layer(h, norm2_g[1], w1, w3, w2, final_g, final_norm=True)
    return h.reshape(B, S, D)
```

```python
import functools

import jax
import jax.numpy as jnp
from jax import lax
from jax.experimental import pallas as pl
from jax.experimental.pallas import tpu as pltpu

F32 = jnp.float32
BF16 = jnp.bfloat16

RMS_EPS = 1e-6
GN_EPS = 64e-5
L2_EPS = 1e-12
DECAY_SCALE = 0.6065306597126334
POOL_WINDOWS = (2, 4, 8, 16)
POOL_HALO = 16
HEAD = 64
LANES = 128
MXU_WIDTH = 256
CHUNK = 64
GROUP_LAG = 3
VMEM_LIMIT = 56 * 1024 * 1024


def _rms(x, g):
    return x * lax.rsqrt(jnp.mean(x * x, axis=-1, keepdims=True) + RMS_EPS) * g


def _dot(a, b):
    return jnp.dot(a, b, preferred_element_type=F32)


def _dot_nt(a, b):
    return lax.dot_general(a, b, (((1,), (1,)), ((), ())), preferred_element_type=F32)


def _split_dot(x, w, pieces=2):
    acc = None
    rem = x
    for _ in range(pieces):
        p = rem.astype(BF16)
        rem = rem - p.astype(F32)
        t = _dot(p, w)
        acc = t if acc is None else acc + t
    return acc


def _params(sem):
    return pltpu.CompilerParams(dimension_semantics=sem, vmem_limit_bytes=VMEM_LIMIT)


def _pool_kernel(x_ref, xprev_ref, g_ref, w_ref, b_ref, sc_ref, *rest, ts, gdim):
    n_side = (len(rest) - 1) // 2
    side_in, o_ref, side_out = rest[:n_side], rest[n_side], rest[n_side + 1:]
    _side_casts(side_in, side_out)
    i = pl.program_id(1)
    g = g_ref[...]
    xc = x_ref[0]
    hn = _rms(xc, g)
    hp = _rms(xprev_ref[0], g)
    hp = jnp.where(i == 0, 0.0, hp)
    ext = jnp.concatenate([hp, hn], axis=0)
    rows = ts + POOL_HALO
    tau = lax.broadcasted_iota(jnp.int32, (ts, gdim), 0) + i * ts
    for gi, win in enumerate(POOL_WINDOWS):
        lo = gi * gdim
        e = ext[:, lo:lo + gdim]
        acc = e
        step = 1
        while step < win:
            acc = acc + pltpu.roll(acc, step, axis=0)
            step *= 2
        wsum = acc[POOL_HALO:rows]
        cnt = jnp.minimum(tau + 1, win).astype(F32)
        pooled = wsum / cnt - hn[:, lo:lo + gdim]
        mixed = _dot(pooled.astype(BF16), w_ref[gi]) + b_ref[:, lo:lo + gdim]
        o_ref[0, :, lo:lo + gdim] = xc[:, lo:lo + gdim] + mixed * sc_ref[:, lo:lo + gdim]


def _pool_layer(x, g, w_bf, b, scale, *, side=(), ts=512):
    B, S, D = x.shape
    G, C, _ = w_bf.shape
    ts = min(ts, S)
    halo_blocks = ts // POOL_HALO
    n_s = S // ts
    side_in, side_out, side_shapes = _side_cast_specs(side, B * n_s, lambda b, i: b * n_s + i)
    return pl.pallas_call(
        functools.partial(_pool_kernel, ts=ts, gdim=C),
        out_shape=[jax.ShapeDtypeStruct((B, S, D), F32)] + side_shapes,
        grid=(B, n_s),
        in_specs=[
            pl.BlockSpec((1, ts, D), lambda b, i: (b, i, 0)),
            pl.BlockSpec((1, POOL_HALO, D), lambda b, i: (b, jnp.maximum(i * halo_blocks - 1, 0), 0)),
            pl.BlockSpec((1, D), lambda b, i: (0, 0)),
            pl.BlockSpec((G, C, C), lambda b, i: (0, 0, 0)),
            pl.BlockSpec((1, D), lambda b, i: (0, 0)),
            pl.BlockSpec((1, D), lambda b, i: (0, 0)),
        ] + side_in,
        out_specs=[pl.BlockSpec((1, ts, D), lambda b, i: (b, i, 0))] + side_out,
        compiler_params=_params(("arbitrary", "arbitrary")),
        name="pool_mixer",
    )(x, x, g.reshape(1, D), w_bf, b.reshape(1, D), scale.reshape(1, D), *[w for w, _ in side])


def _ffn_kernel(h_ref, g_ref, w1_ref, w3_ref, w2_ref, fg_ref, *rest, n_i, n_f, final_norm, tn):
    n_side = (len(rest) - 4) // 2
    side_in, o_hbm, side_out = rest[:n_side], rest[n_side], rest[n_side + 1:2 * n_side + 1]
    hn_ref, acc_ref, sem = rest[2 * n_side + 1:]
    _side_casts(side_in, side_out)
    i = pl.program_id(0)
    f = pl.program_id(1)
    tm, d = acc_ref.shape
    slabs = [slice(n, n + tn) for n in range(0, d, tn)]

    def out_copy(k, tile):
        return pltpu.make_async_copy(acc_ref.at[:, slabs[k]], o_hbm.at[pl.ds(tile * tm, tm), slabs[k]],
                                     sem.at[k])

    def wait_out(tile):
        for k in range(len(slabs)):
            out_copy(k, tile).wait()

    @pl.when(f == 0)
    def _():
        hn_ref[...] = _rms(h_ref[...], g_ref[...]).astype(BF16)

        @pl.when(i > 0)
        def _():
            wait_out(i - 1)
        acc_ref[...] = h_ref[...]

    def body(is_last):
        hn = hn_ref[...]
        a = _dot(hn, w1_ref[...])
        b = _dot(hn, w3_ref[...])
        act = (a * jax.nn.sigmoid(a) * b).astype(BF16)
        for k, cs in enumerate(slabs):
            acc_ref[:, cs] += _dot(act, w2_ref[:, cs])
            if is_last and not final_norm:
                out_copy(k, i).start()
        if is_last and final_norm:
            x = acc_ref[...]
            scale = lax.rsqrt(jnp.mean(x * x, axis=-1, keepdims=True) + RMS_EPS)
            for k, cs in enumerate(slabs):
                acc_ref[:, cs] = x[:, cs] * scale * fg_ref[:, cs]
                out_copy(k, i).start()

    @pl.when(f < n_f - 1)
    def _():
        body(False)

    @pl.when(f == n_f - 1)
    def _():
        body(True)

        @pl.when(i == n_i - 1)
        def _():
            wait_out(i)


def _ffn_layer(h, g, w1, w3, w2, final_g, *, final_norm, side=(), tm=1024, tf=512, tn=512):
    T, D = h.shape
    F = w1.shape[1]
    tm, tf, tn = min(tm, T), min(tf, F), min(tn, D)
    while F % tf:
        tf -= LANES
    assert T % tm == 0 and D % tn == 0 and tf > 0
    n_f = F // tf
    n_i = T // tm
    side_in, side_out, side_shapes = _side_cast_specs(side, n_i * n_f, lambda i, f: i * n_f + f)
    outs = pl.pallas_call(
        functools.partial(_ffn_kernel, n_i=n_i, n_f=n_f, final_norm=final_norm, tn=tn),
        out_shape=[jax.ShapeDtypeStruct((T, D), F32)] + side_shapes,
        grid=(n_i, n_f),
        in_specs=[
            pl.BlockSpec((tm, D), lambda i, f: (i, 0)),
            pl.BlockSpec((1, D), lambda i, f: (0, 0)),
            pl.BlockSpec((D, tf), lambda i, f: (0, f)),
            pl.BlockSpec((D, tf), lambda i, f: (0, f)),
            pl.BlockSpec((tf, D), lambda i, f: (f, 0)),
            pl.BlockSpec((1, D), lambda i, f: (0, 0)),
        ] + side_in,
        out_specs=[pl.BlockSpec(memory_space=pl.ANY)] + side_out,
        scratch_shapes=[pltpu.VMEM((tm, D), BF16), pltpu.VMEM((tm, D), F32),
                        pltpu.SemaphoreType.DMA((D // tn,))],
        compiler_params=_params(("arbitrary", "arbitrary")),
        name="ffn_final" if final_norm else "ffn",
    )(h, g.reshape(1, D), w1, w3, w2, final_g.reshape(1, D), *[w for w, _ in side])
    return outs


def _side_casts(side_in, side_out):
    for w_ref, o_ref in zip(side_in, side_out):
        o_ref[...] = w_ref[0].astype(o_ref.dtype)


def _side_cast_specs(side, n_steps, flat_step):
    in_specs, out_specs, shapes = [], [], []
    for w, layer in side:
        _, R, C = w.shape
        tr = next(t for t in range(16, R + 1, 16) if R % t == 0 and R // t <= n_steps)
        last = R // tr - 1

        def blk(*idx, last=last):
            return jnp.minimum(flat_step(*idx), last)

        in_specs.append(pl.BlockSpec((1, tr, C), lambda *idx, blk=blk, layer=layer: (layer, blk(*idx), 0)))
        out_specs.append(pl.BlockSpec((tr, C), lambda *idx, blk=blk: (blk(*idx), 0)))
        shapes.append(jax.ShapeDtypeStruct((R, C), BF16))
    return in_specs, out_specs, shapes


def _proj_kernel(h_ref, hprev_ref, g_ref, mu_ref, wla_ref, ala_ref, gla_ref,
                 wr_ref, wk_ref, wv_ref, wlb_ref, alb_ref, glb_ref,
                 w0_ref, a0_ref, kk_ref, ka_ref, hsum_ref,
                 r_out, lw_out, k_out, v_out, an_out, bb_out, g_out,
                 xr_s, xk_s, xv_s, tw_s, ta_s, tg_s, *, tm, seq, n_tiles, n_steps):
    i = pl.program_id(0)
    n = pl.program_id(1)
    rq = tm // n_steps
    slab = hsum_ref.shape[0]
    slabs = [slice(lo, lo + slab) for lo in range(0, r_out.shape[1], slab)]

    def prep():
        slot = i % 2
        r0 = pl.multiple_of(n * rq, rq)
        dst = pl.ds(r0, rq)
        g = g_ref[...]
        hn = _rms(h_ref[dst, :], g)
        inside = h_ref[pl.ds(pl.multiple_of(jnp.maximum(r0 - 8, 0), 8), 8), :]
        prev8 = jnp.where(n == 0, hprev_ref[...], inside)
        hp = _rms(prev8[7:8, :], g)
        hp = jnp.where((n == 0) & ((i * tm) % seq == 0), 0.0, hp)
        row = lax.broadcasted_iota(jnp.int32, hn.shape, 0)
        shifted = jnp.where(row == 0, hp, pltpu.roll(hn, 1, axis=0))
        xx = shifted - hn
        xr_s[slot, dst, :] = (hn + xx * mu_ref[0:1, :]).astype(BF16)
        xk_s[slot, dst, :] = (hn + xx * mu_ref[2:3, :]).astype(BF16)
        xv_s[slot, dst, :] = (hn + xx * mu_ref[3:4, :]).astype(BF16)
        xw = (hn + xx * mu_ref[1:2, :]).astype(BF16)
        tw_s[slot, dst, :] = jnp.tanh(_dot(xw, wla_ref[...])).astype(BF16)
        xa = (hn + xx * mu_ref[4:5, :]).astype(BF16)
        ta_s[slot, dst, :] = _dot(xa, ala_ref[...]).astype(BF16)
        xg = (hn + xx * mu_ref[5:6, :]).astype(BF16)
        tg_s[slot, dst, :] = jax.nn.sigmoid(_dot(xg, gla_ref[...])).astype(BF16)

    def project(between):
        slot = (i + 1) % 2
        half = tm
        units = [(pl.ds(r0, half), cs) for r0 in range(0, tm, half) for cs in slabs]

        def matmuls(rows, cs):
            r = _dot(xr_s[slot, rows, :], wr_ref[:, cs])
            k = _dot(xk_s[slot, rows, :], wk_ref[:, cs])
            v = _dot(xv_s[slot, rows, :], wv_ref[:, cs])
            wl = _dot(tw_s[slot, rows, :], wlb_ref[:, cs])
            al = _dot(ta_s[slot, rows, :], alb_ref[:, cs])
            gg = _dot(tg_s[slot, rows, :], glb_ref[:, cs])
            return r, k, v, wl, al, gg

        def tail(rows, cs, r, k, v, wl, al, gg):
            lw_out[rows, cs] = -DECAY_SCALE * jax.nn.sigmoid(wl + w0_ref[:, cs])
            a_sig = jax.nn.sigmoid(al + a0_ref[:, cs])
            kk = k * kk_ref[:, cs]
            ss = _split_dot(kk * kk, hsum_ref[...])
            kk = kk * jnp.minimum(lax.rsqrt(ss), 1.0 / L2_EPS)
            r_out[rows, cs] = r.astype(r_out.dtype)
            k_out[rows, cs] = (k * (1.0 + (a_sig - 1.0) * ka_ref[:, cs])).astype(k_out.dtype)
            v_out[rows, cs] = v.astype(v_out.dtype)
            an_out[rows, cs] = (-kk).astype(an_out.dtype)
            bb_out[rows, cs] = (kk * a_sig).astype(bb_out.dtype)
            g_out[rows, cs] = gg.astype(g_out.dtype)

        pending = None
        for u, (rows, cs) in enumerate(units):
            res = matmuls(rows, cs)
            if pending is not None:
                tail(*pending)
            pending = (rows, cs) + res
            if u == 0:
                between()
        tail(*pending)

    @pl.when(i == 0)
    def _():
        prep()

    @pl.when((i > 0) & (i < n_tiles))
    def _():
        project(prep)

    @pl.when(i == n_tiles)
    def _():
        project(lambda: None)


def _head_sum_matrix(n, value=1.0):
    hi = lax.broadcasted_iota(jnp.int32, (n, n), 0) // HEAD
    hj = lax.broadcasted_iota(jnp.int32, (n, n), 1) // HEAD
    return jnp.where(hi == hj, value, 0.0).astype(BF16)


def _proj_layer(h, seq, g, mu, w_r, w_k, w_v, w0, w_la, w_lb, a0, a_la, a_lb, g_la, g_lb, k_k, k_a,
                *, tm=512, tn=512):
    T, D = h.shape
    tm, tn = min(tm, T), min(tn, D)
    slab = min(MXU_WIDTH, tn)
    n_tiles, n_steps = T // tm, D // tn
    dl, al, gl = w_la.shape[1], a_la.shape[1], g_la.shape[1]
    row = lambda x: x.reshape(1, D)
    full = lambda shp: pl.BlockSpec(shp, lambda i, n: (0, 0))
    col = lambda rows: pl.BlockSpec((rows, tn), lambda i, n: (0, n))
    out_spec = pl.BlockSpec((tm, tn), lambda i, n: (jnp.maximum(i - 1, 0), jnp.where(i == 0, 0, n)))
    prev_blocks = tm // 8
    tile = lambda i: jnp.minimum(i, n_tiles - 1)
    out_dtypes = [BF16, F32, BF16, BF16, BF16, BF16, BF16]
    outs = pl.pallas_call(
        functools.partial(_proj_kernel, tm=tm, seq=seq, n_tiles=n_tiles, n_steps=n_steps),
        out_shape=[jax.ShapeDtypeStruct((T, D), dt) for dt in out_dtypes],
        grid=(n_tiles + 1, n_steps),
        in_specs=[
            pl.BlockSpec((tm, D), lambda i, n: (tile(i), 0)),
            pl.BlockSpec((8, D), lambda i, n: (jnp.maximum(tile(i) * prev_blocks - 1, 0), 0)),
            full((1, D)), full((8, D)), full((D, dl)), full((D, al)), full((D, gl)),
            col(D), col(D), col(D), col(dl), col(al), col(gl),
            col(1), col(1), col(1), col(1),
            full((slab, slab)),
        ],
        out_specs=[out_spec] * 7,
        scratch_shapes=[pltpu.VMEM((2, tm, D), BF16)] * 3
        + [pltpu.VMEM((2, tm, dl), BF16), pltpu.VMEM((2, tm, al), BF16), pltpu.VMEM((2, tm, gl), BF16)],
        compiler_params=_params(("arbitrary", "arbitrary")),
        name="rwkv_proj",
    )(h, h, row(g), jnp.pad(mu, ((0, 8 - mu.shape[0]), (0, 0))), w_la, a_la, g_la,
      w_r, w_k, w_v, w_lb, a_lb, g_lb, row(w0), row(a0), row(k_k), row(k_a),
      _head_sum_matrix(slab))
    return outs


def _scan_kernel(r_ref, lw_ref, k_ref, v_ref, an_ref, bb_ref, g_ref, rk_ref, lnw_ref, lnb_ref,
                 hmean_ref, hsum_ref, *rest, n_chunks, n_groups, blocks_per_seq):
    n_side = (len(rest) - 7) // 2
    side_in, z_ref, side_out = rest[:n_side], rest[n_side], rest[n_side + 1:2 * n_side + 1]
    s_ref, y_ref, qg_s, yl_s, nm_s, bg_s = rest[2 * n_side + 1:]
    _side_casts(side_in, side_out)
    step = pl.program_id(0)

    @pl.when(step == 0)
    def _():
        for ref in (s_ref, qg_s, yl_s, nm_s, bg_s):
            ref[...] = jnp.zeros_like(ref)

    two = 2 * CHUNK
    lane_head0 = lax.broadcasted_iota(jnp.int32, (CHUNK, LANES), 1) < HEAD
    ri = lax.broadcasted_iota(jnp.int32, (two, two), 0)
    ci = lax.broadcasted_iota(jnp.int32, (two, two), 1)
    same = (ri >= CHUNK) == (ci >= CHUNK)
    strict = same & (ci < ri)
    incl = same & (ci <= ri)
    eye = jnp.where(ri == ci, 1.0, 0.0).astype(F32)

    def stack(x):
        return jnp.concatenate([jnp.where(lane_head0, x, 0.0), jnp.where(lane_head0, 0.0, x)], axis=0)

    def unstack(x):
        return x[:CHUNK] + x[CHUNK:]

    sls = [pl.ds(c * CHUNK, CHUNK) for c in range(n_chunks)]

    prev_first = (step + blocks_per_seq - 1) % blocks_per_seq == 0
    carry = {"state": jnp.where(prev_first, 0.0, s_ref[...]), "next": 0}

    def tail_steps(count):
        for c in range(carry["next"], min(carry["next"] + count, n_chunks)):
            res = _dot(qg_s[c], carry["state"].astype(BF16))
            y_ref[sls[c], :] = res[:CHUNK] + yl_s[c]
            carry["state"] = res[CHUNK:CHUNK + two] + res[CHUNK + two:] + nm_s[c]
            carry["next"] = c + 1

    def group_stages(chunks):
        n = len(chunks)
        rng = range(n)
        v = {}

        def front():
            rows = pl.ds(chunks[0] * CHUNK, n * CHUNK)
            lw_all = lw_ref[0, rows, :]
            cum_all = _chunk_cumsum(lw_all)
            e_in_all = jnp.exp(cum_all)
            e_ex_all = jnp.exp(cum_all - lw_all)
            e_neg_all = jnp.exp(-cum_all)
            v["a_s"], v["r_s"], v["v_sb"], v["bk_t"], v["e_tot"], v["gram"] = [], [], [], [], [], []
            for c in rng:
                sl = slice(c * CHUNK, (c + 1) * CHUNK)
                src = sls[chunks[c]]
                cum = cum_all[sl]
                e_rem = jnp.exp(cum[CHUNK - 1:CHUNK, :] - cum)
                k = k_ref[0, src, :]
                bb = bb_ref[0, src, :]
                a_s = stack(an_ref[0, src, :] * e_ex_all[sl])
                r_s = stack(r_ref[0, src, :] * e_in_all[sl])
                v["a_s"].append(a_s)
                v["r_s"].append(r_s)
                v["v_sb"].append(stack(v_ref[0, src, :]).astype(BF16))
                v["bk_t"].append(
                    jnp.concatenate([stack(bb * e_rem).T, stack(k * e_rem).T], axis=1).astype(BF16))
                v["e_tot"].append(e_in_all[sl][CHUNK - 1:CHUNK, :])
                b_t = (bb * e_neg_all[sl]).astype(BF16)
                k_t = (k * e_neg_all[sl]).astype(BF16)
                lhs = jnp.concatenate([a_s, r_s], axis=0).astype(BF16)
                rhs = jnp.concatenate([b_t, b_t, k_t, k_t], axis=0)
                v["gram"].append(_dot_nt(lhs, rhs))

        def masks():
            gram = v.pop("gram")
            v["p_m"] = [jnp.where(strict, gm[:two, :two], 0.0) for gm in gram]
            v["a_34"] = [jnp.concatenate([jnp.where(incl, gm[two:, :two], 0.0),
                                          jnp.where(incl, gm[two:, two:], 0.0)], axis=1).astype(BF16)
                         for gm in gram]
            v["a2v"] = [_dot(jnp.where(strict, gram[c][:two, two:], 0.0).astype(BF16), v["v_sb"][c])
                        for c in rng]
            v["t_m"] = [eye + p for p in v["p_m"]]

        def square():
            p_b = [p.astype(BF16) for p in v["p_m"]]
            v["p_m"] = [_dot(pb, pb) for pb in p_b]

        def doubling():
            p_b = [p.astype(BF16) for p in v["p_m"]]
            out = [_dot(jnp.concatenate([v["t_m"][c].astype(BF16), p_b[c]], axis=0), p_b[c]) for c in rng]
            v["t_m"] = [v["t_m"][c] + out[c][:two] for c in rng]
            v["p_m"] = [o[two:] for o in out]

        def last_doubling():
            v["t_m"] = [v["t_m"][c] + _dot(v["t_m"][c].astype(BF16), v["p_m"][c].astype(BF16)) for c in rng]

        def solve():
            v["x_b"] = [_dot(v["t_m"][c].astype(BF16),
                             jnp.concatenate([v["a_s"][c], v["a2v"][c]], axis=1).astype(BF16)).astype(BF16)
                        for c in rng]

        def handover():
            for c in rng:
                v_sb = v["v_sb"][c]
                rhs = jnp.concatenate([v["x_b"][c], jnp.concatenate([jnp.zeros_like(v_sb), v_sb], axis=1)],
                                      axis=0)
                out = _dot(jnp.concatenate([v["a_34"][c], v["bk_t"][c]], axis=0), rhs)
                dst = chunks[c]
                q_m = unstack(v["r_s"][c] + out[:two, :LANES]).astype(BF16)
                g_t = out[two:, :LANES] + eye * v["e_tot"][c]
                g_hi = g_t.astype(BF16)
                g_lo = (g_t - g_hi.astype(F32)).astype(BF16)
                qg_s[dst] = jnp.concatenate([q_m, g_hi, g_lo], axis=0)
                yl_s[dst] = unstack(out[:two, LANES:])
                nm_s[dst] = out[two:, LANES:]

        n_rounds = CHUNK.bit_length() - 2
        return [front, masks, square] + [doubling] * (n_rounds - 1) + [last_doubling, solve, handover]

    ep = {}

    def epilogue_mean():
        s_ref[...] = carry["state"]
        y = y_ref[...]
        ep["d"] = y - _split_dot(y, hmean_ref[...])

    def epilogue_var():
        ep["var"] = _split_dot(ep["d"] * ep["d"], hmean_ref[...])

    def epilogue_store():
        yn = ep["d"] * lax.rsqrt(ep["var"] + GN_EPS) * lnw_ref[...] + lnb_ref[...]
        z_ref[0] = ((yn + bg_s[0]) * bg_s[1]).astype(z_ref.dtype)
        rk = r_ref[0].astype(F32) * k_ref[0].astype(F32) * rk_ref[...]
        bg_s[0] = _split_dot(rk, hsum_ref[...]) * v_ref[0].astype(F32)
        bg_s[1] = g_ref[0].astype(F32)

    size = -(-n_chunks // n_groups)
    groups = [group_stages(list(range(lo, min(lo + size, n_chunks)))) for lo in range(0, n_chunks, size)]
    n_stages = len(groups[0])
    slots = n_stages - 1 + GROUP_LAG * (len(groups) - 1)
    per_slot = -(-n_chunks // max(slots - 3, 1))
    for t in range(slots):
        tail_steps(per_slot)
        if t == slots - 3:
            tail_steps(n_chunks)
            epilogue_mean()
        if t == slots - 2:
            epilogue_var()
        for gi, stages in enumerate(groups):
            k_stage = t - GROUP_LAG * gi
            if 0 <= k_stage < n_stages - 1:
                stages[k_stage]()
    epilogue_store()
    for stages in groups:
        stages[-1]()


def _chunk_cumsum(x):
    t = lax.broadcasted_iota(jnp.int32, x.shape, 0) % CHUNK
    sh = 1
    while sh < CHUNK:
        x = x + jnp.where(t >= sh, pltpu.roll(x, sh, axis=0), 0.0)
        sh *= 2
    return x


def _scan_layer(r, lw, k, v, an, bb, g, r_k, lnx_w, lnx_b, *, side=(), n_chunks=16, n_groups=2):
    B, S, D = r.shape
    n_chunks = min(n_chunks, S // CHUNK)
    L = n_chunks * CHUNK
    nb, nj = S // L, D // LANES
    n_blocks = B * nj * nb

    def coords(f):
        return f // (nj * nb), f % nb, (f // nb) % nj

    cur = lambda s: coords(jnp.minimum(s, n_blocks - 1))
    prev = lambda s: coords(jnp.maximum(s - 1, 0))
    blk_in = pl.BlockSpec((1, L, LANES), cur)
    vec_in = pl.BlockSpec((1, LANES), lambda s: (0, cur(s)[2]))
    vec_out = pl.BlockSpec((1, LANES), lambda s: (0, prev(s)[2]))
    const = lambda n: pl.BlockSpec((n, n), lambda s: (0, 0))
    two = 2 * CHUNK
    side_in, side_out, side_shapes = _side_cast_specs(side, n_blocks + 1, lambda s: s)
    return pl.pallas_call(
        functools.partial(_scan_kernel, n_chunks=n_chunks, n_groups=n_groups, blocks_per_seq=nb),
        out_shape=[jax.ShapeDtypeStruct((B, S, D), BF16)] + side_shapes,
        grid=(n_blocks + 1,),
        in_specs=[blk_in] * 7 + [vec_in, vec_out, vec_out] + [const(LANES), const(LANES)] + side_in,
        out_specs=[pl.BlockSpec((1, L, LANES), prev)] + side_out,
        scratch_shapes=[
            pltpu.VMEM((LANES, LANES), F32),
            pltpu.VMEM((L, LANES), F32),
            pltpu.VMEM((n_chunks, CHUNK + 2 * two, LANES), BF16),
            pltpu.VMEM((n_chunks, CHUNK, LANES), F32),
            pltpu.VMEM((n_chunks, two, LANES), F32),
            pltpu.VMEM((2, L, LANES), F32),
        ],
        compiler_params=_params(("arbitrary",)),
        name="rwkv_scan",
    )(r, lw, k, v, an, bb, g, r_k.reshape(1, D), lnx_w.reshape(1, D), lnx_b.reshape(1, D),
      _head_sum_matrix(LANES, 1.0 / HEAD), _head_sum_matrix(LANES), *[w for w, _ in side])


def _oproj_kernel(z_ref, w_ref, h_ref, o_ref):
    o_ref[...] = h_ref[...] + _dot(z_ref[...], w_ref[...])


def _oproj_layer(z, w, h, *, tm=1024, tn=1024):
    T, D = h.shape
    tm, tn = min(tm, T), min(tn, D)
    return pl.pallas_call(
        _oproj_kernel,
        out_shape=jax.ShapeDtypeStruct((T, D), F32),
        grid=(T // tm, D // tn),
        in_specs=[
            pl.BlockSpec((tm, D), lambda i, n: (i, 0)),
            pl.BlockSpec((D, tn), lambda i, n: (0, n)),
            pl.BlockSpec((tm, tn), lambda i, n: (i, n)),
        ],
        out_specs=pl.BlockSpec((tm, tn), lambda i, n: (i, n)),
        compiler_params=_params(("parallel", "parallel")),
        name="rwkv_oproj",
    )(z, w, h)


def kernel(x, norm1_g, norm2_g, final_g, pool_w, pool_b, pool_scale, rw_mu, rw_r, rw_k, rw_v, rw_o, rw_w0, rw_w_la, rw_w_lb, rw_a0, rw_a_la, rw_a_lb, rw_g_la, rw_g_lb, rw_k_k, rw_k_a, rw_r_k, rw_lnx_w, rw_lnx_b, ffn_w1, ffn_w3, ffn_w2):
    B, S, D = x.shape
    T = B * S
    bf = lambda w: w.astype(BF16)

    h, w1, w3, w2 = _pool_layer(x, norm1_g[0], bf(pool_w[0]), pool_b[0].reshape(-1), pool_scale[0],
                                side=((ffn_w1, 0), (ffn_w3, 0), (ffn_w2, 0)))
    h, w_r, w_k, w_v, w_o = _ffn_layer(
        h.reshape(T, D), norm2_g[0], w1, w3, w2, final_g,
        final_norm=False, side=((rw_r, 0), (rw_k, 0), (rw_v, 0), (rw_o, 0)))

    r, lw, k, v, an, bb, g = _proj_layer(
        h, S, norm1_g[1], rw_mu[0], w_r, w_k, w_v, rw_w0[0],
        bf(rw_w_la[0]), bf(rw_w_lb[0]), rw_a0[0], bf(rw_a_la[0]), bf(rw_a_lb[0]),
        bf(rw_g_la[0]), bf(rw_g_lb[0]), rw_k_k[0], rw_k_a[0])
    s3 = lambda t: t.reshape(B, S, D)
    z, w1, w3, w2 = _scan_layer(s3(r), s3(lw), s3(k), s3(v), s3(an), s3(bb), s3(g),
                                rw_r_k[0].reshape(-1), rw_lnx_w[0], rw_lnx_b[0],
                                side=((ffn_w1, 1), (ffn_w3, 1), (ffn_w2, 1)))
    h = _oproj_layer(z.reshape(T, D), w_o, h)
    h, = _ffn_layer(h, norm2_g[1], w1, w3, w2, final_g, final_norm=True)
    return h.reshape(B, S, D)
```

```python
import functools

import jax
import jax.numpy as jnp
from jax import lax
from jax.experimental import pallas as pl
from jax.experimental.pallas import tpu as pltpu

F32 = jnp.float32
BF16 = jnp.bfloat16

RMS_EPS = 1e-6
GN_EPS = 64e-5
L2_EPS = 1e-12
DECAY_SCALE = 0.6065306597126334
POOL_WINDOWS = (2, 4, 8, 16)
POOL_HALO = 16
HEAD = 64
LANES = 128
MXU_WIDTH = 256
CHUNK = 64
GROUP_LAG = 2
VMEM_LIMIT = 56 * 1024 * 1024


def _rms(x, g):
    return x * lax.rsqrt(jnp.mean(x * x, axis=-1, keepdims=True) + RMS_EPS) * g


def _dot(a, b):
    return jnp.dot(a, b, preferred_element_type=F32)


def _dot_nt(a, b):
    return lax.dot_general(a, b, (((1,), (1,)), ((), ())), preferred_element_type=F32)


def _split_dot(x, w, pieces=2):
    acc = None
    rem = x
    for _ in range(pieces):
        p = rem.astype(BF16)
        rem = rem - p.astype(F32)
        t = _dot(p, w)
        acc = t if acc is None else acc + t
    return acc


def _params(sem):
    return pltpu.CompilerParams(dimension_semantics=sem, vmem_limit_bytes=VMEM_LIMIT)


def _pool_kernel(x_ref, xprev_ref, g_ref, w_ref, b_ref, sc_ref, *rest, ts, gdim):
    n_side = (len(rest) - 1) // 2
    side_in, o_ref, side_out = rest[:n_side], rest[n_side], rest[n_side + 1:]
    _side_casts(side_in, side_out)
    i = pl.program_id(1)
    g = g_ref[...]
    xc = x_ref[0]
    hn = _rms(xc, g)
    hp = _rms(xprev_ref[0], g)
    hp = jnp.where(i == 0, 0.0, hp)
    ext = jnp.concatenate([hp, hn], axis=0)
    rows = ts + POOL_HALO
    tau = lax.broadcasted_iota(jnp.int32, (ts, gdim), 0) + i * ts
    for gi, win in enumerate(POOL_WINDOWS):
        lo = gi * gdim
        e = ext[:, lo:lo + gdim]
        acc = e
        step = 1
        while step < win:
            acc = acc + pltpu.roll(acc, step, axis=0)
            step *= 2
        wsum = acc[POOL_HALO:rows]
        cnt = jnp.minimum(tau + 1, win).astype(F32)
        pooled = wsum / cnt - hn[:, lo:lo + gdim]
        mixed = _dot(pooled.astype(BF16), w_ref[gi]) + b_ref[:, lo:lo + gdim]
        o_ref[0, :, lo:lo + gdim] = xc[:, lo:lo + gdim] + mixed * sc_ref[:, lo:lo + gdim]


def _pool_layer(x, g, w_bf, b, scale, *, side=(), ts=512):
    B, S, D = x.shape
    G, C, _ = w_bf.shape
    ts = min(ts, S)
    halo_blocks = ts // POOL_HALO
    n_s = S // ts
    side_in, side_out, side_shapes = _side_cast_specs(side, B * n_s, lambda b, i: b * n_s + i)
    return pl.pallas_call(
        functools.partial(_pool_kernel, ts=ts, gdim=C),
        out_shape=[jax.ShapeDtypeStruct((B, S, D), F32)] + side_shapes,
        grid=(B, n_s),
        in_specs=[
            pl.BlockSpec((1, ts, D), lambda b, i: (b, i, 0)),
            pl.BlockSpec((1, POOL_HALO, D), lambda b, i: (b, jnp.maximum(i * halo_blocks - 1, 0), 0)),
            pl.BlockSpec((1, D), lambda b, i: (0, 0)),
            pl.BlockSpec((G, C, C), lambda b, i: (0, 0, 0)),
            pl.BlockSpec((1, D), lambda b, i: (0, 0)),
            pl.BlockSpec((1, D), lambda b, i: (0, 0)),
        ] + side_in,
        out_specs=[pl.BlockSpec((1, ts, D), lambda b, i: (b, i, 0))] + side_out,
        compiler_params=_params(("arbitrary", "arbitrary")),
        name="pool_mixer",
    )(x, x, g.reshape(1, D), w_bf, b.reshape(1, D), scale.reshape(1, D), *[w for w, _ in side])


def _ffn_kernel(h_ref, g_ref, w1_ref, w3_ref, w2_ref, fg_ref, *rest, n_i, n_f, final_norm, tn):
    n_side = (len(rest) - 4) // 2
    side_in, o_hbm, side_out = rest[:n_side], rest[n_side], rest[n_side + 1:2 * n_side + 1]
    hn_ref, acc_ref, sem = rest[2 * n_side + 1:]
    _side_casts(side_in, side_out)
    i = pl.program_id(0)
    f = pl.program_id(1)
    tm, d = acc_ref.shape
    slabs = [slice(n, n + tn) for n in range(0, d, tn)]

    def out_copy(k, tile):
        return pltpu.make_async_copy(acc_ref.at[:, slabs[k]], o_hbm.at[pl.ds(tile * tm, tm), slabs[k]],
                                     sem.at[k])

    def wait_out(tile):
        for k in range(len(slabs)):
            out_copy(k, tile).wait()

    @pl.when(f == 0)
    def _():
        hn_ref[...] = _rms(h_ref[...], g_ref[...]).astype(BF16)

        @pl.when(i > 0)
        def _():
            wait_out(i - 1)
        acc_ref[...] = h_ref[...]

    def body(is_last):
        hn = hn_ref[...]
        a = _dot(hn, w1_ref[...])
        b = _dot(hn, w3_ref[...])
        act = (a * jax.nn.sigmoid(a) * b).astype(BF16)
        for k, cs in enumerate(slabs):
            acc_ref[:, cs] += _dot(act, w2_ref[:, cs])
            if is_last and not final_norm:
                out_copy(k, i).start()
        if is_last and final_norm:
            x = acc_ref[...]
            scale = lax.rsqrt(jnp.mean(x * x, axis=-1, keepdims=True) + RMS_EPS)
            for k, cs in enumerate(slabs):
                acc_ref[:, cs] = x[:, cs] * scale * fg_ref[:, cs]
                out_copy(k, i).start()

    @pl.when(f < n_f - 1)
    def _():
        body(False)

    @pl.when(f == n_f - 1)
    def _():
        body(True)

        @pl.when(i == n_i - 1)
        def _():
            wait_out(i)


def _ffn_layer(h, g, w1, w3, w2, final_g, *, final_norm, side=(), tm=1024, tf=512, tn=512):
    T, D = h.shape
    F = w1.shape[1]
    tm, tf, tn = min(tm, T), min(tf, F), min(tn, D)
    while F % tf:
        tf -= LANES
    assert T % tm == 0 and D % tn == 0 and tf > 0
    n_f = F // tf
    n_i = T // tm
    side_in, side_out, side_shapes = _side_cast_specs(side, n_i * n_f, lambda i, f: i * n_f + f)
    outs = pl.pallas_call(
        functools.partial(_ffn_kernel, n_i=n_i, n_f=n_f, final_norm=final_norm, tn=tn),
        out_shape=[jax.ShapeDtypeStruct((T, D), F32)] + side_shapes,
        grid=(n_i, n_f),
        in_specs=[
            pl.BlockSpec((tm, D), lambda i, f: (i, 0)),
            pl.BlockSpec((1, D), lambda i, f: (0, 0)),
            pl.BlockSpec((D, tf), lambda i, f: (0, f)),
            pl.BlockSpec((D, tf), lambda i, f: (0, f)),
            pl.BlockSpec((tf, D), lambda i, f: (f, 0)),
            pl.BlockSpec((1, D), lambda i, f: (0, 0)),
        ] + side_in,
        out_specs=[pl.BlockSpec(memory_space=pl.ANY)] + side_out,
        scratch_shapes=[pltpu.VMEM((tm, D), BF16), pltpu.VMEM((tm, D), F32),
                        pltpu.SemaphoreType.DMA((D // tn,))],
        compiler_params=_params(("arbitrary", "arbitrary")),
        name="ffn_final" if final_norm else "ffn",
    )(h, g.reshape(1, D), w1, w3, w2, final_g.reshape(1, D), *[w for w, _ in side])
    return outs


def _side_casts(side_in, side_out):
    for w_ref, o_ref in zip(side_in, side_out):
        o_ref[...] = w_ref[0].astype(o_ref.dtype)


def _side_cast_specs(side, n_steps, flat_step):
    in_specs, out_specs, shapes = [], [], []
    for w, layer in side:
        _, R, C = w.shape
        tr = next(t for t in range(16, R + 1, 16) if R % t == 0 and R // t <= n_steps)
        last = R // tr - 1

        def blk(*idx, last=last):
            return jnp.minimum(flat_step(*idx), last)

        in_specs.append(pl.BlockSpec((1, tr, C), lambda *idx, blk=blk, layer=layer: (layer, blk(*idx), 0)))
        out_specs.append(pl.BlockSpec((tr, C), lambda *idx, blk=blk: (blk(*idx), 0)))
        shapes.append(jax.ShapeDtypeStruct((R, C), BF16))
    return in_specs, out_specs, shapes


def _proj_kernel(h_ref, hprev_ref, g_ref, mu_ref, wla_ref, ala_ref, gla_ref,
                 wr_ref, wk_ref, wv_ref, wlb_ref, alb_ref, glb_ref,
                 w0_ref, a0_ref, kk_ref, ka_ref, hsum_ref,
                 r_out, lw_out, k_out, v_out, an_out, bb_out, g_out,
                 xr_s, xk_s, xv_s, tw_s, ta_s, tg_s, *, tm, seq, n_tiles, n_steps):
    i = pl.program_id(0)
    n = pl.program_id(1)
    rq = tm // n_steps
    slab = hsum_ref.shape[0]
    slabs = [slice(lo, lo + slab) for lo in range(0, r_out.shape[1], slab)]

    def prep():
        slot = i % 2
        r0 = pl.multiple_of(n * rq, rq)
        dst = pl.ds(r0, rq)
        g = g_ref[...]
        hn = _rms(h_ref[dst, :], g)
        inside = h_ref[pl.ds(pl.multiple_of(jnp.maximum(r0 - 8, 0), 8), 8), :]
        prev8 = jnp.where(n == 0, hprev_ref[...], inside)
        hp = _rms(prev8[7:8, :], g)
        hp = jnp.where((n == 0) & ((i * tm) % seq == 0), 0.0, hp)
        row = lax.broadcasted_iota(jnp.int32, hn.shape, 0)
        shifted = jnp.where(row == 0, hp, pltpu.roll(hn, 1, axis=0))
        xx = shifted - hn
        xr_s[slot, dst, :] = (hn + xx * mu_ref[0:1, :]).astype(BF16)
        xk_s[slot, dst, :] = (hn + xx * mu_ref[2:3, :]).astype(BF16)
        xv_s[slot, dst, :] = (hn + xx * mu_ref[3:4, :]).astype(BF16)
        xw = (hn + xx * mu_ref[1:2, :]).astype(BF16)
        tw_s[slot, dst, :] = jnp.tanh(_dot(xw, wla_ref[...])).astype(BF16)
        xa = (hn + xx * mu_ref[4:5, :]).astype(BF16)
        ta_s[slot, dst, :] = _dot(xa, ala_ref[...]).astype(BF16)
        xg = (hn + xx * mu_ref[5:6, :]).astype(BF16)
        tg_s[slot, dst, :] = jax.nn.sigmoid(_dot(xg, gla_ref[...])).astype(BF16)

    def project(between):
        slot = (i + 1) % 2
        half = tm
        units = [(pl.ds(r0, half), cs) for r0 in range(0, tm, half) for cs in slabs]

        def matmuls(rows, cs):
            r = _dot(xr_s[slot, rows, :], wr_ref[:, cs])
            k = _dot(xk_s[slot, rows, :], wk_ref[:, cs])
            v = _dot(xv_s[slot, rows, :], wv_ref[:, cs])
            wl = _dot(tw_s[slot, rows, :], wlb_ref[:, cs])
            al = _dot(ta_s[slot, rows, :], alb_ref[:, cs])
            gg = _dot(tg_s[slot, rows, :], glb_ref[:, cs])
            return r, k, v, wl, al, gg

        def tail(rows, cs, r, k, v, wl, al, gg):
            lw_out[rows, cs] = -DECAY_SCALE * jax.nn.sigmoid(wl + w0_ref[:, cs])
            a_sig = jax.nn.sigmoid(al + a0_ref[:, cs])
            kk = k * kk_ref[:, cs]
            ss = _split_dot(kk * kk, hsum_ref[...])
            kk = kk * jnp.minimum(lax.rsqrt(ss), 1.0 / L2_EPS)
            r_out[rows, cs] = r.astype(r_out.dtype)
            k_out[rows, cs] = (k * (1.0 + (a_sig - 1.0) * ka_ref[:, cs])).astype(k_out.dtype)
            v_out[rows, cs] = v.astype(v_out.dtype)
            an_out[rows, cs] = (-kk).astype(an_out.dtype)
            bb_out[rows, cs] = (kk * a_sig).astype(bb_out.dtype)
            g_out[rows, cs] = gg.astype(g_out.dtype)

        pending = None
        for u, (rows, cs) in enumerate(units):
            res = matmuls(rows, cs)
            if pending is not None:
                tail(*pending)
            pending = (rows, cs) + res
            if u == 0:
                between()
        tail(*pending)

    @pl.when(i == 0)
    def _():
        prep()

    @pl.when((i > 0) & (i < n_tiles))
    def _():
        project(prep)

    @pl.when(i == n_tiles)
    def _():
        project(lambda: None)


def _head_sum_matrix(n, value=1.0):
    hi = lax.broadcasted_iota(jnp.int32, (n, n), 0) // HEAD
    hj = lax.broadcasted_iota(jnp.int32, (n, n), 1) // HEAD
    return jnp.where(hi == hj, value, 0.0).astype(BF16)


def _proj_layer(h, seq, g, mu, w_r, w_k, w_v, w0, w_la, w_lb, a0, a_la, a_lb, g_la, g_lb, k_k, k_a,
                *, tm=512, tn=512):
    T, D = h.shape
    tm, tn = min(tm, T), min(tn, D)
    slab = min(MXU_WIDTH, tn)
    n_tiles, n_steps = T // tm, D // tn
    dl, al, gl = w_la.shape[1], a_la.shape[1], g_la.shape[1]
    row = lambda x: x.reshape(1, D)
    full = lambda shp: pl.BlockSpec(shp, lambda i, n: (0, 0))
    col = lambda rows: pl.BlockSpec((rows, tn), lambda i, n: (0, n))
    out_spec = pl.BlockSpec((tm, tn), lambda i, n: (jnp.maximum(i - 1, 0), jnp.where(i == 0, 0, n)))
    prev_blocks = tm // 8
    tile = lambda i: jnp.minimum(i, n_tiles - 1)
    out_dtypes = [BF16, F32, BF16, BF16, BF16, BF16, BF16]
    outs = pl.pallas_call(
        functools.partial(_proj_kernel, tm=tm, seq=seq, n_tiles=n_tiles, n_steps=n_steps),
        out_shape=[jax.ShapeDtypeStruct((T, D), dt) for dt in out_dtypes],
        grid=(n_tiles + 1, n_steps),
        in_specs=[
            pl.BlockSpec((tm, D), lambda i, n: (tile(i), 0)),
            pl.BlockSpec((8, D), lambda i, n: (jnp.maximum(tile(i) * prev_blocks - 1, 0), 0)),
            full((1, D)), full((8, D)), full((D, dl)), full((D, al)), full((D, gl)),
            col(D), col(D), col(D), col(dl), col(al), col(gl),
            col(1), col(1), col(1), col(1),
            full((slab, slab)),
        ],
        out_specs=[out_spec] * 7,
        scratch_shapes=[pltpu.VMEM((2, tm, D), BF16)] * 3
        + [pltpu.VMEM((2, tm, dl), BF16), pltpu.VMEM((2, tm, al), BF16), pltpu.VMEM((2, tm, gl), BF16)],
        compiler_params=_params(("arbitrary", "arbitrary")),
        name="rwkv_proj",
    )(h, h, row(g), jnp.pad(mu, ((0, 8 - mu.shape[0]), (0, 0))), w_la, a_la, g_la,
      w_r, w_k, w_v, w_lb, a_lb, g_lb, row(w0), row(a0), row(k_k), row(k_a),
      _head_sum_matrix(slab))
    return outs


def _scan_kernel(r_ref, lw_ref, k_ref, v_ref, an_ref, bb_ref, g_ref, rk_ref, lnw_ref, lnb_ref,
                 hmean_ref, hsum_ref, *rest, n_chunks, n_groups, blocks_per_seq):
    n_side = (len(rest) - 7) // 2
    side_in, z_ref, side_out = rest[:n_side], rest[n_side], rest[n_side + 1:2 * n_side + 1]
    s_ref, y_ref, qg_s, yl_s, nm_s, bg_s = rest[2 * n_side + 1:]
    _side_casts(side_in, side_out)
    step = pl.program_id(0)

    @pl.when(step == 0)
    def _():
        for ref in (s_ref, qg_s, yl_s, nm_s, bg_s):
            ref[...] = jnp.zeros_like(ref)

    two = 2 * CHUNK
    lane_head0 = lax.broadcasted_iota(jnp.int32, (CHUNK, LANES), 1) < HEAD
    ri = lax.broadcasted_iota(jnp.int32, (two, two), 0)
    ci = lax.broadcasted_iota(jnp.int32, (two, two), 1)
    same = (ri >= CHUNK) == (ci >= CHUNK)
    strict = same & (ci < ri)
    incl = same & (ci <= ri)
    eye = jnp.where(ri == ci, 1.0, 0.0).astype(F32)

    def stack(x):
        return jnp.concatenate([jnp.where(lane_head0, x, 0.0), jnp.where(lane_head0, 0.0, x)], axis=0)

    def unstack(x):
        return x[:CHUNK] + x[CHUNK:]

    sls = [pl.ds(c * CHUNK, CHUNK) for c in range(n_chunks)]

    prev_first = (step + blocks_per_seq - 1) % blocks_per_seq == 0
    carry = {"state": jnp.where(prev_first, 0.0, s_ref[...]), "next": 0}

    def tail_steps(count):
        for c in range(carry["next"], min(carry["next"] + count, n_chunks)):
            res = _dot(qg_s[c], carry["state"].astype(BF16))
            y_ref[sls[c], :] = res[:CHUNK] + yl_s[c]
            carry["state"] = res[CHUNK:CHUNK + two] + res[CHUNK + two:] + nm_s[c]
            carry["next"] = c + 1

    def group_stages(chunks):
        n = len(chunks)
        rng = range(n)
        v = {}

        def front():
            rows = pl.ds(chunks[0] * CHUNK, n * CHUNK)
            lw_all = lw_ref[0, rows, :]
            cum_all = _chunk_cumsum(lw_all)
            e_in_all = jnp.exp(cum_all)
            e_ex_all = jnp.exp(cum_all - lw_all)
            e_neg_all = jnp.exp(-cum_all)
            v["a_s"], v["r_s"], v["v_sb"], v["bk_t"], v["e_tot"], v["gram"] = [], [], [], [], [], []
            for c in rng:
                sl = slice(c * CHUNK, (c + 1) * CHUNK)
                src = sls[chunks[c]]
                cum = cum_all[sl]
                e_rem = jnp.exp(cum[CHUNK - 1:CHUNK, :] - cum)
                k = k_ref[0, src, :]
                bb = bb_ref[0, src, :]
                a_s = stack(an_ref[0, src, :] * e_ex_all[sl])
                r_s = stack(r_ref[0, src, :] * e_in_all[sl])
                v["a_s"].append(a_s)
                v["r_s"].append(r_s)
                v["v_sb"].append(stack(v_ref[0, src, :]).astype(BF16))
                v["bk_t"].append(
                    jnp.concatenate([stack(bb * e_rem).T, stack(k * e_rem).T], axis=1).astype(BF16))
                v["e_tot"].append(e_in_all[sl][CHUNK - 1:CHUNK, :])
                b_t = (bb * e_neg_all[sl]).astype(BF16)
                k_t = (k * e_neg_all[sl]).astype(BF16)
                lhs = jnp.concatenate([a_s, r_s], axis=0).astype(BF16)
                rhs = jnp.concatenate([b_t, b_t, k_t, k_t], axis=0)
                v["gram"].append(_dot_nt(lhs, rhs))

        def masks():
            gram = v.pop("gram")
            v["p_m"] = [jnp.where(strict, gm[:two, :two], 0.0) for gm in gram]
            v["a_34"] = [jnp.concatenate([jnp.where(incl, gm[two:, :two], 0.0),
                                          jnp.where(incl, gm[two:, two:], 0.0)], axis=1).astype(BF16)
                         for gm in gram]
            v["a2v"] = [_dot(jnp.where(strict, gram[c][:two, two:], 0.0).astype(BF16), v["v_sb"][c])
                        for c in rng]
            v["t_m"] = [eye + p for p in v["p_m"]]

        def square():
            p_b = [p.astype(BF16) for p in v["p_m"]]
            v["p_m"] = [_dot(pb, pb) for pb in p_b]

        def doubling():
            p_b = [p.astype(BF16) for p in v["p_m"]]
            out = [_dot(jnp.concatenate([v["t_m"][c].astype(BF16), p_b[c]], axis=0), p_b[c]) for c in rng]
            v["t_m"] = [v["t_m"][c] + out[c][:two] for c in rng]
            v["p_m"] = [o[two:] for o in out]

        def last_doubling():
            v["t_m"] = [v["t_m"][c] + _dot(v["t_m"][c].astype(BF16), v["p_m"][c].astype(BF16)) for c in rng]

        def solve():
            v["x_b"] = [_dot(v["t_m"][c].astype(BF16),
                             jnp.concatenate([v["a_s"][c], v["a2v"][c]], axis=1).astype(BF16)).astype(BF16)
                        for c in rng]

        def handover():
            for c in rng:
                v_sb = v["v_sb"][c]
                rhs = jnp.concatenate([v["x_b"][c], jnp.concatenate([jnp.zeros_like(v_sb), v_sb], axis=1)],
                                      axis=0)
                out = _dot(jnp.concatenate([v["a_34"][c], v["bk_t"][c]], axis=0), rhs)
                dst = chunks[c]
                q_m = unstack(v["r_s"][c] + out[:two, :LANES]).astype(BF16)
                g_t = out[two:, :LANES] + eye * v["e_tot"][c]
                g_hi = g_t.astype(BF16)
                g_lo = (g_t - g_hi.astype(F32)).astype(BF16)
                qg_s[dst] = jnp.concatenate([q_m, g_hi, g_lo], axis=0)
                yl_s[dst] = unstack(out[:two, LANES:])
                nm_s[dst] = out[two:, LANES:]

        n_rounds = CHUNK.bit_length() - 2
        return [front, masks, square] + [doubling] * (n_rounds - 1) + [last_doubling, solve, handover]

    ep = {}

    def epilogue_mean():
        s_ref[...] = carry["state"]
        y = y_ref[...]
        ep["d"] = y - _split_dot(y, hmean_ref[...])

    def epilogue_var():
        ep["var"] = _split_dot(ep["d"] * ep["d"], hmean_ref[...])

    def epilogue_store():
        yn = ep["d"] * lax.rsqrt(ep["var"] + GN_EPS) * lnw_ref[...] + lnb_ref[...]
        z_ref[0] = ((yn + bg_s[0]) * bg_s[1]).astype(z_ref.dtype)
        rk = r_ref[0].astype(F32) * k_ref[0].astype(F32) * rk_ref[...]
        bg_s[0] = _split_dot(rk, hsum_ref[...]) * v_ref[0].astype(F32)
        bg_s[1] = g_ref[0].astype(F32)

    size = -(-n_chunks // n_groups)
    groups = [group_stages(list(range(lo, min(lo + size, n_chunks)))) for lo in range(0, n_chunks, size)]
    n_stages = len(groups[0])
    slots = n_stages - 1 + GROUP_LAG * (len(groups) - 1)
    per_slot = -(-n_chunks // max(slots - 3, 1))
    for t in range(slots):
        tail_steps(per_slot)
        if t == slots - 3:
            tail_steps(n_chunks)
            epilogue_mean()
        if t == slots - 2:
            epilogue_var()
        for gi, stages in enumerate(groups):
            k_stage = t - GROUP_LAG * gi
            if 0 <= k_stage < n_stages - 1:
                stages[k_stage]()
    epilogue_store()
    for stages in groups:
        stages[-1]()


def _chunk_cumsum(x):
    t = lax.broadcasted_iota(jnp.int32, x.shape, 0) % CHUNK
    sh = 1
    while sh < CHUNK:
        x = x + jnp.where(t >= sh, pltpu.roll(x, sh, axis=0), 0.0)
        sh *= 2
    return x


def _scan_layer(r, lw, k, v, an, bb, g, r_k, lnx_w, lnx_b, *, side=(), n_chunks=16, n_groups=2):
    B, S, D = r.shape
    n_chunks = min(n_chunks, S // CHUNK)
    L = n_chunks * CHUNK
    nb, nj = S // L, D // LANES
    n_blocks = B * nj * nb

    def coords(f):
        return f // (nj * nb), f % nb, (f // nb) % nj

    cur = lambda s: coords(jnp.minimum(s, n_blocks - 1))
    prev = lambda s: coords(jnp.maximum(s - 1, 0))
    blk_in = pl.BlockSpec((1, L, LANES), cur)
    vec_in = pl.BlockSpec((1, LANES), lambda s: (0, cur(s)[2]))
    vec_out = pl.BlockSpec((1, LANES), lambda s: (0, prev(s)[2]))
    const = lambda n: pl.BlockSpec((n, n), lambda s: (0, 0))
    two = 2 * CHUNK
    side_in, side_out, side_shapes = _side_cast_specs(side, n_blocks + 1, lambda s: s)
    return pl.pallas_call(
        functools.partial(_scan_kernel, n_chunks=n_chunks, n_groups=n_groups, blocks_per_seq=nb),
        out_shape=[jax.ShapeDtypeStruct((B, S, D), BF16)] + side_shapes,
        grid=(n_blocks + 1,),
        in_specs=[blk_in] * 7 + [vec_in, vec_out, vec_out] + [const(LANES), const(LANES)] + side_in,
        out_specs=[pl.BlockSpec((1, L, LANES), prev)] + side_out,
        scratch_shapes=[
            pltpu.VMEM((LANES, LANES), F32),
            pltpu.VMEM((L, LANES), F32),
            pltpu.VMEM((n_chunks, CHUNK + 2 * two, LANES), BF16),
            pltpu.VMEM((n_chunks, CHUNK, LANES), F32),
            pltpu.VMEM((n_chunks, two, LANES), F32),
            pltpu.VMEM((2, L, LANES), F32),
        ],
        compiler_params=_params(("arbitrary",)),
        name="rwkv_scan",
    )(r, lw, k, v, an, bb, g, r_k.reshape(1, D), lnx_w.reshape(1, D), lnx_b.reshape(1, D),
      _head_sum_matrix(LANES, 1.0 / HEAD), _head_sum_matrix(LANES), *[w for w, _ in side])


def _oproj_kernel(z_ref, w_ref, h_ref, o_ref):
    o_ref[...] = h_ref[...] + _dot(z_ref[...], w_ref[...])


def _oproj_layer(z, w, h, *, tm=1024, tn=1024):
    T, D = h.shape
    tm, tn = min(tm, T), min(tn, D)
    return pl.pallas_call(
        _oproj_kernel,
        out_shape=jax.ShapeDtypeStruct((T, D), F32),
        grid=(T // tm, D // tn),
        in_specs=[
            pl.BlockSpec((tm, D), lambda i, n: (i, 0)),
            pl.BlockSpec((D, tn), lambda i, n: (0, n)),
            pl.BlockSpec((tm, tn), lambda i, n: (i, n)),
        ],
        out_specs=pl.BlockSpec((tm, tn), lambda i, n: (i, n)),
        compiler_params=_params(("parallel", "parallel")),
        name="rwkv_oproj",
    )(z, w, h)


def kernel(x, norm1_g, norm2_g, final_g, pool_w, pool_b, pool_scale, rw_mu, rw_r, rw_k, rw_v, rw_o, rw_w0, rw_w_la, rw_w_lb, rw_a0, rw_a_la, rw_a_lb, rw_g_la, rw_g_lb, rw_k_k, rw_k_a, rw_r_k, rw_lnx_w, rw_lnx_b, ffn_w1, ffn_w3, ffn_w2):
    B, S, D = x.shape
    T = B * S
    bf = lambda w: w.astype(BF16)

    h, w1, w3, w2 = _pool_layer(x, norm1_g[0], bf(pool_w[0]), pool_b[0].reshape(-1), pool_scale[0],
                                side=((ffn_w1, 0), (ffn_w3, 0), (ffn_w2, 0)))
    h, w_r, w_k, w_v, w_o = _ffn_layer(
        h.reshape(T, D), norm2_g[0], w1, w3, w2, final_g,
        final_norm=False, side=((rw_r, 0), (rw_k, 0), (rw_v, 0), (rw_o, 0)))

    r, lw, k, v, an, bb, g = _proj_layer(
        h, S, norm1_g[1], rw_mu[0], w_r, w_k, w_v, rw_w0[0],
        bf(rw_w_la[0]), bf(rw_w_lb[0]), rw_a0[0], bf(rw_a_la[0]), bf(rw_a_lb[0]),
        bf(rw_g_la[0]), bf(rw_g_lb[0]), rw_k_k[0], rw_k_a[0])
    s3 = lambda t: t.reshape(B, S, D)
    z, w1, w3, w2 = _scan_layer(s3(r), s3(lw), s3(k), s3(v), s3(an), s3(bb), s3(g),
                                rw_r_k[0].reshape(-1), rw_lnx_w[0], rw_lnx_b[0],
                                side=((ffn_w1, 1), (ffn_w3, 1), (ffn_w2, 1)))
    h = _oproj_layer(z.reshape(T, D), w_o, h)
    h, = _ffn_layer(h, norm2_g[1], w1, w3, w2, final_g, final_norm=True)
    return h.reshape(B, S, D)
```

```python
import functools

import jax
import jax.numpy as jnp
from jax import lax
from jax.experimental import pallas as pl
from jax.experimental.pallas import tpu as pltpu

F32 = jnp.float32
BF16 = jnp.bfloat16

RMS_EPS = 1e-6
GN_EPS = 64e-5
L2_EPS = 1e-12
DECAY_SCALE = 0.6065306597126334
POOL_WINDOWS = (2, 4, 8, 16)
POOL_HALO = 16
HEAD = 64
LANES = 128
MXU_WIDTH = 256
CHUNK = 64
GROUP_LAG = 2
VMEM_LIMIT = 56 * 1024 * 1024


def _rms(x, g):
    return x * lax.rsqrt(jnp.mean(x * x, axis=-1, keepdims=True) + RMS_EPS) * g


def _dot(a, b):
    return jnp.dot(a, b, preferred_element_type=F32)


def _dot_nt(a, b):
    return lax.dot_general(a, b, (((1,), (1,)), ((), ())), preferred_element_type=F32)


def _split_dot(x, w, pieces=2):
    acc = None
    rem = x
    for _ in range(pieces):
        p = rem.astype(BF16)
        rem = rem - p.astype(F32)
        t = _dot(p, w)
        acc = t if acc is None else acc + t
    return acc


def _params(sem):
    return pltpu.CompilerParams(dimension_semantics=sem, vmem_limit_bytes=VMEM_LIMIT)


def _pool_kernel(x_ref, xprev_ref, g_ref, w_ref, b_ref, sc_ref, *rest, ts, gdim):
    n_side = (len(rest) - 1) // 2
    side_in, o_ref, side_out = rest[:n_side], rest[n_side], rest[n_side + 1:]
    _side_casts(side_in, side_out)
    i = pl.program_id(1)
    g = g_ref[...]
    xc = x_ref[0]
    hn = _rms(xc, g)
    hp = _rms(xprev_ref[0], g)
    hp = jnp.where(i == 0, 0.0, hp)
    ext = jnp.concatenate([hp, hn], axis=0)
    rows = ts + POOL_HALO
    tau = lax.broadcasted_iota(jnp.int32, (ts, gdim), 0) + i * ts
    for gi, win in enumerate(POOL_WINDOWS):
        lo = gi * gdim
        e = ext[:, lo:lo + gdim]
        acc = e
        step = 1
        while step < win:
            acc = acc + pltpu.roll(acc, step, axis=0)
            step *= 2
        wsum = acc[POOL_HALO:rows]
        cnt = jnp.minimum(tau + 1, win).astype(F32)
        pooled = wsum / cnt - hn[:, lo:lo + gdim]
        mixed = _dot(pooled.astype(BF16), w_ref[gi]) + b_ref[:, lo:lo + gdim]
        o_ref[0, :, lo:lo + gdim] = xc[:, lo:lo + gdim] + mixed * sc_ref[:, lo:lo + gdim]


def _pool_layer(x, g, w_bf, b, scale, *, side=(), ts=512):
    B, S, D = x.shape
    G, C, _ = w_bf.shape
    ts = min(ts, S)
    halo_blocks = ts // POOL_HALO
    n_s = S // ts
    side_in, side_out, side_shapes = _side_cast_specs(side, B * n_s, lambda b, i: b * n_s + i)
    return pl.pallas_call(
        functools.partial(_pool_kernel, ts=ts, gdim=C),
        out_shape=[jax.ShapeDtypeStruct((B, S, D), F32)] + side_shapes,
        grid=(B, n_s),
        in_specs=[
            pl.BlockSpec((1, ts, D), lambda b, i: (b, i, 0)),
            pl.BlockSpec((1, POOL_HALO, D), lambda b, i: (b, jnp.maximum(i * halo_blocks - 1, 0), 0)),
            pl.BlockSpec((1, D), lambda b, i: (0, 0)),
            pl.BlockSpec((G, C, C), lambda b, i: (0, 0, 0)),
            pl.BlockSpec((1, D), lambda b, i: (0, 0)),
            pl.BlockSpec((1, D), lambda b, i: (0, 0)),
        ] + side_in,
        out_specs=[pl.BlockSpec((1, ts, D), lambda b, i: (b, i, 0))] + side_out,
        compiler_params=_params(("arbitrary", "arbitrary")),
        name="pool_mixer",
    )(x, x, g.reshape(1, D), w_bf, b.reshape(1, D), scale.reshape(1, D), *[w for w, _ in side])


def _ffn_kernel(h_ref, g_ref, w1_ref, w3_ref, w2_ref, fg_ref, *rest, n_i, n_f, final_norm, tn):
    n_side = (len(rest) - 4) // 2
    side_in, o_hbm, side_out = rest[:n_side], rest[n_side], rest[n_side + 1:2 * n_side + 1]
    hn_ref, acc_ref, sem = rest[2 * n_side + 1:]
    _side_casts(side_in, side_out)
    i = pl.program_id(0)
    f = pl.program_id(1)
    tm, d = acc_ref.shape
    slabs = [slice(n, n + tn) for n in range(0, d, tn)]

    def out_copy(k, tile):
        return pltpu.make_async_copy(acc_ref.at[:, slabs[k]], o_hbm.at[pl.ds(tile * tm, tm), slabs[k]],
                                     sem.at[k])

    def wait_out(tile):
        for k in range(len(slabs)):
            out_copy(k, tile).wait()

    @pl.when(f == 0)
    def _():
        hn_ref[...] = _rms(h_ref[...], g_ref[...]).astype(BF16)

        @pl.when(i > 0)
        def _():
            wait_out(i - 1)
        acc_ref[...] = h_ref[...]

    def body(is_last):
        hn = hn_ref[...]
        a = _dot(hn, w1_ref[...])
        b = _dot(hn, w3_ref[...])
        act = (a * jax.nn.sigmoid(a) * b).astype(BF16)
        for k, cs in enumerate(slabs):
            acc_ref[:, cs] += _dot(act, w2_ref[:, cs])
            if is_last and not final_norm:
                out_copy(k, i).start()
        if is_last and final_norm:
            x = acc_ref[...]
            scale = lax.rsqrt(jnp.mean(x * x, axis=-1, keepdims=True) + RMS_EPS)
            for k, cs in enumerate(slabs):
                acc_ref[:, cs] = x[:, cs] * scale * fg_ref[:, cs]
                out_copy(k, i).start()

    @pl.when(f < n_f - 1)
    def _():
        body(False)

    @pl.when(f == n_f - 1)
    def _():
        body(True)

        @pl.when(i == n_i - 1)
        def _():
            wait_out(i)


def _ffn_layer(h, g, w1, w3, w2, final_g, *, final_norm, side=(), tm=1024, tf=512, tn=512):
    T, D = h.shape
    F = w1.shape[1]
    tm, tf, tn = min(tm, T), min(tf, F), min(tn, D)
    while F % tf:
        tf -= LANES
    assert T % tm == 0 and D % tn == 0 and tf > 0
    n_f = F // tf
    n_i = T // tm
    side_in, side_out, side_shapes = _side_cast_specs(side, n_i * n_f, lambda i, f: i * n_f + f)
    outs = pl.pallas_call(
        functools.partial(_ffn_kernel, n_i=n_i, n_f=n_f, final_norm=final_norm, tn=tn),
        out_shape=[jax.ShapeDtypeStruct((T, D), F32)] + side_shapes,
        grid=(n_i, n_f),
        in_specs=[
            pl.BlockSpec((tm, D), lambda i, f: (i, 0)),
            pl.BlockSpec((1, D), lambda i, f: (0, 0)),
            pl.BlockSpec((D, tf), lambda i, f: (0, f)),
            pl.BlockSpec((D, tf), lambda i, f: (0, f)),
            pl.BlockSpec((tf, D), lambda i, f: (f, 0)),
            pl.BlockSpec((1, D), lambda i, f: (0, 0)),
        ] + side_in,
        out_specs=[pl.BlockSpec(memory_space=pl.ANY)] + side_out,
        scratch_shapes=[pltpu.VMEM((tm, D), BF16), pltpu.VMEM((tm, D), F32),
                        pltpu.SemaphoreType.DMA((D // tn,))],
        compiler_params=_params(("arbitrary", "arbitrary")),
        name="ffn_final" if final_norm else "ffn",
    )(h, g.reshape(1, D), w1, w3, w2, final_g.reshape(1, D), *[w for w, _ in side])
    return outs


def _side_casts(side_in, side_out):
    for w_ref, o_ref in zip(side_in, side_out):
        o_ref[...] = w_ref[0].astype(o_ref.dtype)


def _side_cast_specs(side, n_steps, flat_step):
    in_specs, out_specs, shapes = [], [], []
    for w, layer in side:
        _, R, C = w.shape
        tr = next(t for t in range(16, R + 1, 16) if R % t == 0 and R // t <= n_steps)
        last = R // tr - 1

        def blk(*idx, last=last):
            return jnp.minimum(flat_step(*idx), last)

        in_specs.append(pl.BlockSpec((1, tr, C), lambda *idx, blk=blk, layer=layer: (layer, blk(*idx), 0)))
        out_specs.append(pl.BlockSpec((tr, C), lambda *idx, blk=blk: (blk(*idx), 0)))
        shapes.append(jax.ShapeDtypeStruct((R, C), BF16))
    return in_specs, out_specs, shapes


def _proj_kernel(h_ref, hprev_ref, g_ref, mu_ref, wla_ref, ala_ref, gla_ref,
                 wr_ref, wk_ref, wv_ref, wlb_ref, alb_ref, glb_ref,
                 w0_ref, a0_ref, kk_ref, ka_ref, hsum_ref,
                 r_out, lw_out, k_out, v_out, an_out, bb_out, g_out,
                 xr_s, xk_s, xv_s, tw_s, ta_s, tg_s, *, tm, seq, n_tiles, n_steps):
    i = pl.program_id(0)
    n = pl.program_id(1)
    rq = tm // n_steps
    slab = hsum_ref.shape[0]
    slabs = [slice(lo, lo + slab) for lo in range(0, r_out.shape[1], slab)]

    def prep():
        slot = i % 2
        r0 = pl.multiple_of(n * rq, rq)
        dst = pl.ds(r0, rq)
        g = g_ref[...]
        hn = _rms(h_ref[dst, :], g)
        inside = h_ref[pl.ds(pl.multiple_of(jnp.maximum(r0 - 8, 0), 8), 8), :]
        prev8 = jnp.where(n == 0, hprev_ref[...], inside)
        hp = _rms(prev8[7:8, :], g)
        hp = jnp.where((n == 0) & ((i * tm) % seq == 0), 0.0, hp)
        row = lax.broadcasted_iota(jnp.int32, hn.shape, 0)
        shifted = jnp.where(row == 0, hp, pltpu.roll(hn, 1, axis=0))
        xx = shifted - hn
        xr_s[slot, dst, :] = (hn + xx * mu_ref[0:1, :]).astype(BF16)
        xk_s[slot, dst, :] = (hn + xx * mu_ref[2:3, :]).astype(BF16)
        xv_s[slot, dst, :] = (hn + xx * mu_ref[3:4, :]).astype(BF16)
        xw = (hn + xx * mu_ref[1:2, :]).astype(BF16)
        tw_s[slot, dst, :] = jnp.tanh(_dot(xw, wla_ref[...])).astype(BF16)
        xa = (hn + xx * mu_ref[4:5, :]).astype(BF16)
        ta_s[slot, dst, :] = _dot(xa, ala_ref[...]).astype(BF16)
        xg = (hn + xx * mu_ref[5:6, :]).astype(BF16)
        tg_s[slot, dst, :] = jax.nn.sigmoid(_dot(xg, gla_ref[...])).astype(BF16)

    def project(between):
        slot = (i + 1) % 2
        half = tm
        units = [(pl.ds(r0, half), cs) for r0 in range(0, tm, half) for cs in slabs]

        def matmuls(rows, cs):
            r = _dot(xr_s[slot, rows, :], wr_ref[:, cs])
            k = _dot(xk_s[slot, rows, :], wk_ref[:, cs])
            v = _dot(xv_s[slot, rows, :], wv_ref[:, cs])
            wl = _dot(tw_s[slot, rows, :], wlb_ref[:, cs])
            al = _dot(ta_s[slot, rows, :], alb_ref[:, cs])
            gg = _dot(tg_s[slot, rows, :], glb_ref[:, cs])
            return r, k, v, wl, al, gg

        def tail(rows, cs, r, k, v, wl, al, gg):
            lw_out[rows, cs] = -DECAY_SCALE * jax.nn.sigmoid(wl + w0_ref[:, cs])
            a_sig = jax.nn.sigmoid(al + a0_ref[:, cs])
            kk = k * kk_ref[:, cs]
            ss = _split_dot(kk * kk, hsum_ref[...])
            kk = kk * jnp.minimum(lax.rsqrt(ss), 1.0 / L2_EPS)
            r_out[rows, cs] = r.astype(r_out.dtype)
            k_out[rows, cs] = (k * (1.0 + (a_sig - 1.0) * ka_ref[:, cs])).astype(k_out.dtype)
            v_out[rows, cs] = v.astype(v_out.dtype)
            an_out[rows, cs] = (-kk).astype(an_out.dtype)
            bb_out[rows, cs] = (kk * a_sig).astype(bb_out.dtype)
            g_out[rows, cs] = gg.astype(g_out.dtype)

        pending = None
        for u, (rows, cs) in enumerate(units):
            res = matmuls(rows, cs)
            if pending is not None:
                tail(*pending)
            pending = (rows, cs) + res
            if u == 0:
                between()
        tail(*pending)

    @pl.when(i == 0)
    def _():
        prep()

    @pl.when((i > 0) & (i < n_tiles))
    def _():
        project(prep)

    @pl.when(i == n_tiles)
    def _():
        project(lambda: None)


def _head_sum_matrix(n, value=1.0):
    hi = lax.broadcasted_iota(jnp.int32, (n, n), 0) // HEAD
    hj = lax.broadcasted_iota(jnp.int32, (n, n), 1) // HEAD
    return jnp.where(hi == hj, value, 0.0).astype(BF16)


def _proj_layer(h, seq, g, mu, w_r, w_k, w_v, w0, w_la, w_lb, a0, a_la, a_lb, g_la, g_lb, k_k, k_a,
                *, tm=512, tn=512):
    T, D = h.shape
    tm, tn = min(tm, T), min(tn, D)
    slab = min(MXU_WIDTH, tn)
    n_tiles, n_steps = T // tm, D // tn
    dl, al, gl = w_la.shape[1], a_la.shape[1], g_la.shape[1]
    row = lambda x: x.reshape(1, D)
    full = lambda shp: pl.BlockSpec(shp, lambda i, n: (0, 0))
    col = lambda rows: pl.BlockSpec((rows, tn), lambda i, n: (0, n))
    out_spec = pl.BlockSpec((tm, tn), lambda i, n: (jnp.maximum(i - 1, 0), jnp.where(i == 0, 0, n)))
    prev_blocks = tm // 8
    tile = lambda i: jnp.minimum(i, n_tiles - 1)
    out_dtypes = [BF16, F32, BF16, BF16, BF16, BF16, BF16]
    outs = pl.pallas_call(
        functools.partial(_proj_kernel, tm=tm, seq=seq, n_tiles=n_tiles, n_steps=n_steps),
        out_shape=[jax.ShapeDtypeStruct((T, D), dt) for dt in out_dtypes],
        grid=(n_tiles + 1, n_steps),
        in_specs=[
            pl.BlockSpec((tm, D), lambda i, n: (tile(i), 0)),
            pl.BlockSpec((8, D), lambda i, n: (jnp.maximum(tile(i) * prev_blocks - 1, 0), 0)),
            full((1, D)), full((8, D)), full((D, dl)), full((D, al)), full((D, gl)),
            col(D), col(D), col(D), col(dl), col(al), col(gl),
            col(1), col(1), col(1), col(1),
            full((slab, slab)),
        ],
        out_specs=[out_spec] * 7,
        scratch_shapes=[pltpu.VMEM((2, tm, D), BF16)] * 3
        + [pltpu.VMEM((2, tm, dl), BF16), pltpu.VMEM((2, tm, al), BF16), pltpu.VMEM((2, tm, gl), BF16)],
        compiler_params=_params(("arbitrary", "arbitrary")),
        name="rwkv_proj",
    )(h, h, row(g), jnp.pad(mu, ((0, 8 - mu.shape[0]), (0, 0))), w_la, a_la, g_la,
      w_r, w_k, w_v, w_lb, a_lb, g_lb, row(w0), row(a0), row(k_k), row(k_a),
      _head_sum_matrix(slab))
    return outs


def _scan_kernel(r_ref, lw_ref, k_ref, v_ref, an_ref, bb_ref, g_ref, rk_ref, lnw_ref, lnb_ref,
                 hmean_ref, hsum_ref, *rest, n_chunks, n_groups, blocks_per_seq):
    n_side = (len(rest) - 8) // 2
    side_in, z_ref, side_out = rest[:n_side], rest[n_side], rest[n_side + 1:2 * n_side + 1]
    s_ref, y_ref, qg_s, yl_s, nm_s, ec_s, bg_s = rest[2 * n_side + 1:]
    _side_casts(side_in, side_out)
    step = pl.program_id(0)

    @pl.when(step == 0)
    def _():
        for ref in (s_ref, qg_s, yl_s, nm_s, ec_s, bg_s):
            ref[...] = jnp.zeros_like(ref)

    two = 2 * CHUNK
    lane_head0 = lax.broadcasted_iota(jnp.int32, (CHUNK, LANES), 1) < HEAD
    ri = lax.broadcasted_iota(jnp.int32, (two, two), 0)
    ci = lax.broadcasted_iota(jnp.int32, (two, two), 1)
    same = (ri >= CHUNK) == (ci >= CHUNK)
    strict = same & (ci < ri)
    incl = same & (ci <= ri)
    eye = jnp.where(ri == ci, 1.0, 0.0).astype(F32)

    def stack(x):
        return jnp.concatenate([jnp.where(lane_head0, x, 0.0), jnp.where(lane_head0, 0.0, x)], axis=0)

    def unstack(x):
        return x[:CHUNK] + x[CHUNK:]

    sls = [pl.ds(c * CHUNK, CHUNK) for c in range(n_chunks)]

    prev_first = (step + blocks_per_seq - 1) % blocks_per_seq == 0
    carry = {"state": jnp.where(prev_first, 0.0, s_ref[...]), "next": 0}

    def tail_steps(count):
        for c in range(carry["next"], min(carry["next"] + count, n_chunks)):
            state = carry["state"]
            res = _dot(qg_s[c], state.astype(BF16))
            y_ref[sls[c], :] = res[:CHUNK] + yl_s[c]
            carry["state"] = state * ec_s[c] + res[CHUNK:] + nm_s[c]
            carry["next"] = c + 1

    def group_stages(chunks):
        n = len(chunks)
        rng = range(n)
        v = {}

        def front():
            rows = pl.ds(chunks[0] * CHUNK, n * CHUNK)
            lw_all = lw_ref[0, rows, :]
            cum_all = _chunk_cumsum(lw_all)
            e_in_all = jnp.exp(cum_all)
            e_ex_all = jnp.exp(cum_all - lw_all)
            e_neg_all = jnp.exp(-cum_all)
            v["a_s"], v["r_t"], v["v_sb"], v["bk_t"], v["e_tot"], v["gram"] = [], [], [], [], [], []
            for c in rng:
                sl = slice(c * CHUNK, (c + 1) * CHUNK)
                src = sls[chunks[c]]
                cum = cum_all[sl]
                e_rem = jnp.exp(cum[CHUNK - 1:CHUNK, :] - cum)
                k = k_ref[0, src, :]
                bb = bb_ref[0, src, :]
                a_s = stack(an_ref[0, src, :] * e_ex_all[sl]).astype(BF16)
                r_t = r_ref[0, src, :] * e_in_all[sl]
                v["a_s"].append(a_s)
                v["r_t"].append(r_t)
                v["v_sb"].append(stack(v_ref[0, src, :]).astype(BF16))
                v["bk_t"].append(
                    jnp.concatenate([stack(bb * e_rem).T, stack(k * e_rem).T], axis=1).astype(BF16))
                v["e_tot"].append(e_in_all[sl][CHUNK - 1:CHUNK, :])
                b_t = (bb * e_neg_all[sl]).astype(BF16)
                k_t = (k * e_neg_all[sl]).astype(BF16)
                lhs = jnp.concatenate([a_s, stack(r_t).astype(BF16)], axis=0)
                rhs = jnp.concatenate([b_t, b_t, k_t, k_t], axis=0)
                v["gram"].append(_dot_nt(lhs, rhs))

        def masks():
            gram = v.pop("gram")
            p_0 = [jnp.where(strict, gm[:two, :two], 0.0) for gm in gram]
            v["a_34"] = [jnp.concatenate([jnp.where(incl, gm[two:, :two], 0.0),
                                          jnp.where(incl, gm[two:, two:], 0.0)], axis=1).astype(BF16)
                         for gm in gram]
            v["a2v"] = [_dot(jnp.where(strict, gram[c][:two, two:], 0.0).astype(BF16),
                             v["v_sb"][c]).astype(BF16) for c in rng]
            v["t_m"] = [eye + p for p in p_0]
            v["p_b"] = [p.astype(BF16) for p in p_0]

        def square():
            v["p_b"] = [_dot(pb, pb).astype(BF16) for pb in v["p_b"]]

        def doubling():
            p_b = v["p_b"]
            out = [_dot(jnp.concatenate([v["t_m"][c].astype(BF16), p_b[c]], axis=0), p_b[c]) for c in rng]
            v["t_m"] = [v["t_m"][c] + out[c][:two] for c in rng]
            v["p_b"] = [o[two:].astype(BF16) for o in out]

        def last_doubling():
            v["t_m"] = [v["t_m"][c] + _dot(v["t_m"][c].astype(BF16), v["p_b"][c]) for c in rng]

        def solve():
            v["x_b"] = [_dot(v["t_m"][c].astype(BF16),
                             jnp.concatenate([v["a_s"][c], v["a2v"][c]], axis=1)).astype(BF16)
                        for c in rng]

        def handover():
            for c in rng:
                v_sb = v["v_sb"][c]
                rhs = jnp.concatenate([v["x_b"][c], jnp.concatenate([jnp.zeros_like(v_sb), v_sb], axis=1)],
                                      axis=0)
                out = _dot(jnp.concatenate([v["a_34"][c], v["bk_t"][c]], axis=0), rhs)
                dst = chunks[c]
                q_m = (v["r_t"][c] + unstack(out[:two, :LANES])).astype(BF16)
                qg_s[dst] = jnp.concatenate([q_m, out[two:, :LANES].astype(BF16)], axis=0)
                yl_s[dst] = unstack(out[:two, LANES:])
                nm_s[dst] = out[two:, LANES:]
                e_col = jnp.broadcast_to(v["e_tot"][c], (8, LANES)).T[:, 0:1]
                ec_s[dst] = jnp.broadcast_to(e_col, (two, LANES))

        n_rounds = CHUNK.bit_length() - 2
        return [front, masks, square] + [doubling] * (n_rounds - 1) + [last_doubling, solve, handover]

    ep = {}

    def epilogue_mean():
        s_ref[...] = carry["state"]
        y = y_ref[...]
        ep["d"] = y - _split_dot(y, hmean_ref[...])

    def epilogue_var():
        ep["var"] = _split_dot(ep["d"] * ep["d"], hmean_ref[...])

    def epilogue_store():
        yn = ep["d"] * lax.rsqrt(ep["var"] + GN_EPS) * lnw_ref[...] + lnb_ref[...]
        z_ref[0] = ((yn + bg_s[0]) * bg_s[1]).astype(z_ref.dtype)

    def stash_bonus_gate():
        rk = r_ref[0].astype(F32) * k_ref[0].astype(F32) * rk_ref[...]
        bg_s[0] = _split_dot(rk, hsum_ref[...]) * v_ref[0].astype(F32)
        bg_s[1] = g_ref[0].astype(F32)

    size = -(-n_chunks // n_groups)
    groups = [group_stages(list(range(lo, min(lo + size, n_chunks)))) for lo in range(0, n_chunks, size)]
    n_stages = len(groups[0])
    slots = n_stages - 1 + GROUP_LAG * (len(groups) - 1)
    per_slot = -(-n_chunks // max(slots - 3, 1))
    for t in range(slots):
        tail_steps(per_slot)
        if t == slots - 3:
            tail_steps(n_chunks)
            epilogue_mean()
        if t == slots - 2:
            epilogue_var()
        for gi, stages in enumerate(groups):
            k_stage = t - GROUP_LAG * gi
            if 0 <= k_stage < n_stages - 1:
                stages[k_stage]()
    epilogue_store()
    stash_bonus_gate()
    for stages in groups:
        stages[-1]()


def _chunk_cumsum(x):
    t = lax.broadcasted_iota(jnp.int32, x.shape, 0) % CHUNK
    sh = 1
    while sh < CHUNK:
        x = x + jnp.where(t >= sh, pltpu.roll(x, sh, axis=0), 0.0)
        sh *= 2
    return x


def _scan_layer(r, lw, k, v, an, bb, g, r_k, lnx_w, lnx_b, *, side=(), n_chunks=16, n_groups=2):
    B, S, D = r.shape
    n_chunks = min(n_chunks, S // CHUNK)
    L = n_chunks * CHUNK
    nb, nj = S // L, D // LANES
    n_blocks = B * nj * nb

    def coords(f):
        return f // (nj * nb), f % nb, (f // nb) % nj

    cur = lambda s: coords(jnp.minimum(s, n_blocks - 1))
    prev = lambda s: coords(jnp.maximum(s - 1, 0))
    blk_in = pl.BlockSpec((1, L, LANES), cur)
    vec_in = pl.BlockSpec((1, LANES), lambda s: (0, cur(s)[2]))
    vec_out = pl.BlockSpec((1, LANES), lambda s: (0, prev(s)[2]))
    const = lambda n: pl.BlockSpec((n, n), lambda s: (0, 0))
    two = 2 * CHUNK
    side_in, side_out, side_shapes = _side_cast_specs(side, n_blocks + 1, lambda s: s)
    return pl.pallas_call(
        functools.partial(_scan_kernel, n_chunks=n_chunks, n_groups=n_groups, blocks_per_seq=nb),
        out_shape=[jax.ShapeDtypeStruct((B, S, D), BF16)] + side_shapes,
        grid=(n_blocks + 1,),
        in_specs=[blk_in] * 7 + [vec_in, vec_out, vec_out] + [const(LANES), const(LANES)] + side_in,
        out_specs=[pl.BlockSpec((1, L, LANES), prev)] + side_out,
        scratch_shapes=[
            pltpu.VMEM((LANES, LANES), F32),
            pltpu.VMEM((L, LANES), F32),
            pltpu.VMEM((n_chunks, CHUNK + two, LANES), BF16),
            pltpu.VMEM((n_chunks, CHUNK, LANES), F32),
            pltpu.VMEM((n_chunks, two, LANES), F32),
            pltpu.VMEM((n_chunks, two, LANES), F32),
            pltpu.VMEM((2, L, LANES), F32),
        ],
        compiler_params=_params(("arbitrary",)),
        name="rwkv_scan",
    )(r, lw, k, v, an, bb, g, r_k.reshape(1, D), lnx_w.reshape(1, D), lnx_b.reshape(1, D),
      _head_sum_matrix(LANES, 1.0 / HEAD), _head_sum_matrix(LANES), *[w for w, _ in side])


def _oproj_kernel(z_ref, w_ref, h_ref, o_ref):
    o_ref[...] = h_ref[...] + _dot(z_ref[...], w_ref[...])


def _oproj_layer(z, w, h, *, tm=1024, tn=1024):
    T, D = h.shape
    tm, tn = min(tm, T), min(tn, D)
    return pl.pallas_call(
        _oproj_kernel,
        out_shape=jax.ShapeDtypeStruct((T, D), F32),
        grid=(T // tm, D // tn),
        in_specs=[
            pl.BlockSpec((tm, D), lambda i, n: (i, 0)),
            pl.BlockSpec((D, tn), lambda i, n: (0, n)),
            pl.BlockSpec((tm, tn), lambda i, n: (i, n)),
        ],
        out_specs=pl.BlockSpec((tm, tn), lambda i, n: (i, n)),
        compiler_params=_params(("parallel", "parallel")),
        name="rwkv_oproj",
    )(z, w, h)


def kernel(x, norm1_g, norm2_g, final_g, pool_w, pool_b, pool_scale, rw_mu, rw_r, rw_k, rw_v, rw_o, rw_w0, rw_w_la, rw_w_lb, rw_a0, rw_a_la, rw_a_lb, rw_g_la, rw_g_lb, rw_k_k, rw_k_a, rw_r_k, rw_lnx_w, rw_lnx_b, ffn_w1, ffn_w3, ffn_w2):
    B, S, D = x.shape
    T = B * S
    bf = lambda w: w.astype(BF16)

    h, w1, w3, w2 = _pool_layer(x, norm1_g[0], bf(pool_w[0]), pool_b[0].reshape(-1), pool_scale[0],
                                side=((ffn_w1, 0), (ffn_w3, 0), (ffn_w2, 0)))
    h, w_r, w_k, w_v, w_o = _ffn_layer(
        h.reshape(T, D), norm2_g[0], w1, w3, w2, final_g,
        final_norm=False, side=((rw_r, 0), (rw_k, 0), (rw_v, 0), (rw_o, 0)))

    r, lw, k, v, an, bb, g = _proj_layer(
        h, S, norm1_g[1], rw_mu[0], w_r, w_k, w_v, rw_w0[0],
        bf(rw_w_la[0]), bf(rw_w_lb[0]), rw_a0[0], bf(rw_a_la[0]), bf(rw_a_lb[0]),
        bf(rw_g_la[0]), bf(rw_g_lb[0]), rw_k_k[0], rw_k_a[0])
    s3 = lambda t: t.reshape(B, S, D)
    z, w1, w3, w2 = _scan_layer(s3(r), s3(lw), s3(k), s3(v), s3(an), s3(bb), s3(g),
                                rw_r_k[0].reshape(-1), rw_lnx_w[0], rw_lnx_b[0],
                                side=((ffn_w1, 1), (ffn_w3, 1), (ffn_w2, 1)))
    h = _oproj_layer(z.reshape(T, D), w_o, h)
    h, = _ffn_layer(h, norm2_g[1], w1, w3, w2, final_g, final_norm=True)
    return h.reshape(B, S, D)
```

```python
import functools

import jax
import jax.numpy as jnp
from jax import lax
from jax.experimental import pallas as pl
from jax.experimental.pallas import tpu as pltpu

F32 = jnp.float32
BF16 = jnp.bfloat16

RMS_EPS = 1e-6
GN_EPS = 64e-5
L2_EPS = 1e-12
DECAY_SCALE = 0.6065306597126334
POOL_WINDOWS = (2, 4, 8, 16)
POOL_HALO = 16
HEAD = 64
LANES = 128
MXU_WIDTH = 256
CHUNK = 64
GROUP_LAG = 1
VMEM_LIMIT = 56 * 1024 * 1024


def _rms(x, g):
    return x * lax.rsqrt(jnp.mean(x * x, axis=-1, keepdims=True) + RMS_EPS) * g


def _dot(a, b):
    return jnp.dot(a, b, preferred_element_type=F32)


def _dot_nt(a, b):
    return lax.dot_general(a, b, (((1,), (1,)), ((), ())), preferred_element_type=F32)


def _split_dot(x, w, pieces=2):
    acc = None
    rem = x
    for _ in range(pieces):
        p = rem.astype(BF16)
        rem = rem - p.astype(F32)
        t = _dot(p, w)
        acc = t if acc is None else acc + t
    return acc


def _params(sem):
    return pltpu.CompilerParams(dimension_semantics=sem, vmem_limit_bytes=VMEM_LIMIT)


def _pool_kernel(x_ref, xprev_ref, g_ref, w_ref, b_ref, sc_ref, *rest, ts, gdim):
    n_side = (len(rest) - 1) // 2
    side_in, o_ref, side_out = rest[:n_side], rest[n_side], rest[n_side + 1:]
    _side_casts(side_in, side_out)
    i = pl.program_id(1)
    g = g_ref[...]
    xc = x_ref[0]
    hn = _rms(xc, g)
    hp = _rms(xprev_ref[0], g)
    hp = jnp.where(i == 0, 0.0, hp)
    ext = jnp.concatenate([hp, hn], axis=0)
    rows = ts + POOL_HALO
    tau = lax.broadcasted_iota(jnp.int32, (ts, gdim), 0) + i * ts
    for gi, win in enumerate(POOL_WINDOWS):
        lo = gi * gdim
        e = ext[:, lo:lo + gdim]
        acc = e
        step = 1
        while step < win:
            acc = acc + pltpu.roll(acc, step, axis=0)
            step *= 2
        wsum = acc[POOL_HALO:rows]
        cnt = jnp.minimum(tau + 1, win).astype(F32)
        pooled = wsum / cnt - hn[:, lo:lo + gdim]
        mixed = _dot(pooled.astype(BF16), w_ref[gi]) + b_ref[:, lo:lo + gdim]
        o_ref[0, :, lo:lo + gdim] = xc[:, lo:lo + gdim] + mixed * sc_ref[:, lo:lo + gdim]


def _pool_layer(x, g, w_bf, b, scale, *, side=(), ts=512):
    B, S, D = x.shape
    G, C, _ = w_bf.shape
    ts = min(ts, S)
    halo_blocks = ts // POOL_HALO
    n_s = S // ts
    side_in, side_out, side_shapes = _side_cast_specs(side, B * n_s, lambda b, i: b * n_s + i)
    return pl.pallas_call(
        functools.partial(_pool_kernel, ts=ts, gdim=C),
        out_shape=[jax.ShapeDtypeStruct((B, S, D), F32)] + side_shapes,
        grid=(B, n_s),
        in_specs=[
            pl.BlockSpec((1, ts, D), lambda b, i: (b, i, 0)),
            pl.BlockSpec((1, POOL_HALO, D), lambda b, i: (b, jnp.maximum(i * halo_blocks - 1, 0), 0)),
            pl.BlockSpec((1, D), lambda b, i: (0, 0)),
            pl.BlockSpec((G, C, C), lambda b, i: (0, 0, 0)),
            pl.BlockSpec((1, D), lambda b, i: (0, 0)),
            pl.BlockSpec((1, D), lambda b, i: (0, 0)),
        ] + side_in,
        out_specs=[pl.BlockSpec((1, ts, D), lambda b, i: (b, i, 0))] + side_out,
        compiler_params=_params(("arbitrary", "arbitrary")),
        name="pool_mixer",
    )(x, x, g.reshape(1, D), w_bf, b.reshape(1, D), scale.reshape(1, D), *[w for w, _ in side])


def _ffn_kernel(h_ref, g_ref, w1_ref, w3_ref, w2_ref, fg_ref, *rest, n_i, n_f, final_norm, tn):
    n_side = (len(rest) - 4) // 2
    side_in, o_hbm, side_out = rest[:n_side], rest[n_side], rest[n_side + 1:2 * n_side + 1]
    hn_ref, acc_ref, sem = rest[2 * n_side + 1:]
    _side_casts(side_in, side_out)
    i = pl.program_id(0)
    f = pl.program_id(1)
    tm, d = acc_ref.shape
    slabs = [slice(n, n + tn) for n in range(0, d, tn)]

    def out_copy(k, tile):
        return pltpu.make_async_copy(acc_ref.at[:, slabs[k]], o_hbm.at[pl.ds(tile * tm, tm), slabs[k]],
                                     sem.at[k])

    def wait_out(tile):
        for k in range(len(slabs)):
            out_copy(k, tile).wait()

    @pl.when(f == 0)
    def _():
        hn_ref[...] = _rms(h_ref[...], g_ref[...]).astype(BF16)

        @pl.when(i > 0)
        def _():
            wait_out(i - 1)
        acc_ref[...] = h_ref[...]

    def body(is_last):
        hn = hn_ref[...]
        a = _dot(hn, w1_ref[...])
        b = _dot(hn, w3_ref[...])
        act = (a * jax.nn.sigmoid(a) * b).astype(BF16)
        for k, cs in enumerate(slabs):
            acc_ref[:, cs] += _dot(act, w2_ref[:, cs])
            if is_last and not final_norm:
                out_copy(k, i).start()
        if is_last and final_norm:
            x = acc_ref[...]
            scale = lax.rsqrt(jnp.mean(x * x, axis=-1, keepdims=True) + RMS_EPS)
            for k, cs in enumerate(slabs):
                acc_ref[:, cs] = x[:, cs] * scale * fg_ref[:, cs]
                out_copy(k, i).start()

    @pl.when(f < n_f - 1)
    def _():
        body(False)

    @pl.when(f == n_f - 1)
    def _():
        body(True)

        @pl.when(i == n_i - 1)
        def _():
            wait_out(i)


def _ffn_layer(h, g, w1, w3, w2, final_g, *, final_norm, side=(), tm=1024, tf=512, tn=512):
    T, D = h.shape
    F = w1.shape[1]
    tm, tf, tn = min(tm, T), min(tf, F), min(tn, D)
    while F % tf:
        tf -= LANES
    assert T % tm == 0 and D % tn == 0 and tf > 0
    n_f = F // tf
    n_i = T // tm
    side_in, side_out, side_shapes = _side_cast_specs(side, n_i * n_f, lambda i, f: i * n_f + f)
    outs = pl.pallas_call(
        functools.partial(_ffn_kernel, n_i=n_i, n_f=n_f, final_norm=final_norm, tn=tn),
        out_shape=[jax.ShapeDtypeStruct((T, D), F32)] + side_shapes,
        grid=(n_i, n_f),
        in_specs=[
            pl.BlockSpec((tm, D), lambda i, f: (i, 0)),
            pl.BlockSpec((1, D), lambda i, f: (0, 0)),
            pl.BlockSpec((D, tf), lambda i, f: (0, f)),
            pl.BlockSpec((D, tf), lambda i, f: (0, f)),
            pl.BlockSpec((tf, D), lambda i, f: (f, 0)),
            pl.BlockSpec((1, D), lambda i, f: (0, 0)),
        ] + side_in,
        out_specs=[pl.BlockSpec(memory_space=pl.ANY)] + side_out,
        scratch_shapes=[pltpu.VMEM((tm, D), BF16), pltpu.VMEM((tm, D), F32),
                        pltpu.SemaphoreType.DMA((D // tn,))],
        compiler_params=_params(("arbitrary", "arbitrary")),
        name="ffn_final" if final_norm else "ffn",
    )(h, g.reshape(1, D), w1, w3, w2, final_g.reshape(1, D), *[w for w, _ in side])
    return outs


def _side_casts(side_in, side_out):
    for w_ref, o_ref in zip(side_in, side_out):
        o_ref[...] = w_ref[0].astype(o_ref.dtype)


def _side_cast_specs(side, n_steps, flat_step):
    in_specs, out_specs, shapes = [], [], []
    for w, layer in side:
        _, R, C = w.shape
        tr = next(t for t in range(16, R + 1, 16) if R % t == 0 and R // t <= n_steps)
        last = R // tr - 1

        def blk(*idx, last=last):
            return jnp.minimum(flat_step(*idx), last)

        in_specs.append(pl.BlockSpec((1, tr, C), lambda *idx, blk=blk, layer=layer: (layer, blk(*idx), 0)))
        out_specs.append(pl.BlockSpec((tr, C), lambda *idx, blk=blk: (blk(*idx), 0)))
        shapes.append(jax.ShapeDtypeStruct((R, C), BF16))
    return in_specs, out_specs, shapes


def _proj_kernel(h_ref, hprev_ref, g_ref, mu_ref, wla_ref, ala_ref, gla_ref,
                 wr_ref, wk_ref, wv_ref, wlb_ref, alb_ref, glb_ref,
                 w0_ref, a0_ref, kk_ref, ka_ref, hsum_ref,
                 r_out, lw_out, k_out, v_out, an_out, bb_out, g_out,
                 xr_s, xk_s, xv_s, tw_s, ta_s, tg_s, *, tm, seq, n_tiles, n_steps):
    i = pl.program_id(0)
    n = pl.program_id(1)
    rq = tm // n_steps
    slab = hsum_ref.shape[0]
    slabs = [slice(lo, lo + slab) for lo in range(0, r_out.shape[1], slab)]

    def prep():
        slot = i % 2
        r0 = pl.multiple_of(n * rq, rq)
        dst = pl.ds(r0, rq)
        g = g_ref[...]
        hn = _rms(h_ref[dst, :], g)
        inside = h_ref[pl.ds(pl.multiple_of(jnp.maximum(r0 - 8, 0), 8), 8), :]
        prev8 = jnp.where(n == 0, hprev_ref[...], inside)
        hp = _rms(prev8[7:8, :], g)
        hp = jnp.where((n == 0) & ((i * tm) % seq == 0), 0.0, hp)
        row = lax.broadcasted_iota(jnp.int32, hn.shape, 0)
        shifted = jnp.where(row == 0, hp, pltpu.roll(hn, 1, axis=0))
        xx = shifted - hn
        xr_s[slot, dst, :] = (hn + xx * mu_ref[0:1, :]).astype(BF16)
        xk_s[slot, dst, :] = (hn + xx * mu_ref[2:3, :]).astype(BF16)
        xv_s[slot, dst, :] = (hn + xx * mu_ref[3:4, :]).astype(BF16)
        xw = (hn + xx * mu_ref[1:2, :]).astype(BF16)
        tw_s[slot, dst, :] = jnp.tanh(_dot(xw, wla_ref[...])).astype(BF16)
        xa = (hn + xx * mu_ref[4:5, :]).astype(BF16)
        ta_s[slot, dst, :] = _dot(xa, ala_ref[...]).astype(BF16)
        xg = (hn + xx * mu_ref[5:6, :]).astype(BF16)
        tg_s[slot, dst, :] = jax.nn.sigmoid(_dot(xg, gla_ref[...])).astype(BF16)

    def project(between):
        slot = (i + 1) % 2
        half = tm
        units = [(pl.ds(r0, half), cs) for r0 in range(0, tm, half) for cs in slabs]

        def matmuls(rows, cs):
            r = _dot(xr_s[slot, rows, :], wr_ref[:, cs])
            k = _dot(xk_s[slot, rows, :], wk_ref[:, cs])
            v = _dot(xv_s[slot, rows, :], wv_ref[:, cs])
            wl = _dot(tw_s[slot, rows, :], wlb_ref[:, cs])
            al = _dot(ta_s[slot, rows, :], alb_ref[:, cs])
            gg = _dot(tg_s[slot, rows, :], glb_ref[:, cs])
            return r, k, v, wl, al, gg

        def tail(rows, cs, r, k, v, wl, al, gg):
            lw_out[rows, cs] = -DECAY_SCALE * jax.nn.sigmoid(wl + w0_ref[:, cs])
            a_sig = jax.nn.sigmoid(al + a0_ref[:, cs])
            kk = k * kk_ref[:, cs]
            ss = _split_dot(kk * kk, hsum_ref[...])
            kk = kk * jnp.minimum(lax.rsqrt(ss), 1.0 / L2_EPS)
            r_out[rows, cs] = r.astype(r_out.dtype)
            k_out[rows, cs] = (k * (1.0 + (a_sig - 1.0) * ka_ref[:, cs])).astype(k_out.dtype)
            v_out[rows, cs] = v.astype(v_out.dtype)
            an_out[rows, cs] = (-kk).astype(an_out.dtype)
            bb_out[rows, cs] = (kk * a_sig).astype(bb_out.dtype)
            g_out[rows, cs] = gg.astype(g_out.dtype)

        pending = None
        for u, (rows, cs) in enumerate(units):
            res = matmuls(rows, cs)
            if pending is not None:
                tail(*pending)
            pending = (rows, cs) + res
            if u == 0:
                between()
        tail(*pending)

    @pl.when(i == 0)
    def _():
        prep()

    @pl.when((i > 0) & (i < n_tiles))
    def _():
        project(prep)

    @pl.when(i == n_tiles)
    def _():
        project(lambda: None)


def _head_sum_matrix(n, value=1.0):
    hi = lax.broadcasted_iota(jnp.int32, (n, n), 0) // HEAD
    hj = lax.broadcasted_iota(jnp.int32, (n, n), 1) // HEAD
    return jnp.where(hi == hj, value, 0.0).astype(BF16)


def _proj_layer(h, seq, g, mu, w_r, w_k, w_v, w0, w_la, w_lb, a0, a_la, a_lb, g_la, g_lb, k_k, k_a,
                *, tm=512, tn=512):
    T, D = h.shape
    tm, tn = min(tm, T), min(tn, D)
    slab = min(MXU_WIDTH, tn)
    n_tiles, n_steps = T // tm, D // tn
    dl, al, gl = w_la.shape[1], a_la.shape[1], g_la.shape[1]
    row = lambda x: x.reshape(1, D)
    full = lambda shp: pl.BlockSpec(shp, lambda i, n: (0, 0))
    col = lambda rows: pl.BlockSpec((rows, tn), lambda i, n: (0, n))
    out_spec = pl.BlockSpec((tm, tn), lambda i, n: (jnp.maximum(i - 1, 0), jnp.where(i == 0, 0, n)))
    prev_blocks = tm // 8
    tile = lambda i: jnp.minimum(i, n_tiles - 1)
    out_dtypes = [BF16, F32, BF16, BF16, BF16, BF16, BF16]
    outs = pl.pallas_call(
        functools.partial(_proj_kernel, tm=tm, seq=seq, n_tiles=n_tiles, n_steps=n_steps),
        out_shape=[jax.ShapeDtypeStruct((T, D), dt) for dt in out_dtypes],
        grid=(n_tiles + 1, n_steps),
        in_specs=[
            pl.BlockSpec((tm, D), lambda i, n: (tile(i), 0)),
            pl.BlockSpec((8, D), lambda i, n: (jnp.maximum(tile(i) * prev_blocks - 1, 0), 0)),
            full((1, D)), full((8, D)), full((D, dl)), full((D, al)), full((D, gl)),
            col(D), col(D), col(D), col(dl), col(al), col(gl),
            col(1), col(1), col(1), col(1),
            full((slab, slab)),
        ],
        out_specs=[out_spec] * 7,
        scratch_shapes=[pltpu.VMEM((2, tm, D), BF16)] * 3
        + [pltpu.VMEM((2, tm, dl), BF16), pltpu.VMEM((2, tm, al), BF16), pltpu.VMEM((2, tm, gl), BF16)],
        compiler_params=_params(("arbitrary", "arbitrary")),
        name="rwkv_proj",
    )(h, h, row(g), jnp.pad(mu, ((0, 8 - mu.shape[0]), (0, 0))), w_la, a_la, g_la,
      w_r, w_k, w_v, w_lb, a_lb, g_lb, row(w0), row(a0), row(k_k), row(k_a),
      _head_sum_matrix(slab))
    return outs


def _scan_kernel(r_ref, lw_ref, k_ref, v_ref, an_ref, bb_ref, g_ref, rk_ref, lnw_ref, lnb_ref,
                 hmean_ref, hsum_ref, *rest, n_chunks, n_groups, blocks_per_seq):
    n_side = (len(rest) - 8) // 2
    side_in, z_ref, side_out = rest[:n_side], rest[n_side], rest[n_side + 1:2 * n_side + 1]
    s_ref, y_ref, qg_s, yl_s, nm_s, ec_s, bg_s = rest[2 * n_side + 1:]
    _side_casts(side_in, side_out)
    step = pl.program_id(0)

    @pl.when(step == 0)
    def _():
        for ref in (s_ref, qg_s, yl_s, nm_s, ec_s, bg_s):
            ref[...] = jnp.zeros_like(ref)

    two = 2 * CHUNK
    lane_head0 = lax.broadcasted_iota(jnp.int32, (CHUNK, LANES), 1) < HEAD
    ri = lax.broadcasted_iota(jnp.int32, (two, two), 0)
    ci = lax.broadcasted_iota(jnp.int32, (two, two), 1)
    same = (ri >= CHUNK) == (ci >= CHUNK)
    strict = same & (ci < ri)
    incl = same & (ci <= ri)
    eye = jnp.where(ri == ci, 1.0, 0.0).astype(F32)

    def stack(x):
        return jnp.concatenate([jnp.where(lane_head0, x, 0.0), jnp.where(lane_head0, 0.0, x)], axis=0)

    def unstack(x):
        return x[:CHUNK] + x[CHUNK:]

    sls = [pl.ds(c * CHUNK, CHUNK) for c in range(n_chunks)]

    prev_first = (step + blocks_per_seq - 1) % blocks_per_seq == 0
    carry = {"state": jnp.where(prev_first, 0.0, s_ref[...]), "next": 0}

    def tail_steps(count):
        for c in range(carry["next"], min(carry["next"] + count, n_chunks)):
            state = carry["state"]
            res = _dot(qg_s[c], state.astype(BF16))
            y_ref[sls[c], :] = res[:CHUNK] + yl_s[c]
            carry["state"] = state * ec_s[c] + res[CHUNK:] + nm_s[c]
            carry["next"] = c + 1

    def group_stages(chunks):
        n = len(chunks)
        rng = range(n)
        v = {}

        def front():
            rows = pl.ds(chunks[0] * CHUNK, n * CHUNK)
            lw_all = lw_ref[0, rows, :]
            cum_all = _chunk_cumsum(lw_all)
            e_in_all = jnp.exp(cum_all)
            e_ex_all = jnp.exp(cum_all - lw_all)
            e_neg_all = jnp.exp(-cum_all)
            v["a_s"], v["r_t"], v["v_sb"], v["bk_t"], v["e_tot"], v["gram"] = [], [], [], [], [], []
            for c in rng:
                sl = slice(c * CHUNK, (c + 1) * CHUNK)
                src = sls[chunks[c]]
                cum = cum_all[sl]
                e_rem = jnp.exp(cum[CHUNK - 1:CHUNK, :] - cum)
                k = k_ref[0, src, :]
                bb = bb_ref[0, src, :]
                a_s = stack(an_ref[0, src, :] * e_ex_all[sl]).astype(BF16)
                r_t = r_ref[0, src, :] * e_in_all[sl]
                v["a_s"].append(a_s)
                v["r_t"].append(r_t)
                v["v_sb"].append(stack(v_ref[0, src, :]).astype(BF16))
                v["bk_t"].append(
                    jnp.concatenate([stack(bb * e_rem).T, stack(k * e_rem).T], axis=1).astype(BF16))
                v["e_tot"].append(e_in_all[sl][CHUNK - 1:CHUNK, :])
                b_t = (bb * e_neg_all[sl]).astype(BF16)
                k_t = (k * e_neg_all[sl]).astype(BF16)
                lhs = jnp.concatenate([a_s, stack(r_t).astype(BF16)], axis=0)
                rhs = jnp.concatenate([b_t, b_t, k_t, k_t], axis=0)
                v["gram"].append(_dot_nt(lhs, rhs))

        def masks():
            gram = v.pop("gram")
            p_0 = [jnp.where(strict, gm[:two, :two], 0.0) for gm in gram]
            v["a_34"] = [jnp.concatenate([jnp.where(incl, gm[two:, :two], 0.0),
                                          jnp.where(incl, gm[two:, two:], 0.0)], axis=1).astype(BF16)
                         for gm in gram]
            v["a2v"] = [_dot(jnp.where(strict, gram[c][:two, two:], 0.0).astype(BF16),
                             v["v_sb"][c]).astype(BF16) for c in rng]
            v["t_m"] = [eye + p for p in p_0]
            v["p_b"] = [p.astype(BF16) for p in p_0]

        def square():
            v["p_b"] = [_dot(pb, pb).astype(BF16) for pb in v["p_b"]]

        def doubling():
            p_b = v["p_b"]
            out = [_dot(jnp.concatenate([v["t_m"][c].astype(BF16), p_b[c]], axis=0), p_b[c]) for c in rng]
            v["t_m"] = [v["t_m"][c] + out[c][:two] for c in rng]
            v["p_b"] = [o[two:].astype(BF16) for o in out]

        def last_doubling():
            v["t_m"] = [v["t_m"][c] + _dot(v["t_m"][c].astype(BF16), v["p_b"][c]) for c in rng]

        def solve():
            v["x_b"] = [_dot(v["t_m"][c].astype(BF16),
                             jnp.concatenate([v["a_s"][c], v["a2v"][c]], axis=1)).astype(BF16)
                        for c in rng]

        def handover():
            for c in rng:
                v_sb = v["v_sb"][c]
                rhs = jnp.concatenate([v["x_b"][c], jnp.concatenate([jnp.zeros_like(v_sb), v_sb], axis=1)],
                                      axis=0)
                out = _dot(jnp.concatenate([v["a_34"][c], v["bk_t"][c]], axis=0), rhs)
                dst = chunks[c]
                q_m = (v["r_t"][c] + unstack(out[:two, :LANES])).astype(BF16)
                qg_s[dst] = jnp.concatenate([q_m, out[two:, :LANES].astype(BF16)], axis=0)
                yl_s[dst] = unstack(out[:two, LANES:])
                nm_s[dst] = out[two:, LANES:]
                e_col = jnp.broadcast_to(v["e_tot"][c], (8, LANES)).T[:, 0:1]
                ec_s[dst] = jnp.broadcast_to(e_col, (two, LANES))

        n_rounds = CHUNK.bit_length() - 2
        return [front, masks, square] + [doubling] * (n_rounds - 1) + [last_doubling, solve, handover]

    ep = {}

    def epilogue_mean():
        s_ref[...] = carry["state"]
        y = y_ref[...]
        ep["d"] = y - _split_dot(y, hmean_ref[...])

    def epilogue_var():
        ep["var"] = _split_dot(ep["d"] * ep["d"], hmean_ref[...])

    def epilogue_store():
        yn = ep["d"] * lax.rsqrt(ep["var"] + GN_EPS) * lnw_ref[...] + lnb_ref[...]
        z_ref[0] = ((yn + bg_s[0]) * bg_s[1]).astype(z_ref.dtype)

    def stash_bonus_gate():
        rk = r_ref[0].astype(F32) * k_ref[0].astype(F32) * rk_ref[...]
        bg_s[0] = _split_dot(rk, hsum_ref[...]) * v_ref[0].astype(F32)
        bg_s[1] = g_ref[0].astype(F32)

    size = -(-n_chunks // n_groups)
    groups = [group_stages(list(range(lo, min(lo + size, n_chunks)))) for lo in range(0, n_chunks, size)]
    n_stages = len(groups[0])
    slots = n_stages - 1 + GROUP_LAG * (len(groups) - 1)
    per_slot = -(-n_chunks // max(slots - 3, 1))
    for t in range(slots):
        tail_steps(per_slot)
        if t == slots - 3:
            tail_steps(n_chunks)
            epilogue_mean()
        if t == slots - 2:
            epilogue_var()
        for gi, stages in enumerate(groups):
            k_stage = t - GROUP_LAG * gi
            if 0 <= k_stage < n_stages - 1:
                stages[k_stage]()
    epilogue_store()
    stash_bonus_gate()
    for stages in groups:
        stages[-1]()


def _chunk_cumsum(x):
    t = lax.broadcasted_iota(jnp.int32, x.shape, 0) % CHUNK
    sh = 1
    while sh < CHUNK:
        x = x + jnp.where(t >= sh, pltpu.roll(x, sh, axis=0), 0.0)
        sh *= 2
    return x


def _scan_layer(r, lw, k, v, an, bb, g, r_k, lnx_w, lnx_b, *, side=(), n_chunks=16, n_groups=3):
    B, S, D = r.shape
    n_chunks = min(n_chunks, S // CHUNK)
    L = n_chunks * CHUNK
    nb, nj = S // L, D // LANES
    n_blocks = B * nj * nb

    def coords(f):
        return f // (nj * nb), f % nb, (f // nb) % nj

    cur = lambda s: coords(jnp.minimum(s, n_blocks - 1))
    prev = lambda s: coords(jnp.maximum(s - 1, 0))
    blk_in = pl.BlockSpec((1, L, LANES), cur)
    vec_in = pl.BlockSpec((1, LANES), lambda s: (0, cur(s)[2]))
    vec_out = pl.BlockSpec((1, LANES), lambda s: (0, prev(s)[2]))
    const = lambda n: pl.BlockSpec((n, n), lambda s: (0, 0))
    two = 2 * CHUNK
    side_in, side_out, side_shapes = _side_cast_specs(side, n_blocks + 1, lambda s: s)
    return pl.pallas_call(
        functools.partial(_scan_kernel, n_chunks=n_chunks, n_groups=n_groups, blocks_per_seq=nb),
        out_shape=[jax.ShapeDtypeStruct((B, S, D), BF16)] + side_shapes,
        grid=(n_blocks + 1,),
        in_specs=[blk_in] * 7 + [vec_in, vec_out, vec_out] + [const(LANES), const(LANES)] + side_in,
        out_specs=[pl.BlockSpec((1, L, LANES), prev)] + side_out,
        scratch_shapes=[
            pltpu.VMEM((LANES, LANES), F32),
            pltpu.VMEM((L, LANES), F32),
            pltpu.VMEM((n_chunks, CHUNK + two, LANES), BF16),
            pltpu.VMEM((n_chunks, CHUNK, LANES), F32),
            pltpu.VMEM((n_chunks, two, LANES), F32),
            pltpu.VMEM((n_chunks, two, LANES), F32),
            pltpu.VMEM((2, L, LANES), F32),
        ],
        compiler_params=_params(("arbitrary",)),
        name="rwkv_scan",
    )(r, lw, k, v, an, bb, g, r_k.reshape(1, D), lnx_w.reshape(1, D), lnx_b.reshape(1, D),
      _head_sum_matrix(LANES, 1.0 / HEAD), _head_sum_matrix(LANES), *[w for w, _ in side])


def _oproj_kernel(z_ref, w_ref, h_ref, o_ref):
    o_ref[...] = h_ref[...] + _dot(z_ref[...], w_ref[...])


def _oproj_layer(z, w, h, *, tm=1024, tn=1024):
    T, D = h.shape
    tm, tn = min(tm, T), min(tn, D)
    return pl.pallas_call(
        _oproj_kernel,
        out_shape=jax.ShapeDtypeStruct((T, D), F32),
        grid=(T // tm, D // tn),
        in_specs=[
            pl.BlockSpec((tm, D), lambda i, n: (i, 0)),
            pl.BlockSpec((D, tn), lambda i, n: (0, n)),
            pl.BlockSpec((tm, tn), lambda i, n: (i, n)),
        ],
        out_specs=pl.BlockSpec((tm, tn), lambda i, n: (i, n)),
        compiler_params=_params(("parallel", "parallel")),
        name="rwkv_oproj",
    )(z, w, h)


def kernel(x, norm1_g, norm2_g, final_g, pool_w, pool_b, pool_scale, rw_mu, rw_r, rw_k, rw_v, rw_o, rw_w0, rw_w_la, rw_w_lb, rw_a0, rw_a_la, rw_a_lb, rw_g_la, rw_g_lb, rw_k_k, rw_k_a, rw_r_k, rw_lnx_w, rw_lnx_b, ffn_w1, ffn_w3, ffn_w2):
    B, S, D = x.shape
    T = B * S
    bf = lambda w: w.astype(BF16)

    h, w1, w3, w2 = _pool_layer(x, norm1_g[0], bf(pool_w[0]), pool_b[0].reshape(-1), pool_scale[0],
                                side=((ffn_w1, 0), (ffn_w3, 0), (ffn_w2, 0)))
    h, w_r, w_k, w_v, w_o = _ffn_layer(
        h.reshape(T, D), norm2_g[0], w1, w3, w2, final_g,
        final_norm=False, side=((rw_r, 0), (rw_k, 0), (rw_v, 0), (rw_o, 0)))

    r, lw, k, v, an, bb, g = _proj_layer(
        h, S, norm1_g[1], rw_mu[0], w_r, w_k, w_v, rw_w0[0],
        bf(rw_w_la[0]), bf(rw_w_lb[0]), rw_a0[0], bf(rw_a_la[0]), bf(rw_a_lb[0]),
        bf(rw_g_la[0]), bf(rw_g_lb[0]), rw_k_k[0], rw_k_a[0])
    s3 = lambda t: t.reshape(B, S, D)
    z, w1, w3, w2 = _scan_layer(s3(r), s3(lw), s3(k), s3(v), s3(an), s3(bb), s3(g),
                                rw_r_k[0].reshape(-1), rw_lnx_w[0], rw_lnx_b[0],
                                side=((ffn_w1, 1), (ffn_w3, 1), (ffn_w2, 1)))
    h = _oproj_layer(z.reshape(T, D), w_o, h)
    h, = _ffn_layer(h, norm2_g[1], w1, w3, w2, final_g, final_norm=True)
    return h.reshape(B, S, D)
```

```python
import functools

import jax
import jax.numpy as jnp
from jax import lax
from jax.experimental import pallas as pl
from jax.experimental.pallas import tpu as pltpu

F32 = jnp.float32
BF16 = jnp.bfloat16

RMS_EPS = 1e-6
GN_EPS = 64e-5
L2_EPS = 1e-12
DECAY_SCALE = 0.6065306597126334
POOL_WINDOWS = (2, 4, 8, 16)
POOL_HALO = 16
HEAD = 64
LANES = 128
MXU_WIDTH = 256
CHUNK = 64
GROUP_LAG = 1
VMEM_LIMIT = 56 * 1024 * 1024


def _rms(x, g):
    return x * lax.rsqrt(jnp.mean(x * x, axis=-1, keepdims=True) + RMS_EPS) * g


def _dot(a, b):
    return jnp.dot(a, b, preferred_element_type=F32)


def _dot_nt(a, b):
    return lax.dot_general(a, b, (((1,), (1,)), ((), ())), preferred_element_type=F32)


def _split_dot(x, w, pieces=2):
    acc = None
    rem = x
    for _ in range(pieces):
        p = rem.astype(BF16)
        rem = rem - p.astype(F32)
        t = _dot(p, w)
        acc = t if acc is None else acc + t
    return acc


def _params(sem):
    return pltpu.CompilerParams(dimension_semantics=sem, vmem_limit_bytes=VMEM_LIMIT)


def _pool_kernel(x_ref, xprev_ref, g_ref, w_ref, b_ref, sc_ref, *rest, ts, gdim):
    n_side = (len(rest) - 1) // 2
    side_in, o_ref, side_out = rest[:n_side], rest[n_side], rest[n_side + 1:]
    _side_casts(side_in, side_out)
    i = pl.program_id(1)
    g = g_ref[...]
    xc = x_ref[0]
    hn = _rms(xc, g)
    hp = _rms(xprev_ref[0], g)
    hp = jnp.where(i == 0, 0.0, hp)
    ext = jnp.concatenate([hp, hn], axis=0)
    rows = ts + POOL_HALO
    tau = lax.broadcasted_iota(jnp.int32, (ts, gdim), 0) + i * ts
    for gi, win in enumerate(POOL_WINDOWS):
        lo = gi * gdim
        e = ext[:, lo:lo + gdim]
        acc = e
        step = 1
        while step < win:
            acc = acc + pltpu.roll(acc, step, axis=0)
            step *= 2
        wsum = acc[POOL_HALO:rows]
        cnt = jnp.minimum(tau + 1, win).astype(F32)
        pooled = wsum / cnt - hn[:, lo:lo + gdim]
        mixed = _dot(pooled.astype(BF16), w_ref[gi]) + b_ref[:, lo:lo + gdim]
        o_ref[0, :, lo:lo + gdim] = xc[:, lo:lo + gdim] + mixed * sc_ref[:, lo:lo + gdim]


def _pool_layer(x, g, w_bf, b, scale, *, side=(), ts=512):
    B, S, D = x.shape
    G, C, _ = w_bf.shape
    ts = min(ts, S)
    halo_blocks = ts // POOL_HALO
    n_s = S // ts
    side_in, side_out, side_shapes = _side_cast_specs(side, B * n_s, lambda b, i: b * n_s + i)
    return pl.pallas_call(
        functools.partial(_pool_kernel, ts=ts, gdim=C),
        out_shape=[jax.ShapeDtypeStruct((B, S, D), F32)] + side_shapes,
        grid=(B, n_s),
        in_specs=[
            pl.BlockSpec((1, ts, D), lambda b, i: (b, i, 0)),
            pl.BlockSpec((1, POOL_HALO, D), lambda b, i: (b, jnp.maximum(i * halo_blocks - 1, 0), 0)),
            pl.BlockSpec((1, D), lambda b, i: (0, 0)),
            pl.BlockSpec((G, C, C), lambda b, i: (0, 0, 0)),
            pl.BlockSpec((1, D), lambda b, i: (0, 0)),
            pl.BlockSpec((1, D), lambda b, i: (0, 0)),
        ] + side_in,
        out_specs=[pl.BlockSpec((1, ts, D), lambda b, i: (b, i, 0))] + side_out,
        compiler_params=_params(("arbitrary", "arbitrary")),
        name="pool_mixer",
    )(x, x, g.reshape(1, D), w_bf, b.reshape(1, D), scale.reshape(1, D), *[w for w, _ in side])


def _ffn_kernel(h_ref, g_ref, w1_ref, w3_ref, w2_ref, fg_ref, *rest, n_i, n_f, final_norm, tn):
    n_side = (len(rest) - 4) // 2
    side_in, o_hbm, side_out = rest[:n_side], rest[n_side], rest[n_side + 1:2 * n_side + 1]
    hn_ref, acc_ref, sem = rest[2 * n_side + 1:]
    _side_casts(side_in, side_out)
    i = pl.program_id(0)
    f = pl.program_id(1)
    tm, d = acc_ref.shape
    slabs = [slice(n, n + tn) for n in range(0, d, tn)]

    def out_copy(k, tile):
        return pltpu.make_async_copy(acc_ref.at[:, slabs[k]], o_hbm.at[pl.ds(tile * tm, tm), slabs[k]],
                                     sem.at[k])

    def wait_out(tile):
        for k in range(len(slabs)):
            out_copy(k, tile).wait()

    @pl.when(f == 0)
    def _():
        hn_ref[...] = _rms(h_ref[...], g_ref[...]).astype(BF16)

        @pl.when(i > 0)
        def _():
            wait_out(i - 1)
        acc_ref[...] = h_ref[...]

    def body(is_last):
        hn = hn_ref[...]
        a = _dot(hn, w1_ref[...])
        b = _dot(hn, w3_ref[...])
        act = (a * jax.nn.sigmoid(a) * b).astype(BF16)
        for k, cs in enumerate(slabs):
            acc_ref[:, cs] += _dot(act, w2_ref[:, cs])
            if is_last and not final_norm:
                out_copy(k, i).start()
        if is_last and final_norm:
            x = acc_ref[...]
            scale = lax.rsqrt(jnp.mean(x * x, axis=-1, keepdims=True) + RMS_EPS)
            for k, cs in enumerate(slabs):
                acc_ref[:, cs] = x[:, cs] * scale * fg_ref[:, cs]
                out_copy(k, i).start()

    @pl.when(f < n_f - 1)
    def _():
        body(False)

    @pl.when(f == n_f - 1)
    def _():
        body(True)

        @pl.when(i == n_i - 1)
        def _():
            wait_out(i)


def _ffn_layer(h, g, w1, w3, w2, final_g, *, final_norm, side=(), tm=1024, tf=512, tn=512):
    T, D = h.shape
    F = w1.shape[1]
    tm, tf, tn = min(tm, T), min(tf, F), min(tn, D)
    while F % tf:
        tf -= LANES
    assert T % tm == 0 and D % tn == 0 and tf > 0
    n_f = F // tf
    n_i = T // tm
    side_in, side_out, side_shapes = _side_cast_specs(side, n_i * n_f, lambda i, f: i * n_f + f)
    outs = pl.pallas_call(
        functools.partial(_ffn_kernel, n_i=n_i, n_f=n_f, final_norm=final_norm, tn=tn),
        out_shape=[jax.ShapeDtypeStruct((T, D), F32)] + side_shapes,
        grid=(n_i, n_f),
        in_specs=[
            pl.BlockSpec((tm, D), lambda i, f: (i, 0)),
            pl.BlockSpec((1, D), lambda i, f: (0, 0)),
            pl.BlockSpec((D, tf), lambda i, f: (0, f)),
            pl.BlockSpec((D, tf), lambda i, f: (0, f)),
            pl.BlockSpec((tf, D), lambda i, f: (f, 0)),
            pl.BlockSpec((1, D), lambda i, f: (0, 0)),
        ] + side_in,
        out_specs=[pl.BlockSpec(memory_space=pl.ANY)] + side_out,
        scratch_shapes=[pltpu.VMEM((tm, D), BF16), pltpu.VMEM((tm, D), F32),
                        pltpu.SemaphoreType.DMA((D // tn,))],
        compiler_params=_params(("arbitrary", "arbitrary")),
        name="ffn_final" if final_norm else "ffn",
    )(h, g.reshape(1, D), w1, w3, w2, final_g.reshape(1, D), *[w for w, _ in side])
    return outs


def _side_casts(side_in, side_out):
    for w_ref, o_ref in zip(side_in, side_out):
        o_ref[...] = w_ref[0].astype(o_ref.dtype)


def _side_cast_specs(side, n_steps, flat_step):
    in_specs, out_specs, shapes = [], [], []
    for w, layer in side:
        _, R, C = w.shape
        tr = next(t for t in range(16, R + 1, 16) if R % t == 0 and R // t <= n_steps)
        last = R // tr - 1

        def blk(*idx, last=last):
            return jnp.minimum(flat_step(*idx), last)

        in_specs.append(pl.BlockSpec((1, tr, C), lambda *idx, blk=blk, layer=layer: (layer, blk(*idx), 0)))
        out_specs.append(pl.BlockSpec((tr, C), lambda *idx, blk=blk: (blk(*idx), 0)))
        shapes.append(jax.ShapeDtypeStruct((R, C), BF16))
    return in_specs, out_specs, shapes


def _proj_kernel(h_ref, hprev_ref, g_ref, mu_ref, wla_ref, ala_ref, gla_ref,
                 wr_ref, wk_ref, wv_ref, wlb_ref, alb_ref, glb_ref,
                 w0_ref, a0_ref, kk_ref, ka_ref, hsum_ref,
                 r_out, lw_out, k_out, v_out, an_out, bb_out, g_out,
                 xr_s, xk_s, xv_s, tw_s, ta_s, tg_s, *, tm, seq, n_tiles, n_steps):
    i = pl.program_id(0)
    n = pl.program_id(1)
    rq = tm // n_steps
    slab = hsum_ref.shape[0]
    slabs = [slice(lo, lo + slab) for lo in range(0, r_out.shape[1], slab)]

    def prep():
        slot = i % 2
        r0 = pl.multiple_of(n * rq, rq)
        dst = pl.ds(r0, rq)
        g = g_ref[...]
        hn = _rms(h_ref[dst, :], g)
        inside = h_ref[pl.ds(pl.multiple_of(jnp.maximum(r0 - 8, 0), 8), 8), :]
        prev8 = jnp.where(n == 0, hprev_ref[...], inside)
        hp = _rms(prev8[7:8, :], g)
        hp = jnp.where((n == 0) & ((i * tm) % seq == 0), 0.0, hp)
        row = lax.broadcasted_iota(jnp.int32, hn.shape, 0)
        shifted = jnp.where(row == 0, hp, pltpu.roll(hn, 1, axis=0))
        xx = shifted - hn
        xr_s[slot, dst, :] = (hn + xx * mu_ref[0:1, :]).astype(BF16)
        xk_s[slot, dst, :] = (hn + xx * mu_ref[2:3, :]).astype(BF16)
        xv_s[slot, dst, :] = (hn + xx * mu_ref[3:4, :]).astype(BF16)
        xw = (hn + xx * mu_ref[1:2, :]).astype(BF16)
        tw_s[slot, dst, :] = jnp.tanh(_dot(xw, wla_ref[...])).astype(BF16)
        xa = (hn + xx * mu_ref[4:5, :]).astype(BF16)
        ta_s[slot, dst, :] = _dot(xa, ala_ref[...]).astype(BF16)
        xg = (hn + xx * mu_ref[5:6, :]).astype(BF16)
        tg_s[slot, dst, :] = jax.nn.sigmoid(_dot(xg, gla_ref[...])).astype(BF16)

    def project(between):
        slot = (i + 1) % 2
        half = tm
        units = [(pl.ds(r0, half), cs) for r0 in range(0, tm, half) for cs in slabs]

        def matmuls(rows, cs):
            r = _dot(xr_s[slot, rows, :], wr_ref[:, cs])
            k = _dot(xk_s[slot, rows, :], wk_ref[:, cs])
            v = _dot(xv_s[slot, rows, :], wv_ref[:, cs])
            wl = _dot(tw_s[slot, rows, :], wlb_ref[:, cs])
            al = _dot(ta_s[slot, rows, :], alb_ref[:, cs])
            gg = _dot(tg_s[slot, rows, :], glb_ref[:, cs])
            return r, k, v, wl, al, gg

        def tail(rows, cs, r, k, v, wl, al, gg):
            lw_out[rows, cs] = -DECAY_SCALE * jax.nn.sigmoid(wl + w0_ref[:, cs])
            a_sig = jax.nn.sigmoid(al + a0_ref[:, cs])
            kk = k * kk_ref[:, cs]
            ss = _split_dot(kk * kk, hsum_ref[...])
            kk = kk * jnp.minimum(lax.rsqrt(ss), 1.0 / L2_EPS)
            r_out[rows, cs] = r.astype(r_out.dtype)
            k_out[rows, cs] = (k * (1.0 + (a_sig - 1.0) * ka_ref[:, cs])).astype(k_out.dtype)
            v_out[rows, cs] = v.astype(v_out.dtype)
            an_out[rows, cs] = (-kk).astype(an_out.dtype)
            bb_out[rows, cs] = (kk * a_sig).astype(bb_out.dtype)
            g_out[rows, cs] = gg.astype(g_out.dtype)

        pending = None
        for u, (rows, cs) in enumerate(units):
            res = matmuls(rows, cs)
            if pending is not None:
                tail(*pending)
            pending = (rows, cs) + res
            if u == 0:
                between()
        tail(*pending)

    @pl.when(i == 0)
    def _():
        prep()

    @pl.when((i > 0) & (i < n_tiles))
    def _():
        project(prep)

    @pl.when(i == n_tiles)
    def _():
        project(lambda: None)


def _head_sum_matrix(n, value=1.0):
    hi = lax.broadcasted_iota(jnp.int32, (n, n), 0) // HEAD
    hj = lax.broadcasted_iota(jnp.int32, (n, n), 1) // HEAD
    return jnp.where(hi == hj, value, 0.0).astype(BF16)


def _proj_layer(h, seq, g, mu, w_r, w_k, w_v, w0, w_la, w_lb, a0, a_la, a_lb, g_la, g_lb, k_k, k_a,
                *, tm=512, tn=512):
    T, D = h.shape
    tm, tn = min(tm, T), min(tn, D)
    slab = min(MXU_WIDTH, tn)
    n_tiles, n_steps = T // tm, D // tn
    dl, al, gl = w_la.shape[1], a_la.shape[1], g_la.shape[1]
    row = lambda x: x.reshape(1, D)
    full = lambda shp: pl.BlockSpec(shp, lambda i, n: (0, 0))
    col = lambda rows: pl.BlockSpec((rows, tn), lambda i, n: (0, n))
    out_spec = pl.BlockSpec((tm, tn), lambda i, n: (jnp.maximum(i - 1, 0), jnp.where(i == 0, 0, n)))
    prev_blocks = tm // 8
    tile = lambda i: jnp.minimum(i, n_tiles - 1)
    out_dtypes = [BF16, F32, BF16, BF16, BF16, BF16, BF16]
    outs = pl.pallas_call(
        functools.partial(_proj_kernel, tm=tm, seq=seq, n_tiles=n_tiles, n_steps=n_steps),
        out_shape=[jax.ShapeDtypeStruct((T, D), dt) for dt in out_dtypes],
        grid=(n_tiles + 1, n_steps),
        in_specs=[
            pl.BlockSpec((tm, D), lambda i, n: (tile(i), 0)),
            pl.BlockSpec((8, D), lambda i, n: (jnp.maximum(tile(i) * prev_blocks - 1, 0), 0)),
            full((1, D)), full((8, D)), full((D, dl)), full((D, al)), full((D, gl)),
            col(D), col(D), col(D), col(dl), col(al), col(gl),
            col(1), col(1), col(1), col(1),
            full((slab, slab)),
        ],
        out_specs=[out_spec] * 7,
        scratch_shapes=[pltpu.VMEM((2, tm, D), BF16)] * 3
        + [pltpu.VMEM((2, tm, dl), BF16), pltpu.VMEM((2, tm, al), BF16), pltpu.VMEM((2, tm, gl), BF16)],
        compiler_params=_params(("arbitrary", "arbitrary")),
        name="rwkv_proj",
    )(h, h, row(g), jnp.pad(mu, ((0, 8 - mu.shape[0]), (0, 0))), w_la, a_la, g_la,
      w_r, w_k, w_v, w_lb, a_lb, g_lb, row(w0), row(a0), row(k_k), row(k_a),
      _head_sum_matrix(slab))
    return outs


def _scan_kernel(r_ref, lw_ref, k_ref, v_ref, an_ref, bb_ref, g_ref, rk_ref, lnw_ref, lnb_ref,
                 hmean_ref, hsum_ref, *rest, n_chunks, n_groups, blocks_per_seq):
    n_side = (len(rest) - 8) // 2
    side_in, z_ref, side_out = rest[:n_side], rest[n_side], rest[n_side + 1:2 * n_side + 1]
    s_ref, y_ref, qg_s, yl_s, nm_s, ec_s, bg_s = rest[2 * n_side + 1:]
    _side_casts(side_in, side_out)
    step = pl.program_id(0)

    @pl.when(step == 0)
    def _():
        for ref in (s_ref, qg_s, yl_s, nm_s, ec_s, bg_s):
            ref[...] = jnp.zeros_like(ref)

    two = 2 * CHUNK
    lane_head0 = lax.broadcasted_iota(jnp.int32, (CHUNK, LANES), 1) < HEAD
    ri = lax.broadcasted_iota(jnp.int32, (two, two), 0)
    ci = lax.broadcasted_iota(jnp.int32, (two, two), 1)
    same = (ri >= CHUNK) == (ci >= CHUNK)
    strict = same & (ci < ri)
    incl = same & (ci <= ri)
    eye = jnp.where(ri == ci, 1.0, 0.0).astype(F32)

    def stack(x):
        return jnp.concatenate([jnp.where(lane_head0, x, 0.0), jnp.where(lane_head0, 0.0, x)], axis=0)

    def unstack(x):
        return x[:CHUNK] + x[CHUNK:]

    sls = [pl.ds(c * CHUNK, CHUNK) for c in range(n_chunks)]

    prev_first = (step + blocks_per_seq - 1) % blocks_per_seq == 0
    carry = {"state": jnp.where(prev_first, 0.0, s_ref[...]), "next": 0}

    def tail_steps(count):
        for c in range(carry["next"], min(carry["next"] + count, n_chunks)):
            state = carry["state"]
            res = _dot(qg_s[c], state.astype(BF16))
            y_ref[sls[c], :] = res[:CHUNK] + yl_s[c]
            carry["state"] = state * ec_s[c] + res[CHUNK:] + nm_s[c]
            carry["next"] = c + 1

    def group_stages(chunks):
        n = len(chunks)
        rng = range(n)
        v = {}

        def front():
            rows = pl.ds(chunks[0] * CHUNK, n * CHUNK)
            lw_all = lw_ref[0, rows, :]
            cum_all = _chunk_cumsum(lw_all)
            e_in_all = jnp.exp(cum_all)
            e_ex_all = jnp.exp(cum_all - lw_all)
            e_neg_all = jnp.exp(-cum_all)
            v["a_s"], v["r_t"], v["v_sb"], v["bk_t"], v["e_tot"], v["gram"] = [], [], [], [], [], []
            for c in rng:
                sl = slice(c * CHUNK, (c + 1) * CHUNK)
                src = sls[chunks[c]]
                cum = cum_all[sl]
                e_rem = jnp.exp(cum[CHUNK - 1:CHUNK, :] - cum)
                k = k_ref[0, src, :]
                bb = bb_ref[0, src, :]
                a_s = stack(an_ref[0, src, :] * e_ex_all[sl]).astype(BF16)
                r_t = r_ref[0, src, :] * e_in_all[sl]
                v["a_s"].append(a_s)
                v["r_t"].append(r_t)
                v["v_sb"].append(stack(v_ref[0, src, :]).astype(BF16))
                v["bk_t"].append(
                    jnp.concatenate([stack(bb * e_rem).T, stack(k * e_rem).T], axis=1).astype(BF16))
                v["e_tot"].append(e_in_all[sl][CHUNK - 1:CHUNK, :])
                b_t = (bb * e_neg_all[sl]).astype(BF16)
                k_t = (k * e_neg_all[sl]).astype(BF16)
                lhs = jnp.concatenate([a_s, stack(r_t).astype(BF16)], axis=0)
                rhs = jnp.concatenate([b_t, b_t, k_t, k_t], axis=0)
                v["gram"].append(_dot_nt(lhs, rhs))

        def masks():
            gram = v.pop("gram")
            p_0 = [jnp.where(strict, gm[:two, :two], 0.0) for gm in gram]
            v["a_34"] = [jnp.concatenate([jnp.where(incl, gm[two:, :two], 0.0),
                                          jnp.where(incl, gm[two:, two:], 0.0)], axis=1).astype(BF16)
                         for gm in gram]
            v["a2v"] = [_dot(jnp.where(strict, gram[c][:two, two:], 0.0).astype(BF16),
                             v["v_sb"][c]).astype(BF16) for c in rng]
            v["t_m"] = [eye + p for p in p_0]
            v["p_b"] = [p.astype(BF16) for p in p_0]

        def square():
            v["p_b"] = [_dot(pb, pb).astype(BF16) for pb in v["p_b"]]

        def doubling():
            p_b = v["p_b"]
            out = [_dot(jnp.concatenate([v["t_m"][c].astype(BF16), p_b[c]], axis=0), p_b[c]) for c in rng]
            v["t_m"] = [v["t_m"][c] + out[c][:two] for c in rng]
            v["p_b"] = [o[two:].astype(BF16) for o in out]

        def last_doubling():
            v["t_m"] = [v["t_m"][c] + _dot(v["t_m"][c].astype(BF16), v["p_b"][c]) for c in rng]

        def solve():
            v["x_b"] = [_dot(v["t_m"][c].astype(BF16),
                             jnp.concatenate([v["a_s"][c], v["a2v"][c]], axis=1)).astype(BF16)
                        for c in rng]

        def handover():
            for c in rng:
                v_sb = v["v_sb"][c]
                rhs = jnp.concatenate([v["x_b"][c], jnp.concatenate([jnp.zeros_like(v_sb), v_sb], axis=1)],
                                      axis=0)
                out = _dot(jnp.concatenate([v["a_34"][c], v["bk_t"][c]], axis=0), rhs)
                dst = chunks[c]
                q_m = (v["r_t"][c] + unstack(out[:two, :LANES])).astype(BF16)
                qg_s[dst] = jnp.concatenate([q_m, out[two:, :LANES].astype(BF16)], axis=0)
                yl_s[dst] = unstack(out[:two, LANES:])
                nm_s[dst] = out[two:, LANES:]
                e_col = jnp.broadcast_to(v["e_tot"][c], (8, LANES)).T[:, 0:1]
                ec_s[dst] = jnp.broadcast_to(e_col, (two, LANES))

        n_rounds = CHUNK.bit_length() - 2
        return [front, masks, square] + [doubling] * (n_rounds - 1) + [last_doubling, solve, handover]

    ep = {}

    def epilogue_mean():
        s_ref[...] = carry["state"]
        y = y_ref[...]
        ep["d"] = y - _split_dot(y, hmean_ref[...])

    def epilogue_var():
        ep["var"] = _split_dot(ep["d"] * ep["d"], hmean_ref[...])

    def epilogue_store():
        yn = ep["d"] * lax.rsqrt(ep["var"] + GN_EPS) * lnw_ref[...] + lnb_ref[...]
        z_ref[0] = ((yn + bg_s[0]) * bg_s[1]).astype(z_ref.dtype)

    def stash_bonus_gate():
        rk = r_ref[0].astype(F32) * k_ref[0].astype(F32) * rk_ref[...]
        bg_s[0] = _split_dot(rk, hsum_ref[...]) * v_ref[0].astype(F32)
        bg_s[1] = g_ref[0].astype(F32)

    size = -(-n_chunks // n_groups)
    groups = [group_stages(list(range(lo, min(lo + size, n_chunks)))) for lo in range(0, n_chunks, size)]
    n_stages = len(groups[0])
    slots = n_stages - 1 + GROUP_LAG * (len(groups) - 1)
    per_slot = -(-n_chunks // max(slots - 3, 1))
    for t in range(slots):
        tail_steps(per_slot)
        if t == slots - 3:
            tail_steps(n_chunks)
            epilogue_mean()
        if t == slots - 2:
            epilogue_var()
        for gi, stages in enumerate(groups):
            k_stage = t - GROUP_LAG * gi
            if 0 <= k_stage < n_stages - 1:
                stages[k_stage]()
    epilogue_store()
    stash_bonus_gate()
    for stages in groups:
        stages[-1]()


def _chunk_cumsum(x):
    t = lax.broadcasted_iota(jnp.int32, x.shape, 0) % CHUNK
    sh = 1
    while sh < CHUNK:
        x = x + jnp.where(t >= sh, pltpu.roll(x, sh, axis=0), 0.0)
        sh *= 2
    return x


def _scan_layer(r, lw, k, v, an, bb, g, r_k, lnx_w, lnx_b, *, side=(), n_chunks=16, n_groups=3):
    B, S, D = r.shape
    n_chunks = min(n_chunks, S // CHUNK)
    L = n_chunks * CHUNK
    nb, nj = S // L, D // LANES
    n_blocks = B * nj * nb

    def coords(f):
        return f // (nj * nb), f % nb, (f // nb) % nj

    cur = lambda s: coords(jnp.minimum(s, n_blocks - 1))
    prev = lambda s: coords(jnp.maximum(s - 1, 0))
    blk_in = pl.BlockSpec((1, L, LANES), cur)
    vec_in = pl.BlockSpec((1, LANES), lambda s: (0, cur(s)[2]))
    vec_out = pl.BlockSpec((1, LANES), lambda s: (0, prev(s)[2]))
    const = lambda n: pl.BlockSpec((n, n), lambda s: (0, 0))
    two = 2 * CHUNK
    side_in, side_out, side_shapes = _side_cast_specs(side, n_blocks + 1, lambda s: s)
    return pl.pallas_call(
        functools.partial(_scan_kernel, n_chunks=n_chunks, n_groups=n_groups, blocks_per_seq=nb),
        out_shape=[jax.ShapeDtypeStruct((B, S, D), BF16)] + side_shapes,
        grid=(n_blocks + 1,),
        in_specs=[blk_in] * 7 + [vec_in, vec_out, vec_out] + [const(LANES), const(LANES)] + side_in,
        out_specs=[pl.BlockSpec((1, L, LANES), prev)] + side_out,
        scratch_shapes=[
            pltpu.VMEM((LANES, LANES), F32),
            pltpu.VMEM((L, LANES), F32),
            pltpu.VMEM((n_chunks, CHUNK + two, LANES), BF16),
            pltpu.VMEM((n_chunks, CHUNK, LANES), F32),
            pltpu.VMEM((n_chunks, two, LANES), F32),
            pltpu.VMEM((n_chunks, two, LANES), F32),
            pltpu.VMEM((2, L, LANES), F32),
        ],
        compiler_params=_params(("arbitrary",)),
        name="rwkv_scan",
    )(r, lw, k, v, an, bb, g, r_k.reshape(1, D), lnx_w.reshape(1, D), lnx_b.reshape(1, D),
      _head_sum_matrix(LANES, 1.0 / HEAD), _head_sum_matrix(LANES), *[w for w, _ in side])


def _oproj_kernel(z_ref, w_ref, h_ref, o_ref):
    o_ref[...] = h_ref[...] + _dot(z_ref[...], w_ref[...])


def _oproj_layer(z, w, h, *, tm=512, tn=2048):
    T, D = h.shape
    tm, tn = min(tm, T), min(tn, D)
    return pl.pallas_call(
        _oproj_kernel,
        out_shape=jax.ShapeDtypeStruct((T, D), F32),
        grid=(T // tm, D // tn),
        in_specs=[
            pl.BlockSpec((tm, D), lambda i, n: (i, 0)),
            pl.BlockSpec((D, tn), lambda i, n: (0, n)),
            pl.BlockSpec((tm, tn), lambda i, n: (i, n)),
        ],
        out_specs=pl.BlockSpec((tm, tn), lambda i, n: (i, n)),
        compiler_params=_params(("parallel", "parallel")),
        name="rwkv_oproj",
    )(z, w, h)


def kernel(x, norm1_g, norm2_g, final_g, pool_w, pool_b, pool_scale, rw_mu, rw_r, rw_k, rw_v, rw_o, rw_w0, rw_w_la, rw_w_lb, rw_a0, rw_a_la, rw_a_lb, rw_g_la, rw_g_lb, rw_k_k, rw_k_a, rw_r_k, rw_lnx_w, rw_lnx_b, ffn_w1, ffn_w3, ffn_w2):
    B, S, D = x.shape
    T = B * S
    bf = lambda w: w.astype(BF16)

    h, w1, w3, w2 = _pool_layer(x, norm1_g[0], bf(pool_w[0]), pool_b[0].reshape(-1), pool_scale[0],
                                side=((ffn_w1, 0), (ffn_w3, 0), (ffn_w2, 0)))
    h, w_r, w_k, w_v, w_o = _ffn_layer(
        h.reshape(T, D), norm2_g[0], w1, w3, w2, final_g,
        final_norm=False, side=((rw_r, 0), (rw_k, 0), (rw_v, 0), (rw_o, 0)))

    r, lw, k, v, an, bb, g = _proj_layer(
        h, S, norm1_g[1], rw_mu[0], w_r, w_k, w_v, rw_w0[0],
        bf(rw_w_la[0]), bf(rw_w_lb[0]), rw_a0[0], bf(rw_a_la[0]), bf(rw_a_lb[0]),
        bf(rw_g_la[0]), bf(rw_g_lb[0]), rw_k_k[0], rw_k_a[0])
    s3 = lambda t: t.reshape(B, S, D)
    z, w1, w3, w2 = _scan_layer(s3(r), s3(lw), s3(k), s3(v), s3(an), s3(bb), s3(g),
                                rw_r_k[0].reshape(-1), rw_lnx_w[0], rw_lnx_b[0],
                                side=((ffn_w1, 1), (ffn_w3, 1), (ffn_w2, 1)))
    h = _oproj_layer(z.reshape(T, D), w_o, h)
    h, = _ffn_layer(h, norm2_g[1], w1, w3, w2, final_g, final_norm=True)
    return h.reshape(B, S, D)
```

```python
import functools

import jax
import jax.numpy as jnp
from jax import lax
from jax.experimental import pallas as pl
from jax.experimental.pallas import tpu as pltpu

F32 = jnp.float32
BF16 = jnp.bfloat16

RMS_EPS = 1e-6
GN_EPS = 64e-5
L2_EPS = 1e-12
DECAY_SCALE = 0.6065306597126334
POOL_WINDOWS = (2, 4, 8, 16)
POOL_HALO = 16
HEAD = 64
LANES = 128
MXU_WIDTH = 256
CHUNK = 64
GROUP_LAG = 1
VMEM_LIMIT = 56 * 1024 * 1024


def _rms(x, g):
    return x * lax.rsqrt(jnp.mean(x * x, axis=-1, keepdims=True) + RMS_EPS) * g


def _dot(a, b):
    return jnp.dot(a, b, preferred_element_type=F32)


def _dot_nt(a, b):
    return lax.dot_general(a, b, (((1,), (1,)), ((), ())), preferred_element_type=F32)


def _split_dot(x, w, pieces=2):
    acc = None
    rem = x
    for _ in range(pieces):
        p = rem.astype(BF16)
        rem = rem - p.astype(F32)
        t = _dot(p, w)
        acc = t if acc is None else acc + t
    return acc


def _params(sem):
    return pltpu.CompilerParams(dimension_semantics=sem, vmem_limit_bytes=VMEM_LIMIT)


def _pool_kernel(x_ref, xprev_ref, g_ref, w_ref, b_ref, sc_ref, *rest, ts, gdim):
    n_side = (len(rest) - 1) // 2
    side_in, o_ref, side_out = rest[:n_side], rest[n_side], rest[n_side + 1:]
    _side_casts(side_in, side_out)
    i = pl.program_id(1)
    g = g_ref[...]
    xc = x_ref[0]
    hn = _rms(xc, g)
    hp = _rms(xprev_ref[0], g)
    hp = jnp.where(i == 0, 0.0, hp)
    ext = jnp.concatenate([hp, hn], axis=0)
    rows = ts + POOL_HALO
    tau = lax.broadcasted_iota(jnp.int32, (ts, gdim), 0) + i * ts
    for gi, win in enumerate(POOL_WINDOWS):
        lo = gi * gdim
        e = ext[:, lo:lo + gdim]
        acc = e
        step = 1
        while step < win:
            acc = acc + pltpu.roll(acc, step, axis=0)
            step *= 2
        wsum = acc[POOL_HALO:rows]
        cnt = jnp.minimum(tau + 1, win).astype(F32)
        pooled = wsum / cnt - hn[:, lo:lo + gdim]
        mixed = _dot(pooled.astype(BF16), w_ref[gi]) + b_ref[:, lo:lo + gdim]
        o_ref[0, :, lo:lo + gdim] = xc[:, lo:lo + gdim] + mixed * sc_ref[:, lo:lo + gdim]


def _pool_layer(x, g, w_bf, b, scale, *, side=(), ts=512):
    B, S, D = x.shape
    G, C, _ = w_bf.shape
    ts = min(ts, S)
    halo_blocks = ts // POOL_HALO
    n_s = S // ts
    side_in, side_out, side_shapes = _side_cast_specs(side, B * n_s, lambda b, i: b * n_s + i)
    return pl.pallas_call(
        functools.partial(_pool_kernel, ts=ts, gdim=C),
        out_shape=[jax.ShapeDtypeStruct((B, S, D), F32)] + side_shapes,
        grid=(B, n_s),
        in_specs=[
            pl.BlockSpec((1, ts, D), lambda b, i: (b, i, 0)),
            pl.BlockSpec((1, POOL_HALO, D), lambda b, i: (b, jnp.maximum(i * halo_blocks - 1, 0), 0)),
            pl.BlockSpec((1, D), lambda b, i: (0, 0)),
            pl.BlockSpec((G, C, C), lambda b, i: (0, 0, 0)),
            pl.BlockSpec((1, D), lambda b, i: (0, 0)),
            pl.BlockSpec((1, D), lambda b, i: (0, 0)),
        ] + side_in,
        out_specs=[pl.BlockSpec((1, ts, D), lambda b, i: (b, i, 0))] + side_out,
        compiler_params=_params(("arbitrary", "arbitrary")),
        name="pool_mixer",
    )(x, x, g.reshape(1, D), w_bf, b.reshape(1, D), scale.reshape(1, D), *[w for w, _ in side])


def _ffn_kernel(h_ref, g_ref, w1_ref, w3_ref, w2_ref, fg_ref, *rest, n_i, n_f, final_norm, tn):
    n_side = (len(rest) - 4) // 2
    side_in, o_hbm, side_out = rest[:n_side], rest[n_side], rest[n_side + 1:2 * n_side + 1]
    hn_ref, acc_ref, sem = rest[2 * n_side + 1:]
    _side_casts(side_in, side_out)
    i = pl.program_id(0)
    f = pl.program_id(1)
    tm, d = acc_ref.shape
    slabs = [slice(n, n + tn) for n in range(0, d, tn)]

    def out_copy(k, tile):
        return pltpu.make_async_copy(acc_ref.at[:, slabs[k]], o_hbm.at[pl.ds(tile * tm, tm), slabs[k]],
                                     sem.at[k])

    def wait_out(tile):
        for k in range(len(slabs)):
            out_copy(k, tile).wait()

    @pl.when(f == 0)
    def _():
        hn_ref[...] = _rms(h_ref[...], g_ref[...]).astype(BF16)

        @pl.when(i > 0)
        def _():
            wait_out(i - 1)
        acc_ref[...] = h_ref[...]

    def body(is_last):
        hn = hn_ref[...]
        a = _dot(hn, w1_ref[...])
        b = _dot(hn, w3_ref[...])
        act = (a * jax.nn.sigmoid(a) * b).astype(BF16)
        for k, cs in enumerate(slabs):
            acc_ref[:, cs] += _dot(act, w2_ref[:, cs])
            if is_last and not final_norm:
                out_copy(k, i).start()
        if is_last and final_norm:
            x = acc_ref[...]
            scale = lax.rsqrt(jnp.mean(x * x, axis=-1, keepdims=True) + RMS_EPS)
            for k, cs in enumerate(slabs):
                acc_ref[:, cs] = x[:, cs] * scale * fg_ref[:, cs]
                out_copy(k, i).start()

    @pl.when(f < n_f - 1)
    def _():
        body(False)

    @pl.when(f == n_f - 1)
    def _():
        body(True)

        @pl.when(i == n_i - 1)
        def _():
            wait_out(i)


def _ffn_layer(h, g, w1, w3, w2, final_g, *, final_norm, side=(), tm=1024, tf=512, tn=512):
    T, D = h.shape
    F = w1.shape[1]
    tm, tf, tn = min(tm, T), min(tf, F), min(tn, D)
    while F % tf:
        tf -= LANES
    assert T % tm == 0 and D % tn == 0 and tf > 0
    n_f = F // tf
    n_i = T // tm
    side_in, side_out, side_shapes = _side_cast_specs(side, n_i * n_f, lambda i, f: i * n_f + f)
    outs = pl.pallas_call(
        functools.partial(_ffn_kernel, n_i=n_i, n_f=n_f, final_norm=final_norm, tn=tn),
        out_shape=[jax.ShapeDtypeStruct((T, D), F32)] + side_shapes,
        grid=(n_i, n_f),
        in_specs=[
            pl.BlockSpec((tm, D), lambda i, f: (i, 0)),
            pl.BlockSpec((1, D), lambda i, f: (0, 0)),
            pl.BlockSpec((D, tf), lambda i, f: (0, f)),
            pl.BlockSpec((D, tf), lambda i, f: (0, f)),
            pl.BlockSpec((tf, D), lambda i, f: (f, 0)),
            pl.BlockSpec((1, D), lambda i, f: (0, 0)),
        ] + side_in,
        out_specs=[pl.BlockSpec(memory_space=pl.ANY)] + side_out,
        scratch_shapes=[pltpu.VMEM((tm, D), BF16), pltpu.VMEM((tm, D), F32),
                        pltpu.SemaphoreType.DMA((D // tn,))],
        compiler_params=_params(("arbitrary", "arbitrary")),
        name="ffn_final" if final_norm else "ffn",
    )(h, g.reshape(1, D), w1, w3, w2, final_g.reshape(1, D), *[w for w, _ in side])
    return outs


def _side_casts(side_in, side_out):
    for w_ref, o_ref in zip(side_in, side_out):
        o_ref[...] = w_ref[0].astype(o_ref.dtype)


def _side_cast_specs(side, n_steps, flat_step):
    in_specs, out_specs, shapes = [], [], []
    for w, layer in side:
        _, R, C = w.shape
        tr = next(t for t in range(16, R + 1, 16) if R % t == 0 and R // t <= n_steps)
        last = R // tr - 1

        def blk(*idx, last=last):
            return jnp.minimum(flat_step(*idx), last)

        in_specs.append(pl.BlockSpec((1, tr, C), lambda *idx, blk=blk, layer=layer: (layer, blk(*idx), 0)))
        out_specs.append(pl.BlockSpec((tr, C), lambda *idx, blk=blk: (blk(*idx), 0)))
        shapes.append(jax.ShapeDtypeStruct((R, C), BF16))
    return in_specs, out_specs, shapes


def _proj_kernel(h_ref, hprev_ref, g_ref, mu_ref, wla_ref, ala_ref, gla_ref,
                 wr_ref, wk_ref, wv_ref, lb_ref, vec_ref, hsum_ref,
                 r_out, lw_out, k_out, v_out, an_out, bb_out, g_out,
                 xr_s, xk_s, xv_s, tw_s, ta_s, tg_s, *, tm, seq, n_tiles, n_steps):
    i = pl.program_id(0)
    n = pl.program_id(1)
    rq = tm // n_steps
    dl, da = tw_s.shape[-1], ta_s.shape[-1]
    slab = hsum_ref.shape[0]
    slabs = [slice(lo, lo + slab) for lo in range(0, r_out.shape[1], slab)]

    def prep():
        slot = i % 2
        r0 = pl.multiple_of(n * rq, rq)
        dst = pl.ds(r0, rq)
        g = g_ref[...]
        hn = _rms(h_ref[dst, :], g)
        inside = h_ref[pl.ds(pl.multiple_of(jnp.maximum(r0 - 8, 0), 8), 8), :]
        prev8 = jnp.where(n == 0, hprev_ref[...], inside)
        hp = _rms(prev8[7:8, :], g)
        hp = jnp.where((n == 0) & ((i * tm) % seq == 0), 0.0, hp)
        row = lax.broadcasted_iota(jnp.int32, hn.shape, 0)
        shifted = jnp.where(row == 0, hp, pltpu.roll(hn, 1, axis=0))
        xx = shifted - hn
        xr_s[slot, dst, :] = (hn + xx * mu_ref[0:1, :]).astype(BF16)
        xk_s[slot, dst, :] = (hn + xx * mu_ref[2:3, :]).astype(BF16)
        xv_s[slot, dst, :] = (hn + xx * mu_ref[3:4, :]).astype(BF16)
        xw = (hn + xx * mu_ref[1:2, :]).astype(BF16)
        tw_s[slot, dst, :] = jnp.tanh(_dot(xw, wla_ref[...])).astype(BF16)
        xa = (hn + xx * mu_ref[4:5, :]).astype(BF16)
        ta_s[slot, dst, :] = _dot(xa, ala_ref[...]).astype(BF16)
        xg = (hn + xx * mu_ref[5:6, :]).astype(BF16)
        tg_s[slot, dst, :] = jax.nn.sigmoid(_dot(xg, gla_ref[...])).astype(BF16)

    def project(between):
        slot = (i + 1) % 2
        half = tm
        units = [(pl.ds(r0, half), cs) for r0 in range(0, tm, half) for cs in slabs]

        def matmuls(rows, cs):
            r = _dot(xr_s[slot, rows, :], wr_ref[:, cs])
            k = _dot(xk_s[slot, rows, :], wk_ref[:, cs])
            v = _dot(xv_s[slot, rows, :], wv_ref[:, cs])
            wl = _dot(tw_s[slot, rows, :], lb_ref[0:dl, cs])
            al = _dot(ta_s[slot, rows, :], lb_ref[dl:dl + da, cs])
            gg = _dot(tg_s[slot, rows, :], lb_ref[dl + da:, cs])
            return r, k, v, wl, al, gg

        def tail(rows, cs, r, k, v, wl, al, gg):
            w0, a0, k_k, k_a = (vec_ref[j:j + 1, cs] for j in range(4))
            lw_out[rows, cs] = -DECAY_SCALE * jax.nn.sigmoid(wl + w0)
            a_sig = jax.nn.sigmoid(al + a0)
            kk = k * k_k
            ss = _split_dot(kk * kk, hsum_ref[...])
            kk = kk * jnp.minimum(lax.rsqrt(ss), 1.0 / L2_EPS)
            r_out[rows, cs] = r.astype(r_out.dtype)
            k_out[rows, cs] = (k * (1.0 + (a_sig - 1.0) * k_a)).astype(k_out.dtype)
            v_out[rows, cs] = v.astype(v_out.dtype)
            an_out[rows, cs] = (-kk).astype(an_out.dtype)
            bb_out[rows, cs] = (kk * a_sig).astype(bb_out.dtype)
            g_out[rows, cs] = gg.astype(g_out.dtype)

        pending = None
        for u, (rows, cs) in enumerate(units):
            res = matmuls(rows, cs)
            if pending is not None:
                tail(*pending)
            pending = (rows, cs) + res
            if u == 0:
                between()
        tail(*pending)

    @pl.when(i == 0)
    def _():
        prep()

    @pl.when((i > 0) & (i < n_tiles))
    def _():
        project(prep)

    @pl.when(i == n_tiles)
    def _():
        project(lambda: None)


def _head_sum_matrix(n, value=1.0):
    hi = lax.broadcasted_iota(jnp.int32, (n, n), 0) // HEAD
    hj = lax.broadcasted_iota(jnp.int32, (n, n), 1) // HEAD
    return jnp.where(hi == hj, value, 0.0).astype(BF16)


def _proj_layer(h, seq, g, mu, w_r, w_k, w_v, w0, w_la, w_lb, a0, a_la, a_lb, g_la, g_lb, k_k, k_a,
                *, tm=512, tn=512):
    T, D = h.shape
    tm, tn = min(tm, T), min(tn, D)
    slab = min(MXU_WIDTH, tn)
    n_tiles, n_steps = T // tm, D // tn
    dl, al, gl = w_la.shape[1], a_la.shape[1], g_la.shape[1]
    row = lambda x: x.reshape(1, D)
    pad8 = lambda x: jnp.pad(x, ((0, 8 - x.shape[0]), (0, 0)))
    full = lambda shp: pl.BlockSpec(shp, lambda i, n: (0, 0))
    col = lambda rows: pl.BlockSpec((rows, tn), lambda i, n: (0, n))
    out_spec = pl.BlockSpec((tm, tn), lambda i, n: (jnp.maximum(i - 1, 0), jnp.where(i == 0, 0, n)))
    prev_blocks = tm // 8
    tile = lambda i: jnp.minimum(i, n_tiles - 1)
    out_dtypes = [BF16, F32, BF16, BF16, BF16, BF16, BF16]
    outs = pl.pallas_call(
        functools.partial(_proj_kernel, tm=tm, seq=seq, n_tiles=n_tiles, n_steps=n_steps),
        out_shape=[jax.ShapeDtypeStruct((T, D), dt) for dt in out_dtypes],
        grid=(n_tiles + 1, n_steps),
        in_specs=[
            pl.BlockSpec((tm, D), lambda i, n: (tile(i), 0)),
            pl.BlockSpec((8, D), lambda i, n: (jnp.maximum(tile(i) * prev_blocks - 1, 0), 0)),
            full((1, D)), full((8, D)), full((D, dl)), full((D, al)), full((D, gl)),
            col(D), col(D), col(D), col(dl + al + gl), col(8),
            full((slab, slab)),
        ],
        out_specs=[out_spec] * 7,
        scratch_shapes=[pltpu.VMEM((2, tm, D), BF16)] * 3
        + [pltpu.VMEM((2, tm, dl), BF16), pltpu.VMEM((2, tm, al), BF16), pltpu.VMEM((2, tm, gl), BF16)],
        compiler_params=_params(("arbitrary", "arbitrary")),
        name="rwkv_proj",
    )(h, h, row(g), pad8(mu), w_la, a_la, g_la,
      w_r, w_k, w_v, jnp.concatenate([w_lb, a_lb, g_lb], axis=0), pad8(jnp.stack([w0, a0, k_k, k_a])),
      _head_sum_matrix(slab))
    return outs


def _scan_kernel(r_ref, lw_ref, k_ref, v_ref, an_ref, bb_ref, g_ref, rk_ref, lnw_ref, lnb_ref,
                 hmean_ref, hsum_ref, *rest, group_sizes, blocks_per_seq):
    n_side = (len(rest) - 8) // 2
    side_in, z_ref, side_out = rest[:n_side], rest[n_side], rest[n_side + 1:2 * n_side + 1]
    s_ref, y_ref, qg_s, yl_s, nm_s, ec_s, bg_s = rest[2 * n_side + 1:]
    _side_casts(side_in, side_out)
    step = pl.program_id(0)
    n_chunks = sum(group_sizes)

    @pl.when(step == 0)
    def _():
        for ref in (s_ref, qg_s, yl_s, nm_s, ec_s, bg_s):
            ref[...] = jnp.zeros_like(ref)

    two = 2 * CHUNK
    lane_head0 = lax.broadcasted_iota(jnp.int32, (CHUNK, LANES), 1) < HEAD
    ri = lax.broadcasted_iota(jnp.int32, (two, two), 0)
    ci = lax.broadcasted_iota(jnp.int32, (two, two), 1)
    same = (ri >= CHUNK) == (ci >= CHUNK)
    strict = same & (ci < ri)
    incl = same & (ci <= ri)
    eye = jnp.where(ri == ci, 1.0, 0.0).astype(F32)

    def stack(x):
        return jnp.concatenate([jnp.where(lane_head0, x, 0.0), jnp.where(lane_head0, 0.0, x)], axis=0)

    def unstack(x):
        return x[:CHUNK] + x[CHUNK:]

    sls = [pl.ds(c * CHUNK, CHUNK) for c in range(n_chunks)]

    prev_first = (step + blocks_per_seq - 1) % blocks_per_seq == 0
    carry = {"state": jnp.where(prev_first, 0.0, s_ref[...]), "next": 0}

    def tail_steps(count):
        for c in range(carry["next"], min(carry["next"] + count, n_chunks)):
            state = carry["state"]
            res = _dot(qg_s[c], state.astype(BF16))
            y_ref[sls[c], :] = res[:CHUNK] + yl_s[c]
            carry["state"] = state * ec_s[c] + res[CHUNK:] + nm_s[c]
            carry["next"] = c + 1

    def group_stages(chunks):
        n = len(chunks)
        rng = range(n)
        v = {}

        def front():
            rows = pl.ds(chunks[0] * CHUNK, n * CHUNK)
            lw_all = lw_ref[0, rows, :]
            cum_all = _chunk_cumsum(lw_all)
            e_in_all = jnp.exp(cum_all)
            e_ex_all = jnp.exp(cum_all - lw_all)
            e_neg_all = jnp.exp(-cum_all)
            v["a_s"], v["r_t"], v["v_sb"], v["bk_t"], v["e_tot"], v["gram"] = [], [], [], [], [], []
            for c in rng:
                sl = slice(c * CHUNK, (c + 1) * CHUNK)
                src = sls[chunks[c]]
                cum = cum_all[sl]
                e_rem = jnp.exp(cum[CHUNK - 1:CHUNK, :] - cum)
                k = k_ref[0, src, :]
                bb = bb_ref[0, src, :]
                a_s = stack(an_ref[0, src, :] * e_ex_all[sl]).astype(BF16)
                r_t = r_ref[0, src, :] * e_in_all[sl]
                v["a_s"].append(a_s)
                v["r_t"].append(r_t)
                v["v_sb"].append(stack(v_ref[0, src, :]).astype(BF16))
                v["bk_t"].append(
                    jnp.concatenate([stack(bb * e_rem).T, stack(k * e_rem).T], axis=1).astype(BF16))
                v["e_tot"].append(e_in_all[sl][CHUNK - 1:CHUNK, :])
                b_t = (bb * e_neg_all[sl]).astype(BF16)
                k_t = (k * e_neg_all[sl]).astype(BF16)
                lhs = jnp.concatenate([a_s, stack(r_t).astype(BF16)], axis=0)
                rhs = jnp.concatenate([b_t, b_t, k_t, k_t], axis=0)
                v["gram"].append(_dot_nt(lhs, rhs))

        def masks():
            gram = v.pop("gram")
            p_0 = [jnp.where(strict, gm[:two, :two], 0.0) for gm in gram]
            v["a_34"] = [jnp.concatenate([jnp.where(incl, gm[two:, :two], 0.0),
                                          jnp.where(incl, gm[two:, two:], 0.0)], axis=1).astype(BF16)
                         for gm in gram]
            v["a2v"] = [_dot(jnp.where(strict, gram[c][:two, two:], 0.0).astype(BF16),
                             v["v_sb"][c]).astype(BF16) for c in rng]
            v["t_m"] = [eye + p for p in p_0]
            v["p_b"] = [p.astype(BF16) for p in p_0]

        def square():
            v["p_b"] = [_dot(pb, pb).astype(BF16) for pb in v["p_b"]]

        def doubling():
            p_b = v["p_b"]
            out = [_dot(jnp.concatenate([v["t_m"][c].astype(BF16), p_b[c]], axis=0), p_b[c]) for c in rng]
            v["t_m"] = [v["t_m"][c] + out[c][:two] for c in rng]
            v["p_b"] = [o[two:].astype(BF16) for o in out]

        def last_doubling():
            v["t_m"] = [v["t_m"][c] + _dot(v["t_m"][c].astype(BF16), v["p_b"][c]) for c in rng]

        def solve():
            v["x_b"] = [_dot(v["t_m"][c].astype(BF16),
                             jnp.concatenate([v["a_s"][c], v["a2v"][c]], axis=1)).astype(BF16)
                        for c in rng]

        def handover():
            for c in rng:
                v_sb = v["v_sb"][c]
                rhs = jnp.concatenate([v["x_b"][c], jnp.concatenate([jnp.zeros_like(v_sb), v_sb], axis=1)],
                                      axis=0)
                out = _dot(jnp.concatenate([v["a_34"][c], v["bk_t"][c]], axis=0), rhs)
                dst = chunks[c]
                q_m = (v["r_t"][c] + unstack(out[:two, :LANES])).astype(BF16)
                qg_s[dst] = jnp.concatenate([q_m, out[two:, :LANES].astype(BF16)], axis=0)
                yl_s[dst] = unstack(out[:two, LANES:])
                nm_s[dst] = out[two:, LANES:]
                e_col = jnp.broadcast_to(v["e_tot"][c], (8, LANES)).T[:, 0:1]
                ec_s[dst] = jnp.broadcast_to(e_col, (two, LANES))

        n_rounds = CHUNK.bit_length() - 2
        return [front, masks, square] + [doubling] * (n_rounds - 1) + [last_doubling, solve, handover]

    ep = {}

    def epilogue_mean():
        s_ref[...] = carry["state"]
        y = y_ref[...]
        ep["d"] = y - _split_dot(y, hmean_ref[...])

    def epilogue_var():
        ep["var"] = _split_dot(ep["d"] * ep["d"], hmean_ref[...])

    def epilogue_store():
        yn = ep["d"] * lax.rsqrt(ep["var"] + GN_EPS) * lnw_ref[...] + lnb_ref[...]
        z_ref[0] = ((yn + bg_s[0]) * bg_s[1]).astype(z_ref.dtype)

    def stash_bonus_gate():
        rk = r_ref[0].astype(F32) * k_ref[0].astype(F32) * rk_ref[...]
        bg_s[0] = _split_dot(rk, hsum_ref[...]) * v_ref[0].astype(F32)
        bg_s[1] = g_ref[0].astype(F32)

    starts = [sum(group_sizes[:gi]) for gi in range(len(group_sizes))]
    groups = [group_stages(list(range(lo, lo + sz))) for lo, sz in zip(starts, group_sizes)]
    n_stages = len(groups[0])
    slots = n_stages - 1 + GROUP_LAG * (len(groups) - 1)
    per_slot = -(-n_chunks // max(slots - 3, 1))
    for t in range(slots):
        tail_steps(per_slot)
        if t == slots - 3:
            tail_steps(n_chunks)
            epilogue_mean()
        if t == slots - 2:
            epilogue_var()
        for gi, stages in enumerate(groups):
            k_stage = t - GROUP_LAG * gi
            if 0 <= k_stage < n_stages - 1:
                stages[k_stage]()
    epilogue_store()
    stash_bonus_gate()
    for stages in groups:
        stages[-1]()


def _chunk_cumsum(x):
    t = lax.broadcasted_iota(jnp.int32, x.shape, 0) % CHUNK
    sh = 1
    while sh < CHUNK:
        x = x + jnp.where(t >= sh, pltpu.roll(x, sh, axis=0), 0.0)
        sh *= 2
    return x


def _scan_layer(r, lw, k, v, an, bb, g, r_k, lnx_w, lnx_b, *, side=(), group_sizes=(8, 4, 4)):
    B, S, D = r.shape
    if sum(group_sizes) > S // CHUNK:
        group_sizes = (S // CHUNK,)
    n_chunks = sum(group_sizes)
    assert S % (n_chunks * CHUNK) == 0 and D % LANES == 0
    L = n_chunks * CHUNK
    nb, nj = S // L, D // LANES
    n_blocks = B * nj * nb

    def coords(f):
        return f // (nj * nb), f % nb, (f // nb) % nj

    cur = lambda s: coords(jnp.minimum(s, n_blocks - 1))
    prev = lambda s: coords(jnp.maximum(s - 1, 0))
    blk_in = pl.BlockSpec((1, L, LANES), cur)
    vec_in = pl.BlockSpec((1, LANES), lambda s: (0, cur(s)[2]))
    vec_out = pl.BlockSpec((1, LANES), lambda s: (0, prev(s)[2]))
    const = lambda n: pl.BlockSpec((n, n), lambda s: (0, 0))
    two = 2 * CHUNK
    side_in, side_out, side_shapes = _side_cast_specs(side, n_blocks + 1, lambda s: s)
    return pl.pallas_call(
        functools.partial(_scan_kernel, group_sizes=group_sizes, blocks_per_seq=nb),
        out_shape=[jax.ShapeDtypeStruct((B, S, D), BF16)] + side_shapes,
        grid=(n_blocks + 1,),
        in_specs=[blk_in] * 7 + [vec_in, vec_out, vec_out] + [const(LANES), const(LANES)] + side_in,
        out_specs=[pl.BlockSpec((1, L, LANES), prev)] + side_out,
        scratch_shapes=[
            pltpu.VMEM((LANES, LANES), F32),
            pltpu.VMEM((L, LANES), F32),
            pltpu.VMEM((n_chunks, CHUNK + two, LANES), BF16),
            pltpu.VMEM((n_chunks, CHUNK, LANES), F32),
            pltpu.VMEM((n_chunks, two, LANES), F32),
            pltpu.VMEM((n_chunks, two, LANES), F32),
            pltpu.VMEM((2, L, LANES), F32),
        ],
        compiler_params=_params(("arbitrary",)),
        name="rwkv_scan",
    )(r, lw, k, v, an, bb, g, r_k.reshape(1, D), lnx_w.reshape(1, D), lnx_b.reshape(1, D),
      _head_sum_matrix(LANES, 1.0 / HEAD), _head_sum_matrix(LANES), *[w for w, _ in side])


def _oproj_kernel(z_ref, w_ref, h_ref, o_ref):
    o_ref[...] = h_ref[...] + _dot(z_ref[...], w_ref[...])


def _oproj_layer(z, w, h, *, tm=512, tn=2048):
    T, D = h.shape
    tm, tn = min(tm, T), min(tn, D)
    return pl.pallas_call(
        _oproj_kernel,
        out_shape=jax.ShapeDtypeStruct((T, D), F32),
        grid=(T // tm, D // tn),
        in_specs=[
            pl.BlockSpec((tm, D), lambda i, n: (i, 0)),
            pl.BlockSpec((D, tn), lambda i, n: (0, n)),
            pl.BlockSpec((tm, tn), lambda i, n: (i, n)),
        ],
        out_specs=pl.BlockSpec((tm, tn), lambda i, n: (i, n)),
        compiler_params=_params(("parallel", "parallel")),
        name="rwkv_oproj",
    )(z, w, h)


def kernel(x, norm1_g, norm2_g, final_g, pool_w, pool_b, pool_scale, rw_mu, rw_r, rw_k, rw_v, rw_o, rw_w0, rw_w_la, rw_w_lb, rw_a0, rw_a_la, rw_a_lb, rw_g_la, rw_g_lb, rw_k_k, rw_k_a, rw_r_k, rw_lnx_w, rw_lnx_b, ffn_w1, ffn_w3, ffn_w2):
    B, S, D = x.shape
    T = B * S
    bf = lambda w: w.astype(BF16)

    h, w1, w3, w2 = _pool_layer(x, norm1_g[0], bf(pool_w[0]), pool_b[0].reshape(-1), pool_scale[0],
                                side=((ffn_w1, 0), (ffn_w3, 0), (ffn_w2, 0)))
    h, w_r, w_k, w_v, w_o = _ffn_layer(
        h.reshape(T, D), norm2_g[0], w1, w3, w2, final_g,
        final_norm=False, side=((rw_r, 0), (rw_k, 0), (rw_v, 0), (rw_o, 0)))

    r, lw, k, v, an, bb, g = _proj_layer(
        h, S, norm1_g[1], rw_mu[0], w_r, w_k, w_v, rw_w0[0],
        bf(rw_w_la[0]), bf(rw_w_lb[0]), rw_a0[0], bf(rw_a_la[0]), bf(rw_a_lb[0]),
        bf(rw_g_la[0]), bf(rw_g_lb[0]), rw_k_k[0], rw_k_a[0])
    s3 = lambda t: t.reshape(B, S, D)
    z, w1, w3, w2 = _scan_layer(s3(r), s3(lw), s3(k), s3(v), s3(an), s3(bb), s3(g),
                                rw_r_k[0].reshape(-1), rw_lnx_w[0], rw_lnx_b[0],
                                side=((ffn_w1, 1), (ffn_w3, 1), (ffn_w2, 1)))
    h = _oproj_layer(z.reshape(T, D), w_o, h)
    h, = _ffn_layer(h, norm2_g[1], w1, w3, w2, final_g, final_norm=True)
    return h.reshape(B, S, D)
```

```python
import functools

import jax
import jax.numpy as jnp
from jax import lax
from jax.experimental import pallas as pl
from jax.experimental.pallas import tpu as pltpu

F32 = jnp.float32
BF16 = jnp.bfloat16

RMS_EPS = 1e-6
GN_EPS = 64e-5
L2_EPS = 1e-12
DECAY_SCALE = 0.6065306597126334
POOL_WINDOWS = (2, 4, 8, 16)
POOL_HALO = 16
HEAD = 64
LANES = 128
SUBLANES = 8
MXU_WIDTH = 256
CHUNK = 64
GROUP_LAG = 1
VMEM_LIMIT = 56 * 1024 * 1024


def _rms(x, g):
    return x * lax.rsqrt(jnp.mean(x * x, axis=-1, keepdims=True) + RMS_EPS) * g


def _dot(a, b):
    return jnp.dot(a, b, preferred_element_type=F32)


def _dot_nt(a, b):
    return lax.dot_general(a, b, (((1,), (1,)), ((), ())), preferred_element_type=F32)


def _split_dot(x, w, pieces=2):
    acc = None
    rem = x
    for _ in range(pieces):
        p = rem.astype(BF16)
        rem = rem - p.astype(F32)
        t = _dot(p, w)
        acc = t if acc is None else acc + t
    return acc


def _params(sem):
    return pltpu.CompilerParams(dimension_semantics=sem, vmem_limit_bytes=VMEM_LIMIT)


def _pool_kernel(x_ref, xprev_ref, g_ref, w_ref, b_ref, sc_ref, *rest, ts, gdim):
    n_side = (len(rest) - 1) // 2
    side_in, o_ref, side_out = rest[:n_side], rest[n_side], rest[n_side + 1:]
    _side_casts(side_in, side_out)
    i = pl.program_id(1)
    g = g_ref[...]
    xc = x_ref[0]
    hn = _rms(xc, g)
    hp = _rms(xprev_ref[0], g)
    hp = jnp.where(i == 0, 0.0, hp)
    ext = jnp.concatenate([hp, hn], axis=0)
    rows = ts + POOL_HALO
    tau = lax.broadcasted_iota(jnp.int32, (ts, gdim), 0) + i * ts
    for gi, win in enumerate(POOL_WINDOWS):
        lo = gi * gdim
        e = ext[:, lo:lo + gdim]
        acc = e
        step = 1
        while step < win:
            acc = acc + pltpu.roll(acc, step, axis=0)
            step *= 2
        wsum = acc[POOL_HALO:rows]
        cnt = jnp.minimum(tau + 1, win).astype(F32)
        pooled = wsum / cnt - hn[:, lo:lo + gdim]
        mixed = _dot(pooled.astype(BF16), w_ref[gi]) + b_ref[:, lo:lo + gdim]
        o_ref[0, :, lo:lo + gdim] = xc[:, lo:lo + gdim] + mixed * sc_ref[:, lo:lo + gdim]


def _pool_layer(x, g, w_bf, b, scale, *, side=(), ts=512):
    B, S, D = x.shape
    G, C, _ = w_bf.shape
    ts = min(ts, S)
    halo_blocks = ts // POOL_HALO
    n_s = S // ts
    side_in, side_out, side_shapes = _side_cast_specs(side, B * n_s, lambda b, i: b * n_s + i)
    return pl.pallas_call(
        functools.partial(_pool_kernel, ts=ts, gdim=C),
        out_shape=[jax.ShapeDtypeStruct((B, S, D), F32)] + side_shapes,
        grid=(B, n_s),
        in_specs=[
            pl.BlockSpec((1, ts, D), lambda b, i: (b, i, 0)),
            pl.BlockSpec((1, POOL_HALO, D), lambda b, i: (b, jnp.maximum(i * halo_blocks - 1, 0), 0)),
            pl.BlockSpec((1, D), lambda b, i: (0, 0)),
            pl.BlockSpec((G, C, C), lambda b, i: (0, 0, 0)),
            pl.BlockSpec((1, D), lambda b, i: (0, 0)),
            pl.BlockSpec((1, D), lambda b, i: (0, 0)),
        ] + side_in,
        out_specs=[pl.BlockSpec((1, ts, D), lambda b, i: (b, i, 0))] + side_out,
        compiler_params=_params(("arbitrary", "arbitrary")),
        name="pool_mixer",
    )(x, x, g.reshape(1, D), w_bf, b.reshape(1, D), scale.reshape(1, D), *[w for w, _ in side])


def _ffn_kernel(h_ref, g_ref, w1_ref, w3_ref, w2_ref, fg_ref, *rest, n_i, n_f, final_norm, tn):
    n_side = (len(rest) - 4) // 2
    side_in, o_hbm, side_out = rest[:n_side], rest[n_side], rest[n_side + 1:2 * n_side + 1]
    hn_ref, acc_ref, sem = rest[2 * n_side + 1:]
    _side_casts(side_in, side_out)
    i = pl.program_id(0)
    f = pl.program_id(1)
    tm, d = acc_ref.shape
    slabs = [slice(n, n + tn) for n in range(0, d, tn)]

    def out_copy(k, tile):
        return pltpu.make_async_copy(acc_ref.at[:, slabs[k]], o_hbm.at[pl.ds(tile * tm, tm), slabs[k]],
                                     sem.at[k])

    def wait_out(tile):
        for k in range(len(slabs)):
            out_copy(k, tile).wait()

    @pl.when(f == 0)
    def _():
        hn_ref[...] = _rms(h_ref[...], g_ref[...]).astype(BF16)

        @pl.when(i > 0)
        def _():
            wait_out(i - 1)
        acc_ref[...] = h_ref[...]

    def body(is_last):
        hn = hn_ref[...]
        a = _dot(hn, w1_ref[...])
        b = _dot(hn, w3_ref[...])
        act = (a * jax.nn.sigmoid(a) * b).astype(BF16)
        for k, cs in enumerate(slabs):
            acc_ref[:, cs] += _dot(act, w2_ref[:, cs])
            if is_last and not final_norm:
                out_copy(k, i).start()
        if is_last and final_norm:
            x = acc_ref[...]
            scale = lax.rsqrt(jnp.mean(x * x, axis=-1, keepdims=True) + RMS_EPS)
            for k, cs in enumerate(slabs):
                acc_ref[:, cs] = x[:, cs] * scale * fg_ref[:, cs]
                out_copy(k, i).start()

    @pl.when(f < n_f - 1)
    def _():
        body(False)

    @pl.when(f == n_f - 1)
    def _():
        body(True)

        @pl.when(i == n_i - 1)
        def _():
            wait_out(i)


def _ffn_layer(h, g, w1, w3, w2, final_g, *, final_norm, side=(), tm=1024, tf=512, tn=512):
    T, D = h.shape
    F = w1.shape[1]
    tm, tf, tn = min(tm, T), min(tf, F), min(tn, D)
    while F % tf:
        tf -= LANES
    assert T % tm == 0 and D % tn == 0 and tf > 0
    n_f = F // tf
    n_i = T // tm
    side_in, side_out, side_shapes = _side_cast_specs(side, n_i * n_f, lambda i, f: i * n_f + f)
    outs = pl.pallas_call(
        functools.partial(_ffn_kernel, n_i=n_i, n_f=n_f, final_norm=final_norm, tn=tn),
        out_shape=[jax.ShapeDtypeStruct((T, D), F32)] + side_shapes,
        grid=(n_i, n_f),
        in_specs=[
            pl.BlockSpec((tm, D), lambda i, f: (i, 0)),
            pl.BlockSpec((1, D), lambda i, f: (0, 0)),
            pl.BlockSpec((D, tf), lambda i, f: (0, f)),
            pl.BlockSpec((D, tf), lambda i, f: (0, f)),
            pl.BlockSpec((tf, D), lambda i, f: (f, 0)),
            pl.BlockSpec((1, D), lambda i, f: (0, 0)),
        ] + side_in,
        out_specs=[pl.BlockSpec(memory_space=pl.ANY)] + side_out,
        scratch_shapes=[pltpu.VMEM((tm, D), BF16), pltpu.VMEM((tm, D), F32),
                        pltpu.SemaphoreType.DMA((D // tn,))],
        compiler_params=_params(("arbitrary", "arbitrary")),
        name="ffn_final" if final_norm else "ffn",
    )(h, g.reshape(1, D), w1, w3, w2, final_g.reshape(1, D), *[w for w, _ in side])
    return outs


def _side_casts(side_in, side_out):
    for w_ref, o_ref in zip(side_in, side_out):
        o_ref[...] = w_ref[0].astype(o_ref.dtype)


def _side_cast_specs(side, n_steps, flat_step):
    in_specs, out_specs, shapes = [], [], []
    for w, layer in side:
        _, R, C = w.shape
        tr = next(t for t in range(16, R + 1, 16) if R % t == 0 and R // t <= n_steps)
        last = R // tr - 1

        def blk(*idx, last=last):
            return jnp.minimum(flat_step(*idx), last)

        in_specs.append(pl.BlockSpec((1, tr, C), lambda *idx, blk=blk, layer=layer: (layer, blk(*idx), 0)))
        out_specs.append(pl.BlockSpec((tr, C), lambda *idx, blk=blk: (blk(*idx), 0)))
        shapes.append(jax.ShapeDtypeStruct((R, C), BF16))
    return in_specs, out_specs, shapes


def _proj_kernel(h_ref, hprev_ref, g_ref, mu_ref, wla_ref, ala_ref, gla_ref,
                 wr_ref, wk_ref, wv_ref, lb_ref, vec_ref, hsum_ref,
                 r_out, lw_out, k_out, v_out, an_out, bb_out, g_out,
                 xr_s, xk_s, xv_s, tw_s, ta_s, tg_s, *, tm, seq, n_tiles, n_steps):
    i = pl.program_id(0)
    n = pl.program_id(1)
    rq = tm // n_steps
    dl, da = tw_s.shape[-1], ta_s.shape[-1]
    slab = hsum_ref.shape[0]
    slabs = [slice(lo, lo + slab) for lo in range(0, r_out.shape[1], slab)]

    def prep():
        slot = i % 2
        r0 = pl.multiple_of(n * rq, rq)
        dst = pl.ds(r0, rq)
        g = g_ref[...]
        hn = _rms(h_ref[dst, :], g)
        inside = h_ref[pl.ds(pl.multiple_of(jnp.maximum(r0 - SUBLANES, 0), SUBLANES), SUBLANES), :]
        prev8 = jnp.where(n == 0, hprev_ref[...], inside)
        hp = _rms(prev8[SUBLANES - 1:SUBLANES, :], g)
        hp = jnp.where((n == 0) & ((i * tm) % seq == 0), 0.0, hp)
        row = lax.broadcasted_iota(jnp.int32, hn.shape, 0)
        shifted = jnp.where(row == 0, hp, pltpu.roll(hn, 1, axis=0))
        xx = shifted - hn
        xr_s[slot, dst, :] = (hn + xx * mu_ref[0:1, :]).astype(BF16)
        xk_s[slot, dst, :] = (hn + xx * mu_ref[2:3, :]).astype(BF16)
        xv_s[slot, dst, :] = (hn + xx * mu_ref[3:4, :]).astype(BF16)
        xw = (hn + xx * mu_ref[1:2, :]).astype(BF16)
        tw_s[slot, dst, :] = jnp.tanh(_dot(xw, wla_ref[...])).astype(BF16)
        xa = (hn + xx * mu_ref[4:5, :]).astype(BF16)
        ta_s[slot, dst, :] = _dot(xa, ala_ref[...]).astype(BF16)
        xg = (hn + xx * mu_ref[5:6, :]).astype(BF16)
        tg_s[slot, dst, :] = jax.nn.sigmoid(_dot(xg, gla_ref[...])).astype(BF16)

    def project(between):
        slot = (i + 1) % 2
        units = [(pl.ds(0, tm), cs) for cs in slabs]

        def matmuls(rows, cs):
            r = _dot(xr_s[slot, rows, :], wr_ref[:, cs])
            k = _dot(xk_s[slot, rows, :], wk_ref[:, cs])
            v = _dot(xv_s[slot, rows, :], wv_ref[:, cs])
            wl = _dot(tw_s[slot, rows, :], lb_ref[0:dl, cs])
            al = _dot(ta_s[slot, rows, :], lb_ref[dl:dl + da, cs])
            gg = _dot(tg_s[slot, rows, :], lb_ref[dl + da:, cs])
            return r, k, v, wl, al, gg

        def tail(rows, cs, r, k, v, wl, al, gg):
            w0, a0, k_k, k_a = (vec_ref[j:j + 1, cs] for j in range(4))
            lw_out[rows, cs] = -DECAY_SCALE * jax.nn.sigmoid(wl + w0)
            a_sig = jax.nn.sigmoid(al + a0)
            kk = k * k_k
            ss = _split_dot(kk * kk, hsum_ref[...])
            kk = kk * jnp.minimum(lax.rsqrt(ss), 1.0 / L2_EPS)
            r_out[rows, cs] = r.astype(r_out.dtype)
            k_out[rows, cs] = (k * (1.0 + (a_sig - 1.0) * k_a)).astype(k_out.dtype)
            v_out[rows, cs] = v.astype(v_out.dtype)
            an_out[rows, cs] = (-kk).astype(an_out.dtype)
            bb_out[rows, cs] = (kk * a_sig).astype(bb_out.dtype)
            g_out[rows, cs] = gg.astype(g_out.dtype)

        pending = None
        for u, (rows, cs) in enumerate(units):
            res = matmuls(rows, cs)
            if pending is not None:
                tail(*pending)
            pending = (rows, cs) + res
            if u == 0:
                between()
        tail(*pending)

    @pl.when(i == 0)
    def _():
        prep()

    @pl.when((i > 0) & (i < n_tiles))
    def _():
        project(prep)

    @pl.when(i == n_tiles)
    def _():
        project(lambda: None)


def _head_sum_matrix(n, value=1.0):
    hi = lax.broadcasted_iota(jnp.int32, (n, n), 0) // HEAD
    hj = lax.broadcasted_iota(jnp.int32, (n, n), 1) // HEAD
    return jnp.where(hi == hj, value, 0.0).astype(BF16)


def _proj_layer(h, seq, g, mu, w_r, w_k, w_v, w0, w_la, w_lb, a0, a_la, a_lb, g_la, g_lb, k_k, k_a,
                *, tm=512, tn=512):
    T, D = h.shape
    tm, tn = min(tm, T), min(tn, D)
    slab = min(MXU_WIDTH, tn)
    n_tiles, n_steps = T // tm, D // tn
    dl, al, gl = w_la.shape[1], a_la.shape[1], g_la.shape[1]
    row = lambda x: x.reshape(1, D)
    pad8 = lambda x: jnp.pad(x, ((0, SUBLANES - x.shape[0]), (0, 0)))
    full = lambda shp: pl.BlockSpec(shp, lambda i, n: (0, 0))
    col = lambda rows: pl.BlockSpec((rows, tn), lambda i, n: (0, n))
    out_spec = pl.BlockSpec((tm, tn), lambda i, n: (jnp.maximum(i - 1, 0), jnp.where(i == 0, 0, n)))
    prev_blocks = tm // SUBLANES
    tile = lambda i: jnp.minimum(i, n_tiles - 1)
    out_dtypes = [BF16, F32, BF16, BF16, BF16, BF16, BF16]
    outs = pl.pallas_call(
        functools.partial(_proj_kernel, tm=tm, seq=seq, n_tiles=n_tiles, n_steps=n_steps),
        out_shape=[jax.ShapeDtypeStruct((T, D), dt) for dt in out_dtypes],
        grid=(n_tiles + 1, n_steps),
        in_specs=[
            pl.BlockSpec((tm, D), lambda i, n: (tile(i), 0)),
            pl.BlockSpec((SUBLANES, D), lambda i, n: (jnp.maximum(tile(i) * prev_blocks - 1, 0), 0)),
            full((1, D)), full((SUBLANES, D)), full((D, dl)), full((D, al)), full((D, gl)),
            col(D), col(D), col(D), col(dl + al + gl), col(SUBLANES),
            full((slab, slab)),
        ],
        out_specs=[out_spec] * 7,
        scratch_shapes=[pltpu.VMEM((2, tm, D), BF16)] * 3
        + [pltpu.VMEM((2, tm, dl), BF16), pltpu.VMEM((2, tm, al), BF16), pltpu.VMEM((2, tm, gl), BF16)],
        compiler_params=_params(("arbitrary", "arbitrary")),
        name="rwkv_proj",
    )(h, h, row(g), pad8(mu), w_la, a_la, g_la,
      w_r, w_k, w_v, jnp.concatenate([w_lb, a_lb, g_lb], axis=0), pad8(jnp.stack([w0, a0, k_k, k_a])),
      _head_sum_matrix(slab))
    return outs


def _scan_kernel(r_ref, lw_ref, k_ref, v_ref, an_ref, bb_ref, g_ref, rk_ref, lnw_ref, lnb_ref,
                 hmean_ref, hsum_ref, *rest, group_sizes, blocks_per_seq):
    n_side = (len(rest) - 8) // 2
    side_in, z_ref, side_out = rest[:n_side], rest[n_side], rest[n_side + 1:2 * n_side + 1]
    s_ref, y_ref, qg_s, yl_s, nm_s, ec_s, bg_s = rest[2 * n_side + 1:]
    _side_casts(side_in, side_out)
    step = pl.program_id(0)
    n_chunks = sum(group_sizes)

    @pl.when(step == 0)
    def _():
        for ref in (s_ref, qg_s, yl_s, nm_s, ec_s, bg_s):
            ref[...] = jnp.zeros_like(ref)

    two = 2 * CHUNK
    lane_head0 = lax.broadcasted_iota(jnp.int32, (CHUNK, LANES), 1) < HEAD
    ri = lax.broadcasted_iota(jnp.int32, (two, two), 0)
    ci = lax.broadcasted_iota(jnp.int32, (two, two), 1)
    same = (ri >= CHUNK) == (ci >= CHUNK)
    strict = same & (ci < ri)
    incl = same & (ci <= ri)
    eye = jnp.where(ri == ci, 1.0, 0.0).astype(F32)

    def stack(x):
        return jnp.concatenate([jnp.where(lane_head0, x, 0.0), jnp.where(lane_head0, 0.0, x)], axis=0)

    def unstack(x):
        return x[:CHUNK] + x[CHUNK:]

    sls = [pl.ds(c * CHUNK, CHUNK) for c in range(n_chunks)]

    prev_first = (step + blocks_per_seq - 1) % blocks_per_seq == 0
    carry = {"state": jnp.where(prev_first, 0.0, s_ref[...]), "next": 0}

    def tail_steps(count):
        for c in range(carry["next"], min(carry["next"] + count, n_chunks)):
            state = carry["state"]
            res = _dot(qg_s[c], state.astype(BF16))
            y_ref[sls[c], :] = res[:CHUNK] + yl_s[c]
            carry["state"] = state * ec_s[c] + res[CHUNK:] + nm_s[c]
            carry["next"] = c + 1

    def group_stages(chunks):
        n = len(chunks)
        rng = range(n)
        v = {}

        def front():
            rows = pl.ds(chunks[0] * CHUNK, n * CHUNK)
            lw_all = lw_ref[0, rows, :]
            cum_all = _chunk_cumsum(lw_all)
            e_in_all = jnp.exp(cum_all)
            e_ex_all = jnp.exp(cum_all - lw_all)
            e_neg_all = jnp.exp(-cum_all)
            v["a_s"], v["r_t"], v["v_sb"], v["bk_t"], v["e_tot"], v["gram"] = [], [], [], [], [], []
            for c in rng:
                sl = slice(c * CHUNK, (c + 1) * CHUNK)
                src = sls[chunks[c]]
                cum = cum_all[sl]
                e_rem = jnp.exp(cum[CHUNK - 1:CHUNK, :] - cum)
                k = k_ref[0, src, :]
                bb = bb_ref[0, src, :]
                a_s = stack(an_ref[0, src, :] * e_ex_all[sl]).astype(BF16)
                r_t = r_ref[0, src, :] * e_in_all[sl]
                v["a_s"].append(a_s)
                v["r_t"].append(r_t)
                v["v_sb"].append(stack(v_ref[0, src, :]).astype(BF16))
                v["bk_t"].append(
                    jnp.concatenate([stack(bb * e_rem).T, stack(k * e_rem).T], axis=1).astype(BF16))
                v["e_tot"].append(e_in_all[sl][CHUNK - 1:CHUNK, :])
                b_t = (bb * e_neg_all[sl]).astype(BF16)
                k_t = (k * e_neg_all[sl]).astype(BF16)
                lhs = jnp.concatenate([a_s, stack(r_t).astype(BF16)], axis=0)
                rhs = jnp.concatenate([b_t, b_t, k_t, k_t], axis=0)
                v["gram"].append(_dot_nt(lhs, rhs))

        def masks():
            gram = v.pop("gram")
            p_0 = [jnp.where(strict, gm[:two, :two], 0.0) for gm in gram]
            v["a_34"] = [jnp.concatenate([jnp.where(incl, gm[two:, :two], 0.0),
                                          jnp.where(incl, gm[two:, two:], 0.0)], axis=1).astype(BF16)
                         for gm in gram]
            v["a2v"] = [_dot(jnp.where(strict, gram[c][:two, two:], 0.0).astype(BF16),
                             v["v_sb"][c]).astype(BF16) for c in rng]
            v["t_m"] = [eye + p for p in p_0]
            v["p_b"] = [p.astype(BF16) for p in p_0]

        def square():
            v["p_b"] = [_dot(pb, pb).astype(BF16) for pb in v["p_b"]]

        def doubling():
            p_b = v["p_b"]
            out = [_dot(jnp.concatenate([v["t_m"][c].astype(BF16), p_b[c]], axis=0), p_b[c]) for c in rng]
            v["t_m"] = [v["t_m"][c] + out[c][:two] for c in rng]
            v["p_b"] = [o[two:].astype(BF16) for o in out]

        def last_doubling():
            v["t_m"] = [v["t_m"][c] + _dot(v["t_m"][c].astype(BF16), v["p_b"][c]) for c in rng]

        def solve():
            v["x_b"] = [_dot(v["t_m"][c].astype(BF16),
                             jnp.concatenate([v["a_s"][c], v["a2v"][c]], axis=1)).astype(BF16)
                        for c in rng]

        def handover():
            for c in rng:
                v_sb = v["v_sb"][c]
                rhs = jnp.concatenate([v["x_b"][c], jnp.concatenate([jnp.zeros_like(v_sb), v_sb], axis=1)],
                                      axis=0)
                out = _dot(jnp.concatenate([v["a_34"][c], v["bk_t"][c]], axis=0), rhs)
                dst = chunks[c]
                q_m = (v["r_t"][c] + unstack(out[:two, :LANES])).astype(BF16)
                qg_s[dst] = jnp.concatenate([q_m, out[two:, :LANES].astype(BF16)], axis=0)
                yl_s[dst] = unstack(out[:two, LANES:])
                nm_s[dst] = out[two:, LANES:]
                e_col = jnp.broadcast_to(v["e_tot"][c], (SUBLANES, LANES)).T[:, 0:1]
                ec_s[dst] = jnp.broadcast_to(e_col, (two, LANES))

        n_rounds = CHUNK.bit_length() - 2
        return [front, masks, square] + [doubling] * (n_rounds - 1) + [last_doubling, solve, handover]

    ep = {}

    def epilogue_mean():
        s_ref[...] = carry["state"]
        y = y_ref[...]
        ep["d"] = y - _split_dot(y, hmean_ref[...])

    def epilogue_var():
        ep["var"] = _split_dot(ep["d"] * ep["d"], hmean_ref[...])

    def epilogue_store():
        yn = ep["d"] * lax.rsqrt(ep["var"] + GN_EPS) * lnw_ref[...] + lnb_ref[...]
        z_ref[0] = ((yn + bg_s[0]) * bg_s[1]).astype(z_ref.dtype)

    def stash_bonus_gate():
        rk = r_ref[0].astype(F32) * k_ref[0].astype(F32) * rk_ref[...]
        bg_s[0] = _split_dot(rk, hsum_ref[...]) * v_ref[0].astype(F32)
        bg_s[1] = g_ref[0].astype(F32)

    starts = [sum(group_sizes[:gi]) for gi in range(len(group_sizes))]
    groups = [group_stages(list(range(lo, lo + sz))) for lo, sz in zip(starts, group_sizes)]
    n_stages = len(groups[0])
    slots = n_stages - 1 + GROUP_LAG * (len(groups) - 1)
    per_slot = -(-n_chunks // max(slots - 3, 1))
    for t in range(slots):
        tail_steps(per_slot)
        if t == slots - 3:
            tail_steps(n_chunks)
            epilogue_mean()
        if t == slots - 2:
            epilogue_var()
        for gi, stages in enumerate(groups):
            k_stage = t - GROUP_LAG * gi
            if 0 <= k_stage < n_stages - 1:
                stages[k_stage]()
    epilogue_store()
    stash_bonus_gate()
    for stages in groups:
        stages[-1]()


def _chunk_cumsum(x):
    t = lax.broadcasted_iota(jnp.int32, x.shape, 0) % CHUNK
    sh = 1
    while sh < CHUNK:
        x = x + jnp.where(t >= sh, pltpu.roll(x, sh, axis=0), 0.0)
        sh *= 2
    return x


def _scan_layer(r, lw, k, v, an, bb, g, r_k, lnx_w, lnx_b, *, side=(), group_sizes=(8, 4, 4)):
    B, S, D = r.shape
    if sum(group_sizes) > S // CHUNK:
        group_sizes = (S // CHUNK,)
    n_chunks = sum(group_sizes)
    assert S % (n_chunks * CHUNK) == 0 and D % LANES == 0
    L = n_chunks * CHUNK
    nb, nj = S // L, D // LANES
    n_blocks = B * nj * nb

    def coords(f):
        return f // (nj * nb), f % nb, (f // nb) % nj

    cur = lambda s: coords(jnp.minimum(s, n_blocks - 1))
    prev = lambda s: coords(jnp.maximum(s - 1, 0))
    blk_in = pl.BlockSpec((1, L, LANES), cur)
    vec_in = pl.BlockSpec((1, LANES), lambda s: (0, cur(s)[2]))
    vec_out = pl.BlockSpec((1, LANES), lambda s: (0, prev(s)[2]))
    const = lambda n: pl.BlockSpec((n, n), lambda s: (0, 0))
    two = 2 * CHUNK
    side_in, side_out, side_shapes = _side_cast_specs(side, n_blocks + 1, lambda s: s)
    return pl.pallas_call(
        functools.partial(_scan_kernel, group_sizes=group_sizes, blocks_per_seq=nb),
        out_shape=[jax.ShapeDtypeStruct((B, S, D), BF16)] + side_shapes,
        grid=(n_blocks + 1,),
        in_specs=[blk_in] * 7 + [vec_in, vec_out, vec_out] + [const(LANES), const(LANES)] + side_in,
        out_specs=[pl.BlockSpec((1, L, LANES), prev)] + side_out,
        scratch_shapes=[
            pltpu.VMEM((LANES, LANES), F32),
            pltpu.VMEM((L, LANES), F32),
            pltpu.VMEM((n_chunks, CHUNK + two, LANES), BF16),
            pltpu.VMEM((n_chunks, CHUNK, LANES), F32),
            pltpu.VMEM((n_chunks, two, LANES), F32),
            pltpu.VMEM((n_chunks, two, LANES), F32),
            pltpu.VMEM((2, L, LANES), F32),
        ],
        compiler_params=_params(("arbitrary",)),
        name="rwkv_scan",
    )(r, lw, k, v, an, bb, g, r_k.reshape(1, D), lnx_w.reshape(1, D), lnx_b.reshape(1, D),
      _head_sum_matrix(LANES, 1.0 / HEAD), _head_sum_matrix(LANES), *[w for w, _ in side])


def _oproj_kernel(z_ref, w_ref, h_ref, o_ref):
    o_ref[...] = h_ref[...] + _dot(z_ref[...], w_ref[...])


def _oproj_layer(z, w, h, *, tm=512, tn=2048):
    T, D = h.shape
    tm, tn = min(tm, T), min(tn, D)
    return pl.pallas_call(
        _oproj_kernel,
        out_shape=jax.ShapeDtypeStruct((T, D), F32),
        grid=(T // tm, D // tn),
        in_specs=[
            pl.BlockSpec((tm, D), lambda i, n: (i, 0)),
            pl.BlockSpec((D, tn), lambda i, n: (0, n)),
            pl.BlockSpec((tm, tn), lambda i, n: (i, n)),
        ],
        out_specs=pl.BlockSpec((tm, tn), lambda i, n: (i, n)),
        compiler_params=_params(("parallel", "parallel")),
        name="rwkv_oproj",
    )(z, w, h)


def kernel(x, norm1_g, norm2_g, final_g, pool_w, pool_b, pool_scale, rw_mu, rw_r, rw_k, rw_v, rw_o, rw_w0, rw_w_la, rw_w_lb, rw_a0, rw_a_la, rw_a_lb, rw_g_la, rw_g_lb, rw_k_k, rw_k_a, rw_r_k, rw_lnx_w, rw_lnx_b, ffn_w1, ffn_w3, ffn_w2):
    B, S, D = x.shape
    T = B * S
    bf = lambda w: w.astype(BF16)

    h, w1, w3, w2 = _pool_layer(x, norm1_g[0], bf(pool_w[0]), pool_b[0].reshape(-1), pool_scale[0],
                                side=((ffn_w1, 0), (ffn_w3, 0), (ffn_w2, 0)))
    h, w_r, w_k, w_v, w_o = _ffn_layer(
        h.reshape(T, D), norm2_g[0], w1, w3, w2, final_g,
        final_norm=False, side=((rw_r, 0), (rw_k, 0), (rw_v, 0), (rw_o, 0)))

    r, lw, k, v, an, bb, g = _proj_layer(
        h, S, norm1_g[1], rw_mu[0], w_r, w_k, w_v, rw_w0[0],
        bf(rw_w_la[0]), bf(rw_w_lb[0]), rw_a0[0], bf(rw_a_la[0]), bf(rw_a_lb[0]),
        bf(rw_g_la[0]), bf(rw_g_lb[0]), rw_k_k[0], rw_k_a[0])
    s3 = lambda t: t.reshape(B, S, D)
    z, w1, w3, w2 = _scan_layer(s3(r), s3(lw), s3(k), s3(v), s3(an), s3(bb), s3(g),
                                rw_r_k[0].reshape(-1), rw_lnx_w[0], rw_lnx_b[0],
                                side=((ffn_w1, 1), (ffn_w3, 1), (ffn_w2, 1)))
    h = _oproj_layer(z.reshape(T, D), w_o, h)
    h, = _ffn_layer(h, norm2_g[1], w1, w3, w2, final_g, final_norm=True)
    return h.reshape(B, S, D)
```

```python
import functools

import jax
import jax.numpy as jnp
from jax import lax
from jax.experimental import pallas as pl
from jax.experimental.pallas import tpu as pltpu

F32 = jnp.float32
BF16 = jnp.bfloat16

RMS_EPS = 1e-6
GN_EPS = 64e-5
L2_EPS = 1e-12
DECAY_SCALE = 0.6065306597126334
POOL_WINDOWS = (2, 4, 8, 16)
POOL_HALO = 16
HEAD = 64
LANES = 128
SUBLANES = 8
MXU_WIDTH = 256
CHUNK = 64
GROUP_LAG = 1
VMEM_LIMIT = 56 * 1024 * 1024


def _rms(x, g):
    return x * lax.rsqrt(jnp.mean(x * x, axis=-1, keepdims=True) + RMS_EPS) * g


def _dot(a, b):
    return jnp.dot(a, b, preferred_element_type=F32)


def _dot_nt(a, b):
    return lax.dot_general(a, b, (((1,), (1,)), ((), ())), preferred_element_type=F32)


def _split_dot(x, w, pieces=2):
    acc = None
    rem = x
    for _ in range(pieces):
        p = rem.astype(BF16)
        rem = rem - p.astype(F32)
        t = _dot(p, w)
        acc = t if acc is None else acc + t
    return acc


def _params(sem):
    return pltpu.CompilerParams(dimension_semantics=sem, vmem_limit_bytes=VMEM_LIMIT)


def _pool_kernel(x_ref, xprev_ref, g_ref, w_ref, b_ref, sc_ref, *rest, ts, gdim):
    n_side = (len(rest) - 1) // 2
    side_in, o_ref, side_out = rest[:n_side], rest[n_side], rest[n_side + 1:]
    _side_casts(side_in, side_out)
    i = pl.program_id(1)
    g = g_ref[...]
    xc = x_ref[0]
    hn = _rms(xc, g)
    hp = _rms(xprev_ref[0], g)
    hp = jnp.where(i == 0, 0.0, hp)
    ext = jnp.concatenate([hp, hn], axis=0)
    rows = ts + POOL_HALO
    tau = lax.broadcasted_iota(jnp.int32, (ts, gdim), 0) + i * ts
    for gi, win in enumerate(POOL_WINDOWS):
        lo = gi * gdim
        e = ext[:, lo:lo + gdim]
        acc = e
        step = 1
        while step < win:
            acc = acc + pltpu.roll(acc, step, axis=0)
            step *= 2
        wsum = acc[POOL_HALO:rows]
        cnt = jnp.minimum(tau + 1, win).astype(F32)
        pooled = wsum / cnt - hn[:, lo:lo + gdim]
        mixed = _dot(pooled.astype(BF16), w_ref[gi]) + b_ref[:, lo:lo + gdim]
        o_ref[0, :, lo:lo + gdim] = xc[:, lo:lo + gdim] + mixed * sc_ref[:, lo:lo + gdim]


def _pool_layer(x, g, w_bf, b, scale, *, side=(), ts=512):
    B, S, D = x.shape
    G, C, _ = w_bf.shape
    ts = min(ts, S)
    halo_blocks = ts // POOL_HALO
    n_s = S // ts
    side_in, side_out, side_shapes = _side_cast_specs(side, B * n_s, lambda b, i: b * n_s + i)
    return pl.pallas_call(
        functools.partial(_pool_kernel, ts=ts, gdim=C),
        out_shape=[jax.ShapeDtypeStruct((B, S, D), F32)] + side_shapes,
        grid=(B, n_s),
        in_specs=[
            pl.BlockSpec((1, ts, D), lambda b, i: (b, i, 0)),
            pl.BlockSpec((1, POOL_HALO, D), lambda b, i: (b, jnp.maximum(i * halo_blocks - 1, 0), 0)),
            pl.BlockSpec((1, D), lambda b, i: (0, 0)),
            pl.BlockSpec((G, C, C), lambda b, i: (0, 0, 0)),
            pl.BlockSpec((1, D), lambda b, i: (0, 0)),
            pl.BlockSpec((1, D), lambda b, i: (0, 0)),
        ] + side_in,
        out_specs=[pl.BlockSpec((1, ts, D), lambda b, i: (b, i, 0))] + side_out,
        compiler_params=_params(("arbitrary", "arbitrary")),
        name="pool_mixer",
    )(x, x, g.reshape(1, D), w_bf, b.reshape(1, D), scale.reshape(1, D), *[w for w, _ in side])


def _ffn_kernel(h_ref, g_ref, w1_ref, w3_ref, w2_ref, fg_ref, *rest, n_i, n_f, final_norm, tn):
    n_side = (len(rest) - 4) // 2
    side_in, o_hbm, side_out = rest[:n_side], rest[n_side], rest[n_side + 1:2 * n_side + 1]
    hn_ref, acc_ref, sem = rest[2 * n_side + 1:]
    _side_casts(side_in, side_out)
    i = pl.program_id(0)
    f = pl.program_id(1)
    tm, d = acc_ref.shape
    slabs = [slice(n, n + tn) for n in range(0, d, tn)]

    def out_copy(k, tile):
        return pltpu.make_async_copy(acc_ref.at[:, slabs[k]], o_hbm.at[pl.ds(tile * tm, tm), slabs[k]],
                                     sem.at[k])

    def wait_out(tile):
        for k in range(len(slabs)):
            out_copy(k, tile).wait()

    @pl.when(f == 0)
    def _():
        hn_ref[...] = _rms(h_ref[...], g_ref[...]).astype(BF16)

        @pl.when(i > 0)
        def _():
            wait_out(i - 1)

    def body(is_last, is_first=False):
        hn = hn_ref[...]
        a = _dot(hn, w1_ref[...])
        b = _dot(hn, w3_ref[...])
        act = (a * jax.nn.sigmoid(a) * b).astype(BF16)
        base = h_ref if is_first else acc_ref
        for k, cs in enumerate(slabs):
            acc_ref[:, cs] = base[:, cs] + _dot(act, w2_ref[:, cs])
            if is_last and not final_norm:
                out_copy(k, i).start()
        if is_last and final_norm:
            x = acc_ref[...]
            scale = lax.rsqrt(jnp.mean(x * x, axis=-1, keepdims=True) + RMS_EPS)
            for k, cs in enumerate(slabs):
                acc_ref[:, cs] = x[:, cs] * scale * fg_ref[:, cs]
                out_copy(k, i).start()

    assert n_f >= 2

    @pl.when(f == 0)
    def _():
        body(False, is_first=True)

    @pl.when((f > 0) & (f < n_f - 1))
    def _():
        body(False)

    @pl.when(f == n_f - 1)
    def _():
        body(True)

        @pl.when(i == n_i - 1)
        def _():
            wait_out(i)


def _ffn_layer(h, g, w1, w3, w2, final_g, *, final_norm, side=(), tm=1024, tf=512, tn=512):
    T, D = h.shape
    F = w1.shape[1]
    tm, tf, tn = min(tm, T), min(tf, F), min(tn, D)
    while F % tf:
        tf -= LANES
    assert T % tm == 0 and D % tn == 0 and tf > 0
    n_f = F // tf
    n_i = T // tm
    side_in, side_out, side_shapes = _side_cast_specs(side, n_i * n_f, lambda i, f: i * n_f + f)
    outs = pl.pallas_call(
        functools.partial(_ffn_kernel, n_i=n_i, n_f=n_f, final_norm=final_norm, tn=tn),
        out_shape=[jax.ShapeDtypeStruct((T, D), F32)] + side_shapes,
        grid=(n_i, n_f),
        in_specs=[
            pl.BlockSpec((tm, D), lambda i, f: (i, 0)),
            pl.BlockSpec((1, D), lambda i, f: (0, 0)),
            pl.BlockSpec((D, tf), lambda i, f: (0, f)),
            pl.BlockSpec((D, tf), lambda i, f: (0, f)),
            pl.BlockSpec((tf, D), lambda i, f: (f, 0)),
            pl.BlockSpec((1, D), lambda i, f: (0, 0)),
        ] + side_in,
        out_specs=[pl.BlockSpec(memory_space=pl.ANY)] + side_out,
        scratch_shapes=[pltpu.VMEM((tm, D), BF16), pltpu.VMEM((tm, D), F32),
                        pltpu.SemaphoreType.DMA((D // tn,))],
        compiler_params=_params(("arbitrary", "arbitrary")),
        name="ffn_final" if final_norm else "ffn",
    )(h, g.reshape(1, D), w1, w3, w2, final_g.reshape(1, D), *[w for w, _ in side])
    return outs


def _side_casts(side_in, side_out):
    for w_ref, o_ref in zip(side_in, side_out):
        o_ref[...] = w_ref[0].astype(o_ref.dtype)


def _side_cast_specs(side, n_steps, flat_step):
    in_specs, out_specs, shapes = [], [], []
    for w, layer in side:
        _, R, C = w.shape
        tr = next(t for t in range(16, R + 1, 16) if R % t == 0 and R // t <= n_steps)
        last = R // tr - 1

        def blk(*idx, last=last):
            return jnp.minimum(flat_step(*idx), last)

        in_specs.append(pl.BlockSpec((1, tr, C), lambda *idx, blk=blk, layer=layer: (layer, blk(*idx), 0)))
        out_specs.append(pl.BlockSpec((tr, C), lambda *idx, blk=blk: (blk(*idx), 0)))
        shapes.append(jax.ShapeDtypeStruct((R, C), BF16))
    return in_specs, out_specs, shapes


def _proj_kernel(h_ref, hprev_ref, g_ref, mu_ref, wla_ref, ala_ref, gla_ref,
                 wr_ref, wk_ref, wv_ref, lb_ref, vec_ref, hsum_ref,
                 r_out, lw_out, k_out, v_out, an_out, bb_out, g_out,
                 xr_s, xk_s, xv_s, tw_s, ta_s, tg_s, *, tm, seq, n_tiles, n_steps):
    i = pl.program_id(0)
    n = pl.program_id(1)
    rq = tm // n_steps
    dl, da = tw_s.shape[-1], ta_s.shape[-1]
    slab = hsum_ref.shape[0]
    slabs = [slice(lo, lo + slab) for lo in range(0, r_out.shape[1], slab)]

    def prep():
        slot = i % 2
        r0 = pl.multiple_of(n * rq, rq)
        dst = pl.ds(r0, rq)
        g = g_ref[...]
        hn = _rms(h_ref[dst, :], g)
        inside = h_ref[pl.ds(pl.multiple_of(jnp.maximum(r0 - SUBLANES, 0), SUBLANES), SUBLANES), :]
        prev8 = jnp.where(n == 0, hprev_ref[...], inside)
        hp = _rms(prev8[SUBLANES - 1:SUBLANES, :], g)
        hp = jnp.where((n == 0) & ((i * tm) % seq == 0), 0.0, hp)
        row = lax.broadcasted_iota(jnp.int32, hn.shape, 0)
        shifted = jnp.where(row == 0, hp, pltpu.roll(hn, 1, axis=0))
        xx = shifted - hn
        xr_s[slot, dst, :] = (hn + xx * mu_ref[0:1, :]).astype(BF16)
        xk_s[slot, dst, :] = (hn + xx * mu_ref[2:3, :]).astype(BF16)
        xv_s[slot, dst, :] = (hn + xx * mu_ref[3:4, :]).astype(BF16)
        xw = (hn + xx * mu_ref[1:2, :]).astype(BF16)
        tw_s[slot, dst, :] = jnp.tanh(_dot(xw, wla_ref[...])).astype(BF16)
        xa = (hn + xx * mu_ref[4:5, :]).astype(BF16)
        ta_s[slot, dst, :] = _dot(xa, ala_ref[...]).astype(BF16)
        xg = (hn + xx * mu_ref[5:6, :]).astype(BF16)
        tg_s[slot, dst, :] = jax.nn.sigmoid(_dot(xg, gla_ref[...])).astype(BF16)

    def project(between):
        slot = (i + 1) % 2
        units = [(pl.ds(0, tm), cs) for cs in slabs]

        def matmuls(rows, cs):
            r = _dot(xr_s[slot, rows, :], wr_ref[:, cs])
            k = _dot(xk_s[slot, rows, :], wk_ref[:, cs])
            v = _dot(xv_s[slot, rows, :], wv_ref[:, cs])
            wl = _dot(tw_s[slot, rows, :], lb_ref[0:dl, cs])
            al = _dot(ta_s[slot, rows, :], lb_ref[dl:dl + da, cs])
            gg = _dot(tg_s[slot, rows, :], lb_ref[dl + da:, cs])
            return r, k, v, wl, al, gg

        def tail(rows, cs, r, k, v, wl, al, gg):
            w0, a0, k_k, k_a = (vec_ref[j:j + 1, cs] for j in range(4))
            lw_out[rows, cs] = -DECAY_SCALE * jax.nn.sigmoid(wl + w0)
            a_sig = jax.nn.sigmoid(al + a0)
            kk = k * k_k
            ss = _split_dot(kk * kk, hsum_ref[...])
            kk = kk * jnp.minimum(lax.rsqrt(ss), 1.0 / L2_EPS)
            r_out[rows, cs] = r.astype(r_out.dtype)
            k_out[rows, cs] = (k * (1.0 + (a_sig - 1.0) * k_a)).astype(k_out.dtype)
            v_out[rows, cs] = v.astype(v_out.dtype)
            an_out[rows, cs] = (-kk).astype(an_out.dtype)
            bb_out[rows, cs] = (kk * a_sig).astype(bb_out.dtype)
            g_out[rows, cs] = gg.astype(g_out.dtype)

        pending = None
        for u, (rows, cs) in enumerate(units):
            res = matmuls(rows, cs)
            if pending is not None:
                tail(*pending)
            pending = (rows, cs) + res
            if u == 0:
                between()
        tail(*pending)

    @pl.when(i == 0)
    def _():
        prep()

    @pl.when((i > 0) & (i < n_tiles))
    def _():
        project(prep)

    @pl.when(i == n_tiles)
    def _():
        project(lambda: None)


def _head_sum_matrix(n, value=1.0):
    hi = lax.broadcasted_iota(jnp.int32, (n, n), 0) // HEAD
    hj = lax.broadcasted_iota(jnp.int32, (n, n), 1) // HEAD
    return jnp.where(hi == hj, value, 0.0).astype(BF16)


def _proj_layer(h, seq, g, mu, w_r, w_k, w_v, w0, w_la, w_lb, a0, a_la, a_lb, g_la, g_lb, k_k, k_a,
                *, tm=512, tn=512):
    T, D = h.shape
    tm, tn = min(tm, T), min(tn, D)
    slab = min(MXU_WIDTH, tn)
    n_tiles, n_steps = T // tm, D // tn
    dl, al, gl = w_la.shape[1], a_la.shape[1], g_la.shape[1]
    row = lambda x: x.reshape(1, D)
    pad8 = lambda x: jnp.pad(x, ((0, SUBLANES - x.shape[0]), (0, 0)))
    full = lambda shp: pl.BlockSpec(shp, lambda i, n: (0, 0))
    col = lambda rows: pl.BlockSpec((rows, tn), lambda i, n: (0, n))
    out_spec = pl.BlockSpec((tm, tn), lambda i, n: (jnp.maximum(i - 1, 0), jnp.where(i == 0, 0, n)))
    prev_blocks = tm // SUBLANES
    tile = lambda i: jnp.minimum(i, n_tiles - 1)
    out_dtypes = [BF16, F32, BF16, BF16, BF16, BF16, BF16]
    outs = pl.pallas_call(
        functools.partial(_proj_kernel, tm=tm, seq=seq, n_tiles=n_tiles, n_steps=n_steps),
        out_shape=[jax.ShapeDtypeStruct((T, D), dt) for dt in out_dtypes],
        grid=(n_tiles + 1, n_steps),
        in_specs=[
            pl.BlockSpec((tm, D), lambda i, n: (tile(i), 0)),
            pl.BlockSpec((SUBLANES, D), lambda i, n: (jnp.maximum(tile(i) * prev_blocks - 1, 0), 0)),
            full((1, D)), full((SUBLANES, D)), full((D, dl)), full((D, al)), full((D, gl)),
            col(D), col(D), col(D), col(dl + al + gl), col(SUBLANES),
            full((slab, slab)),
        ],
        out_specs=[out_spec] * 7,
        scratch_shapes=[pltpu.VMEM((2, tm, D), BF16)] * 3
        + [pltpu.VMEM((2, tm, dl), BF16), pltpu.VMEM((2, tm, al), BF16), pltpu.VMEM((2, tm, gl), BF16)],
        compiler_params=_params(("arbitrary", "arbitrary")),
        name="rwkv_proj",
    )(h, h, row(g), pad8(mu), w_la, a_la, g_la,
      w_r, w_k, w_v, jnp.concatenate([w_lb, a_lb, g_lb], axis=0), pad8(jnp.stack([w0, a0, k_k, k_a])),
      _head_sum_matrix(slab))
    return outs


def _scan_kernel(r_ref, lw_ref, k_ref, v_ref, an_ref, bb_ref, g_ref, rk_ref, lnw_ref, lnb_ref,
                 hmean_ref, hsum_ref, *rest, group_sizes, blocks_per_seq):
    n_side = (len(rest) - 8) // 2
    side_in, z_ref, side_out = rest[:n_side], rest[n_side], rest[n_side + 1:2 * n_side + 1]
    s_ref, y_ref, qg_s, yl_s, nm_s, ec_s, bg_s = rest[2 * n_side + 1:]
    _side_casts(side_in, side_out)
    step = pl.program_id(0)
    n_chunks = sum(group_sizes)

    @pl.when(step == 0)
    def _():
        for ref in (s_ref, qg_s, yl_s, nm_s, ec_s, bg_s):
            ref[...] = jnp.zeros_like(ref)

    two = 2 * CHUNK
    lane_head0 = lax.broadcasted_iota(jnp.int32, (CHUNK, LANES), 1) < HEAD
    ri = lax.broadcasted_iota(jnp.int32, (two, two), 0)
    ci = lax.broadcasted_iota(jnp.int32, (two, two), 1)
    same = (ri >= CHUNK) == (ci >= CHUNK)
    strict = same & (ci < ri)
    incl = same & (ci <= ri)
    eye = jnp.where(ri == ci, 1.0, 0.0).astype(F32)

    def stack(x):
        return jnp.concatenate([jnp.where(lane_head0, x, 0.0), jnp.where(lane_head0, 0.0, x)], axis=0)

    def unstack(x):
        return x[:CHUNK] + x[CHUNK:]

    sls = [pl.ds(c * CHUNK, CHUNK) for c in range(n_chunks)]

    prev_first = (step + blocks_per_seq - 1) % blocks_per_seq == 0
    carry = {"state": jnp.where(prev_first, 0.0, s_ref[...]), "next": 0}

    def tail_steps(count):
        for c in range(carry["next"], min(carry["next"] + count, n_chunks)):
            state = carry["state"]
            res = _dot(qg_s[c], state.astype(BF16))
            y_ref[sls[c], :] = res[:CHUNK] + yl_s[c]
            carry["state"] = state * ec_s[c] + res[CHUNK:] + nm_s[c]
            carry["next"] = c + 1

    def group_stages(chunks):
        n = len(chunks)
        rng = range(n)
        v = {}

        def front():
            rows = pl.ds(chunks[0] * CHUNK, n * CHUNK)
            lw_all = lw_ref[0, rows, :]
            cum_all = _chunk_cumsum(lw_all)
            e_in_all = jnp.exp(cum_all)
            e_ex_all = jnp.exp(cum_all - lw_all)
            e_neg_all = jnp.exp(-cum_all)
            v["a_s"], v["r_t"], v["v_sb"], v["bk_t"], v["e_tot"], v["gram"] = [], [], [], [], [], []
            for c in rng:
                sl = slice(c * CHUNK, (c + 1) * CHUNK)
                src = sls[chunks[c]]
                cum = cum_all[sl]
                e_rem = jnp.exp(cum[CHUNK - 1:CHUNK, :] - cum)
                k = k_ref[0, src, :]
                bb = bb_ref[0, src, :]
                a_s = stack(an_ref[0, src, :] * e_ex_all[sl]).astype(BF16)
                r_t = r_ref[0, src, :] * e_in_all[sl]
                v["a_s"].append(a_s)
                v["r_t"].append(r_t)
                v["v_sb"].append(stack(v_ref[0, src, :]).astype(BF16))
                v["bk_t"].append(
                    jnp.concatenate([stack(bb * e_rem).T, stack(k * e_rem).T], axis=1).astype(BF16))
                v["e_tot"].append(e_in_all[sl][CHUNK - 1:CHUNK, :])
                b_t = (bb * e_neg_all[sl]).astype(BF16)
                k_t = (k * e_neg_all[sl]).astype(BF16)
                lhs = jnp.concatenate([a_s, stack(r_t).astype(BF16)], axis=0)
                rhs = jnp.concatenate([b_t, b_t, k_t, k_t], axis=0)
                v["gram"].append(_dot_nt(lhs, rhs))

        def masks():
            gram = v.pop("gram")
            p_0 = [jnp.where(strict, gm[:two, :two], 0.0) for gm in gram]
            v["a_34"] = [jnp.concatenate([jnp.where(incl, gm[two:, :two], 0.0),
                                          jnp.where(incl, gm[two:, two:], 0.0)], axis=1).astype(BF16)
                         for gm in gram]
            v["a2v"] = [_dot(jnp.where(strict, gram[c][:two, two:], 0.0).astype(BF16),
                             v["v_sb"][c]).astype(BF16) for c in rng]
            v["t_m"] = [eye + p for p in p_0]
            v["p_b"] = [p.astype(BF16) for p in p_0]

        def square():
            v["p_b"] = [_dot(pb, pb).astype(BF16) for pb in v["p_b"]]

        def doubling():
            p_b = v["p_b"]
            out = [_dot(jnp.concatenate([v["t_m"][c].astype(BF16), p_b[c]], axis=0), p_b[c]) for c in rng]
            v["t_m"] = [v["t_m"][c] + out[c][:two] for c in rng]
            v["p_b"] = [o[two:].astype(BF16) for o in out]

        def last_doubling():
            v["t_m"] = [v["t_m"][c] + _dot(v["t_m"][c].astype(BF16), v["p_b"][c]) for c in rng]

        def solve():
            v["x_b"] = [_dot(v["t_m"][c].astype(BF16),
                             jnp.concatenate([v["a_s"][c], v["a2v"][c]], axis=1)).astype(BF16)
                        for c in rng]

        def handover():
            for c in rng:
                v_sb = v["v_sb"][c]
                rhs = jnp.concatenate([v["x_b"][c], jnp.concatenate([jnp.zeros_like(v_sb), v_sb], axis=1)],
                                      axis=0)
                out = _dot(jnp.concatenate([v["a_34"][c], v["bk_t"][c]], axis=0), rhs)
                dst = chunks[c]
                q_m = (v["r_t"][c] + unstack(out[:two, :LANES])).astype(BF16)
                qg_s[dst] = jnp.concatenate([q_m, out[two:, :LANES].astype(BF16)], axis=0)
                yl_s[dst] = unstack(out[:two, LANES:])
                nm_s[dst] = out[two:, LANES:]
                e_col = jnp.broadcast_to(v["e_tot"][c], (SUBLANES, LANES)).T[:, 0:1]
                ec_s[dst] = jnp.broadcast_to(e_col, (two, LANES))

        n_rounds = CHUNK.bit_length() - 2
        return [front, masks, square] + [doubling] * (n_rounds - 1) + [last_doubling, solve, handover]

    ep = {}

    def epilogue_mean():
        s_ref[...] = carry["state"]
        y = y_ref[...]
        ep["d"] = y - _split_dot(y, hmean_ref[...])

    def epilogue_var():
        ep["var"] = _split_dot(ep["d"] * ep["d"], hmean_ref[...])

    def epilogue_store():
        yn = ep["d"] * lax.rsqrt(ep["var"] + GN_EPS) * lnw_ref[...] + lnb_ref[...]
        z_ref[0] = ((yn + bg_s[0]) * bg_s[1]).astype(z_ref.dtype)

    def stash_bonus_gate():
        rk = r_ref[0].astype(F32) * k_ref[0].astype(F32) * rk_ref[...]
        bg_s[0] = _split_dot(rk, hsum_ref[...]) * v_ref[0].astype(F32)
        bg_s[1] = g_ref[0].astype(F32)

    starts = [sum(group_sizes[:gi]) for gi in range(len(group_sizes))]
    groups = [group_stages(list(range(lo, lo + sz))) for lo, sz in zip(starts, group_sizes)]
    n_stages = len(groups[0])
    slots = n_stages - 1 + GROUP_LAG * (len(groups) - 1)
    per_slot = -(-n_chunks // max(slots - 3, 1))
    for t in range(slots):
        tail_steps(per_slot)
        if t == slots - 3:
            tail_steps(n_chunks)
            epilogue_mean()
        if t == slots - 2:
            epilogue_var()
        for gi, stages in enumerate(groups):
            k_stage = t - GROUP_LAG * gi
            if 0 <= k_stage < n_stages - 1:
                stages[k_stage]()
    epilogue_store()
    stash_bonus_gate()
    for stages in groups:
        stages[-1]()


def _chunk_cumsum(x):
    t = lax.broadcasted_iota(jnp.int32, x.shape, 0) % CHUNK
    sh = 1
    while sh < CHUNK:
        x = x + jnp.where(t >= sh, pltpu.roll(x, sh, axis=0), 0.0)
        sh *= 2
    return x


def _scan_layer(r, lw, k, v, an, bb, g, r_k, lnx_w, lnx_b, *, side=(), group_sizes=(8, 4, 4)):
    B, S, D = r.shape
    if sum(group_sizes) > S // CHUNK:
        group_sizes = (S // CHUNK,)
    n_chunks = sum(group_sizes)
    assert S % (n_chunks * CHUNK) == 0 and D % LANES == 0
    L = n_chunks * CHUNK
    nb, nj = S // L, D // LANES
    n_blocks = B * nj * nb

    def coords(f):
        return f // (nj * nb), f % nb, (f // nb) % nj

    cur = lambda s: coords(jnp.minimum(s, n_blocks - 1))
    prev = lambda s: coords(jnp.maximum(s - 1, 0))
    blk_in = pl.BlockSpec((1, L, LANES), cur)
    vec_in = pl.BlockSpec((1, LANES), lambda s: (0, cur(s)[2]))
    vec_out = pl.BlockSpec((1, LANES), lambda s: (0, prev(s)[2]))
    const = lambda n: pl.BlockSpec((n, n), lambda s: (0, 0))
    two = 2 * CHUNK
    side_in, side_out, side_shapes = _side_cast_specs(side, n_blocks + 1, lambda s: s)
    return pl.pallas_call(
        functools.partial(_scan_kernel, group_sizes=group_sizes, blocks_per_seq=nb),
        out_shape=[jax.ShapeDtypeStruct((B, S, D), BF16)] + side_shapes,
        grid=(n_blocks + 1,),
        in_specs=[blk_in] * 7 + [vec_in, vec_out, vec_out] + [const(LANES), const(LANES)] + side_in,
        out_specs=[pl.BlockSpec((1, L, LANES), prev)] + side_out,
        scratch_shapes=[
            pltpu.VMEM((LANES, LANES), F32),
            pltpu.VMEM((L, LANES), F32),
            pltpu.VMEM((n_chunks, CHUNK + two, LANES), BF16),
            pltpu.VMEM((n_chunks, CHUNK, LANES), F32),
            pltpu.VMEM((n_chunks, two, LANES), F32),
            pltpu.VMEM((n_chunks, two, LANES), F32),
            pltpu.VMEM((2, L, LANES), F32),
        ],
        compiler_params=_params(("arbitrary",)),
        name="rwkv_scan",
    )(r, lw, k, v, an, bb, g, r_k.reshape(1, D), lnx_w.reshape(1, D), lnx_b.reshape(1, D),
      _head_sum_matrix(LANES, 1.0 / HEAD), _head_sum_matrix(LANES), *[w for w, _ in side])


def _oproj_kernel(z_ref, w_ref, h_ref, o_ref):
    o_ref[...] = h_ref[...] + _dot(z_ref[...], w_ref[...])


def _oproj_layer(z, w, h, *, tm=512, tn=2048):
    T, D = h.shape
    tm, tn = min(tm, T), min(tn, D)
    return pl.pallas_call(
        _oproj_kernel,
        out_shape=jax.ShapeDtypeStruct((T, D), F32),
        grid=(T // tm, D // tn),
        in_specs=[
            pl.BlockSpec((tm, D), lambda i, n: (i, 0)),
            pl.BlockSpec((D, tn), lambda i, n: (0, n)),
            pl.BlockSpec((tm, tn), lambda i, n: (i, n)),
        ],
        out_specs=pl.BlockSpec((tm, tn), lambda i, n: (i, n)),
        compiler_params=_params(("parallel", "parallel")),
        name="rwkv_oproj",
    )(z, w, h)


def kernel(x, norm1_g, norm2_g, final_g, pool_w, pool_b, pool_scale, rw_mu, rw_r, rw_k, rw_v, rw_o, rw_w0, rw_w_la, rw_w_lb, rw_a0, rw_a_la, rw_a_lb, rw_g_la, rw_g_lb, rw_k_k, rw_k_a, rw_r_k, rw_lnx_w, rw_lnx_b, ffn_w1, ffn_w3, ffn_w2):
    B, S, D = x.shape
    T = B * S
    bf = lambda w: w.astype(BF16)

    h, w1, w3, w2 = _pool_layer(x, norm1_g[0], bf(pool_w[0]), pool_b[0].reshape(-1), pool_scale[0],
                                side=((ffn_w1, 0), (ffn_w3, 0), (ffn_w2, 0)))
    h, w_r, w_k, w_v, w_o = _ffn_layer(
        h.reshape(T, D), norm2_g[0], w1, w3, w2, final_g,
        final_norm=False, side=((rw_r, 0), (rw_k, 0), (rw_v, 0), (rw_o, 0)))

    r, lw, k, v, an, bb, g = _proj_layer(
        h, S, norm1_g[1], rw_mu[0], w_r, w_k, w_v, rw_w0[0],
        bf(rw_w_la[0]), bf(rw_w_lb[0]), rw_a0[0], bf(rw_a_la[0]), bf(rw_a_lb[0]),
        bf(rw_g_la[0]), bf(rw_g_lb[0]), rw_k_k[0], rw_k_a[0])
    s3 = lambda t: t.reshape(B, S, D)
    z, w1, w3, w2 = _scan_layer(s3(r), s3(lw), s3(k), s3(v), s3(an), s3(bb), s3(g),
                                rw_r_k[0].reshape(-1), rw_lnx_w[0], rw_lnx_b[0],
                                side=((ffn_w1, 1), (ffn_w3, 1), (ffn_w2, 1)))
    h = _oproj_layer(z.reshape(T, D), w_o, h)
    h, = _ffn_layer(h, norm2_g[1], w1, w3, w2, final_g, final_norm=True)
    return h.reshape(B, S, D)
```
